```python
import math
import jax
import jax.numpy as jnp
from jax import lax
import numpy as np

D_MODEL = 1024
BATCH = 4
SEQ = 8192
DEPTH = 1

N_MEM = 256
Q_BLOCK = 128
RMS_EPS = 1e-6
NEG_INF = -1e30
ROPE_THETA = 10000.0
POS_OFFSET_MAX = 1024

DIFF_HEADS = 8
DIFF_HALF_DIM = 32
DIFF_V_DIM = 2 * DIFF_HALF_DIM
DIFF_QK_WIDTH = DIFF_HEADS * 2 * DIFF_HALF_DIM
DIFF_WIDTH = DIFF_HEADS * DIFF_V_DIM

MLA_HEADS = 8
MLA_Q_RANK = 256
MLA_KV_RANK = 128
MLA_NOPE_DIM = 64
MLA_ROPE_DIM = 32
MLA_QK_DIM = MLA_NOPE_DIM + MLA_ROPE_DIM
MLA_V_DIM = 64
MLA_WIDTH = MLA_HEADS * MLA_V_DIM

MIX_WIDTH = DIFF_WIDTH + MLA_WIDTH
IN_SPLITS = (DIFF_QK_WIDTH,
             2 * DIFF_QK_WIDTH,
             2 * DIFF_QK_WIDTH + DIFF_WIDTH,
             2 * DIFF_QK_WIDTH + DIFF_WIDTH + MLA_Q_RANK,
             2 * DIFF_QK_WIDTH + DIFF_WIDTH + MLA_Q_RANK + MLA_KV_RANK)
IN_PROJ_WIDTH = IN_SPLITS[-1] + MLA_ROPE_DIM

MEM_HEADS = 4
MEM_HEAD_DIM = 128
MEM_WIDTH = MEM_HEADS * MEM_HEAD_DIM

N_GROUPS = 4
EXPERTS_PER_GROUP = 8
N_EXPERTS = N_GROUPS * EXPERTS_PER_GROUP
TOP_K_IN_GROUP = 2
EXPERT_FF = 256
EXPERT_BLOCK = 128

kernel_name = 'hybrid_diffattn_mla_hier_moe'


def _rms_norm(x, gain):
    xf = x.astype(jnp.float32)
    y = xf * lax.rsqrt(jnp.mean(xf * xf, axis=-1, keepdims=True) + RMS_EPS)
    return (y * gain.astype(jnp.float32)).astype(x.dtype)


def _alibi_slopes(n_heads):
    return 2.0 ** (-8.0 * jnp.arange(1, n_heads + 1, dtype=jnp.float32) / n_heads)


def _rope_angles(positions, dim):
    inv_freq = ROPE_THETA ** (-jnp.arange(0, dim, 2, dtype=jnp.float32) / dim)
    ang = positions.astype(jnp.float32)[..., None] * inv_freq
    return jnp.cos(ang), jnp.sin(ang)


def _apply_rope(x, cos, sin):
    half = x.shape[-1] // 2
    x1 = x[..., :half].astype(jnp.float32)
    x2 = x[..., half:].astype(jnp.float32)
    return jnp.concatenate([x1 * cos - x2 * sin, x2 * cos + x1 * sin], axis=-1).astype(x.dtype)


def _causal_mask(start, seq_len):
    q_idx = start + jnp.arange(Q_BLOCK)
    k_idx = jnp.arange(seq_len)
    return k_idx[None, :] <= q_idx[:, None]


def _query_block_sweep(block_fn, seq_len):
    n_blocks = seq_len // Q_BLOCK
    out = lax.map(block_fn, jnp.arange(n_blocks))
    nb, b, qb, w = out.shape
    return out.transpose(1, 0, 2, 3).reshape(b, nb * qb, w)


def _differential_attention(q, k, v, positions, lam, lambda_init, out_gain):
    b, s, h, _ = q.shape
    q = q.transpose(0, 2, 1, 3)
    k = k.transpose(0, 2, 1, 3)
    v = v.transpose(0, 2, 1, 3)
    q1, q2 = q[..., :DIFF_HALF_DIM], q[..., DIFF_HALF_DIM:]
    k1, k2 = k[..., :DIFF_HALF_DIM], k[..., DIFF_HALF_DIM:]
    slopes = _alibi_slopes(h)
    scale = DIFF_HALF_DIM ** -0.5
    pos_f = positions.astype(jnp.float32)
    gain = out_gain.astype(jnp.float32) * (1.0 - lambda_init)

    def block_fn(i):
        start = i * Q_BLOCK
        qb1 = lax.dynamic_slice_in_dim(q1, start, Q_BLOCK, axis=2)
        qb2 = lax.dynamic_slice_in_dim(q2, start, Q_BLOCK, axis=2)
        pos_q = lax.dynamic_slice_in_dim(pos_f, start, Q_BLOCK, axis=1)
        dist = jnp.abs(pos_q[:, :, None] - pos_f[:, None, :])
        bias = -slopes[None, :, None, None] * dist[:, None]
        mask = _causal_mask(start, s)

        def attn_map(qb, kk):
            sc = jnp.einsum('bhqd,bhkd->bhqk', qb, kk, preferred_element_type=jnp.float32) * scale + bias
            return jax.nn.softmax(jnp.where(mask, sc, NEG_INF), axis=-1)

        a = attn_map(qb1, k1) - lam * attn_map(qb2, k2)
        o = jnp.einsum('bhqk,bhkd->bhqd', a.astype(v.dtype), v, preferred_element_type=jnp.float32)
        o = o * lax.rsqrt(jnp.mean(o * o, axis=-1, keepdims=True) + RMS_EPS) * gain
        return o.transpose(0, 2, 1, 3).reshape(b, Q_BLOCK, h * DIFF_V_DIM).astype(v.dtype)

    return _query_block_sweep(block_fn, s)


def _latent_attention(q, k, v):
    b, s, h, _ = q.shape
    q = q.transpose(0, 2, 1, 3)
    k = k.transpose(0, 2, 1, 3)
    v = v.transpose(0, 2, 1, 3)
    scale = MLA_QK_DIM ** -0.5

    def block_fn(i):
        start = i * Q_BLOCK
        qb = lax.dynamic_slice_in_dim(q, start, Q_BLOCK, axis=2)
        sc = jnp.einsum('bhqd,bhkd->bhqk', qb, k, preferred_element_type=jnp.float32) * scale
        p = jax.nn.softmax(jnp.where(_causal_mask(start, s), sc, NEG_INF), axis=-1)
        o = jnp.einsum('bhqk,bhkd->bhqd', p.astype(v.dtype), v, preferred_element_type=jnp.float32)
        return o.transpose(0, 2, 1, 3).reshape(b, Q_BLOCK, h * MLA_V_DIM).astype(v.dtype)

    return _query_block_sweep(block_fn, s)


def _memory_cross_attention(h, mem_n, w_q, w_kv, w_o):
    b, s, _ = h.shape
    n_mem = mem_n.shape[1]
    q = (h @ w_q).reshape(b, s, MEM_HEADS, MEM_HEAD_DIM)
    kv = (mem_n @ w_kv).reshape(b, n_mem, 2, MEM_HEADS, MEM_HEAD_DIM)
    k, v = kv[:, :, 0], kv[:, :, 1]
    sc = jnp.einsum('bshd,bmhd->bhsm', q, k, preferred_element_type=jnp.float32) * MEM_HEAD_DIM ** -0.5
    p = jax.nn.softmax(sc, axis=-1)
    o = jnp.einsum('bhsm,bmhd->bshd', p.astype(v.dtype), v).reshape(b, s, MEM_WIDTH)
    return o @ w_o


def _hierarchical_moe(h, w_group_router, b_group_router, w_expert_router, b_expert_router,
                      w_gate, w_up, w_down):
    b, s, d = h.shape
    n_tok = b * s
    hf = h.reshape(n_tok, d)
    g_logits = jnp.einsum('td,dg->tg', hf, w_group_router, preferred_element_type=jnp.float32) \
        + b_group_router.astype(jnp.float32)
    g_probs = jax.nn.softmax(g_logits, axis=-1)
    g_gate, g_sel = lax.top_k(g_probs, 1)
    e_logits = jnp.einsum('td,de->te', hf, w_expert_router, preferred_element_type=jnp.float32) \
        + b_expert_router.astype(jnp.float32)
    e_logits = e_logits.reshape(n_tok, N_GROUPS, EXPERTS_PER_GROUP)
    e_logits = jnp.take_along_axis(e_logits, g_sel[:, :, None], axis=1)[:, 0]
    e_probs = jax.nn.softmax(e_logits, axis=-1)
    e_gate, e_sel = lax.top_k(e_probs, TOP_K_IN_GROUP)
    e_gate = e_gate / jnp.sum(e_gate, axis=-1, keepdims=True)
    weights = (g_gate * e_gate).reshape(-1)
    expert_ids = (g_sel * EXPERTS_PER_GROUP + e_sel).reshape(-1)
    token_ids = jnp.repeat(jnp.arange(n_tok, dtype=jnp.int32), TOP_K_IN_GROUP)

    n_assign = n_tok * TOP_K_IN_GROUP
    n_blocks = -(-(n_assign + N_EXPERTS * (EXPERT_BLOCK - 1)) // EXPERT_BLOCK)
    n_slots = n_blocks * EXPERT_BLOCK
    order = jnp.argsort(expert_ids)
    sorted_e = expert_ids[order]
    counts = jnp.bincount(expert_ids, length=N_EXPERTS)
    padded = ((counts + EXPERT_BLOCK - 1) // EXPERT_BLOCK) * EXPERT_BLOCK
    seg_start = jnp.cumsum(counts) - counts
    padded_end = jnp.cumsum(padded)
    padded_start = padded_end - padded
    dest = padded_start[sorted_e] + (jnp.arange(n_assign) - seg_start[sorted_e])
    slot_token = jnp.full((n_slots,), n_tok, jnp.int32).at[dest].set(token_ids[order])
    slot_weight = jnp.zeros((n_slots,), jnp.float32).at[dest].set(weights[order])
    block_expert = jnp.minimum(
        jnp.searchsorted(padded_end, jnp.arange(n_blocks) * EXPERT_BLOCK, side='right'), N_EXPERTS - 1)
    h_pad = jnp.concatenate([hf, jnp.zeros((1, d), hf.dtype)], axis=0)

    def block_fn(args):
        tok, wt, e = args
        xb = h_pad[tok]
        hidden = jax.nn.silu(xb @ w_gate[e]) * (xb @ w_up[e])
        return (hidden @ w_down[e]) * wt[:, None].astype(hf.dtype)

    y = lax.map(block_fn, (slot_token.reshape(n_blocks, EXPERT_BLOCK),
                           slot_weight.reshape(n_blocks, EXPERT_BLOCK),
                           block_expert))
    out = jnp.zeros((n_tok + 1, d), hf.dtype).at[slot_token].add(y.reshape(n_slots, d))[:n_tok]
    return out.reshape(b, s, d)


def setup_inputs(seed: int = 0) -> dict:
    key = jax.random.key(seed)
    ks = jax.random.split(key, 30)

    def w(k, shape, fan_in):
        return jax.random.normal(k, shape, jnp.float32) * fan_in ** -0.5

    def gain(k, shape):
        return 1.0 + 0.02 * jax.random.normal(k, shape, jnp.float32)

    x = jax.random.normal(ks[0], (BATCH, SEQ, D_MODEL), jnp.float32)
    mem = jax.random.normal(ks[1], (BATCH, N_MEM, D_MODEL), jnp.float32)
    offset = jax.random.randint(ks[2], (BATCH, 1), 0, POS_OFFSET_MAX, jnp.int32)
    positions = (offset + jnp.arange(SEQ, dtype=jnp.int32)[None, :]).astype(jnp.int32)
    return {
        'x': x,
        'mem': mem,
        'positions': positions,
        'norm_mix_g': gain(ks[3], (DEPTH, D_MODEL)),
        'w_in': w(ks[4], (DEPTH, D_MODEL, IN_PROJ_WIDTH), D_MODEL),
        'diff_lambda_q1': 0.1 * jax.random.normal(ks[5], (DEPTH, DIFF_HALF_DIM), jnp.float32),
        'diff_lambda_k1': 0.1 * jax.random.normal(ks[6], (DEPTH, DIFF_HALF_DIM), jnp.float32),
        'diff_lambda_q2': 0.1 * jax.random.normal(ks[7], (DEPTH, DIFF_HALF_DIM), jnp.float32),
        'diff_lambda_k2': 0.1 * jax.random.normal(ks[8], (DEPTH, DIFF_HALF_DIM), jnp.float32),
        'diff_out_g': gain(ks[9], (DEPTH, DIFF_V_DIM)),
        'mla_q_norm_g': gain(ks[10], (DEPTH, MLA_Q_RANK)),
        'w_mla_uq': w(ks[11], (DEPTH, MLA_Q_RANK, MLA_HEADS * MLA_QK_DIM), MLA_Q_RANK),
        'mla_kv_norm_g': gain(ks[12], (DEPTH, MLA_KV_RANK)),
        'w_mla_ukv': w(ks[13], (DEPTH, MLA_KV_RANK, MLA_HEADS * (MLA_NOPE_DIM + MLA_V_DIM)), MLA_KV_RANK),
        'mla_out_g': gain(ks[14], (DEPTH, MLA_WIDTH)),
        'w_out': w(ks[15], (DEPTH, MIX_WIDTH, D_MODEL), MIX_WIDTH),
        'norm_cross_g': gain(ks[16], (DEPTH, D_MODEL)),
        'norm_mem_g': gain(ks[17], (DEPTH, D_MODEL)),
        'w_mem_q': w(ks[18], (DEPTH, D_MODEL, MEM_WIDTH), D_MODEL),
        'w_mem_kv': w(ks[19], (DEPTH, D_MODEL, 2 * MEM_WIDTH), D_MODEL),
        'w_mem_o': w(ks[20], (DEPTH, MEM_WIDTH, D_MODEL), MEM_WIDTH),
        'norm_ffn_g': gain(ks[21], (DEPTH, D_MODEL)),
        'w_group_router': w(ks[22], (DEPTH, D_MODEL, N_GROUPS), D_MODEL),
        'b_group_router': 0.01 * jax.random.normal(ks[23], (DEPTH, N_GROUPS), jnp.float32),
        'w_expert_router': w(ks[24], (DEPTH, D_MODEL, N_EXPERTS), D_MODEL),
        'b_expert_router': 0.01 * jax.random.normal(ks[25], (DEPTH, N_EXPERTS), jnp.float32),
        'w_expert_gate': w(ks[26], (DEPTH, N_EXPERTS, D_MODEL, EXPERT_FF), D_MODEL),
        'w_expert_up': w(ks[27], (DEPTH, N_EXPERTS, D_MODEL, EXPERT_FF), D_MODEL),
        'w_expert_down': w(ks[28], (DEPTH, N_EXPERTS, EXPERT_FF, D_MODEL), EXPERT_FF),
        'norm_final_g': gain(ks[29], (D_MODEL,)),
    }


def reference(x, mem, positions, norm_mix_g, w_in, diff_lambda_q1, diff_lambda_k1,
              diff_lambda_q2, diff_lambda_k2, diff_out_g, mla_q_norm_g, w_mla_uq,
              mla_kv_norm_g, w_mla_ukv, mla_out_g, w_out, norm_cross_g, norm_mem_g,
              w_mem_q, w_mem_kv, w_mem_o, norm_ffn_g, w_group_router, b_group_router,
              w_expert_router, b_expert_router, w_expert_gate, w_expert_up, w_expert_down,
              norm_final_g):
    b, s, _ = x.shape
    cos, sin = _rope_angles(positions, MLA_ROPE_DIM)
    for l in range(DEPTH):
        h = _rms_norm(x, norm_mix_g[l])
        proj = h @ w_in[l]
        q_d, k_d, v_d, c_q, c_kv, k_pe = jnp.split(proj, IN_SPLITS, axis=-1)

        lambda_init = 0.8 - 0.6 * math.exp(-0.3 * l)
        lam = (jnp.exp(jnp.sum(diff_lambda_q1[l].astype(jnp.float32) * diff_lambda_k1[l].astype(jnp.float32)))
               - jnp.exp(jnp.sum(diff_lambda_q2[l].astype(jnp.float32) * diff_lambda_k2[l].astype(jnp.float32)))
               + lambda_init)
        diff_out = _differential_attention(
            q_d.reshape(b, s, DIFF_HEADS, 2 * DIFF_HALF_DIM),
            k_d.reshape(b, s, DIFF_HEADS, 2 * DIFF_HALF_DIM),
            v_d.reshape(b, s, DIFF_HEADS, DIFF_V_DIM),
            positions, lam, lambda_init, diff_out_g[l])

        q_m = (_rms_norm(c_q, mla_q_norm_g[l]) @ w_mla_uq[l]).reshape(b, s, MLA_HEADS, MLA_QK_DIM)
        q_nope, q_pe = q_m[..., :MLA_NOPE_DIM], q_m[..., MLA_NOPE_DIM:]
        q_pe = _apply_rope(q_pe, cos[:, :, None, :], sin[:, :, None, :])
        q_m = jnp.concatenate([q_nope, q_pe], axis=-1)
        kv_m = (_rms_norm(c_kv, mla_kv_norm_g[l]) @ w_mla_ukv[l]).reshape(
            b, s, MLA_HEADS, MLA_NOPE_DIM + MLA_V_DIM)
        k_nope, v_m = kv_m[..., :MLA_NOPE_DIM], kv_m[..., MLA_NOPE_DIM:]
        k_pe = _apply_rope(k_pe, cos, sin)
        k_m = jnp.concatenate(
            [k_nope, jnp.broadcast_to(k_pe[:, :, None, :], (b, s, MLA_HEADS, MLA_ROPE_DIM))], axis=-1)
        mla_out = _rms_norm(_latent_attention(q_m, k_m, v_m), mla_out_g[l])

        x = x + jnp.concatenate([diff_out, mla_out], axis=-1) @ w_out[l]

        x = x + _memory_cross_attention(_rms_norm(x, norm_cross_g[l]), _rms_norm(mem, norm_mem_g[l]),
                                        w_mem_q[l], w_mem_kv[l], w_mem_o[l])

        x = x + _hierarchical_moe(_rms_norm(x, norm_ffn_g[l]), w_group_router[l], b_group_router[l],
                                  w_expert_router[l], b_expert_router[l],
                                  w_expert_gate[l], w_expert_up[l], w_expert_down[l])
    return _rms_norm(x, norm_final_g)
```

```python
import functools
import math

import jax
import jax.numpy as jnp
from jax import lax
from jax.experimental import pallas as pl
from jax.experimental.pallas import tpu as pltpu

F32 = jnp.float32
BF16 = jnp.bfloat16

RMS_EPS = 1e-6
ROPE_THETA = 10000.0
MASK_VALUE = -1e30

DIFF_HEADS = 8
DIFF_HALF_DIM = 32
DIFF_V_DIM = 64
DIFF_WIDTH = DIFF_HEADS * DIFF_V_DIM

MLA_HEADS = 8
MLA_Q_RANK = 256
MLA_KV_RANK = 128
MLA_NOPE_DIM = 64
MLA_ROPE_DIM = 32
MLA_QK_DIM = MLA_NOPE_DIM + MLA_ROPE_DIM
MLA_V_DIM = 64
MLA_WIDTH = MLA_HEADS * MLA_V_DIM

MEM_HEADS = 4
MEM_HEAD_DIM = 128
MEM_WIDTH = MEM_HEADS * MEM_HEAD_DIM

N_GROUPS = 4
EXPERTS_PER_GROUP = 8
N_EXPERTS = N_GROUPS * EXPERTS_PER_GROUP
EXPERT_FF = 256

LANES = 128
MLA_SLAB = LANES
ROUTER_LANE0 = N_GROUPS

PROJ_ROWS = 512
MIX_ROWS = 256
MOE_ROWS = 256
EXPERT_ROWS = 256
DIFF_TQ, DIFF_TK = 128, 256
MLA_TQ, MLA_TK = 256, 256
VMEM_LIMIT = 48 * 1024 * 1024

_NT = (((1,), (1,)), ((), ()))


def _rms(x, gain):
    return x * lax.rsqrt(jnp.mean(x * x, axis=-1, keepdims=True) + RMS_EPS) * gain


def _dot(a, b):
    return jnp.dot(a, b, preferred_element_type=F32)


def _params(*sem):
    return pltpu.CompilerParams(dimension_semantics=sem, vmem_limit_bytes=VMEM_LIMIT)


def _proj_kernel(x_ref, g_ref, w_ref, gq_ref, wq_ref, gkv_ref, wkv_ref, cos_ref, sin_ref,
                 qd_ref, kd_ref, vd_ref, qm_ref, km_ref, vm_ref):
    xn = _rms(x_ref[...], g_ref[...]).astype(BF16)
    r = _dot(xn, w_ref[...])
    w = DIFF_WIDTH
    qd_ref[...] = (r[:, 0:w] * DIFF_HALF_DIM ** -0.5).astype(BF16)
    kd_ref[...] = r[:, w:2 * w].astype(BF16)
    vd_ref[...] = r[:, 2 * w:3 * w].astype(BF16)
    o = 3 * w
    cq = r[:, o:o + MLA_Q_RANK]
    o += MLA_Q_RANK
    ckv = r[:, o:o + MLA_KV_RANK]
    o += MLA_KV_RANK
    cos = cos_ref[...]
    sin = sin_ref[...]
    kpe = r[:, o:o + MLA_SLAB] * cos + r[:, o + MLA_SLAB:o + 2 * MLA_SLAB] * sin
    q2 = _dot(_rms(cq, gq_ref[...]).astype(BF16), wq_ref[...])
    kv2 = _dot(_rms(ckv, gkv_ref[...]).astype(BF16), wkv_ref[...])
    sw = MLA_HEADS * MLA_SLAB
    for h in range(MLA_HEADS):
        lo = h * MLA_SLAB
        qh = q2[:, lo:lo + MLA_SLAB] * cos + q2[:, sw + lo:sw + lo + MLA_SLAB] * sin
        qm_ref[:, lo:lo + MLA_SLAB] = (qh * MLA_QK_DIM ** -0.5).astype(BF16)
        km_ref[:, lo:lo + MLA_SLAB] = (kv2[:, lo:lo + MLA_SLAB] + kpe).astype(BF16)
    vm_ref[...] = kv2[:, sw:sw + MLA_WIDTH].astype(BF16)


def _proj_call(x2d, g, w_all, gq, wq2, gkv, wkv2, cos_t, sin_t):
    t, d = x2d.shape
    rows = min(PROJ_ROWS, t)
    row = lambda i: (i, 0)
    fixed = lambda i: (0, 0)
    full = lambda a: pl.BlockSpec(a.shape, fixed)
    outs = [(DIFF_WIDTH, BF16)] * 3 + [(MLA_HEADS * MLA_SLAB, BF16)] * 2 + [(MLA_WIDTH, BF16)]
    return pl.pallas_call(
        _proj_kernel,
        grid=(t // rows,),
        in_specs=[pl.BlockSpec((rows, d), row), full(g), full(w_all), full(gq), full(wq2),
                  full(gkv), full(wkv2), pl.BlockSpec((rows, MLA_SLAB), row),
                  pl.BlockSpec((rows, MLA_SLAB), row)],
        out_specs=[pl.BlockSpec((rows, n), row) for n, _ in outs],
        out_shape=[jax.ShapeDtypeStruct((t, n), dt) for n, dt in outs],
        compiler_params=_params("parallel"),
        name="proj",
    )(x2d, g, w_all, gq, wq2, gkv, wkv2, cos_t, sin_t)


def _softmax_step(s, v_chunk, m, l, acc):
    m_new = jnp.maximum(m, jnp.max(s, axis=-1, keepdims=True))
    alpha = jnp.exp(m - m_new)
    p = jnp.exp(s - m_new)
    l = alpha * l + jnp.sum(p, axis=-1, keepdims=True)
    acc = alpha * acc + _dot(p.astype(BF16), v_chunk)
    return m_new, l, acc


def _diff_attn_kernel(lam_ref, q_ref, k_ref, v_ref, cb_ref, g_ref, o_ref, qs_ref, *, tq, tk):
    qi = pl.program_id(2)
    q = q_ref[0]
    lane = lax.broadcasted_iota(jnp.int32, (tq, LANES), 1)
    for v in range(4):
        keep = (lane >= v * DIFF_HALF_DIM) & (lane < (v + 1) * DIFF_HALF_DIM)
        qs_ref[v * tq:(v + 1) * tq, :] = jnp.where(keep, q, jnp.zeros_like(q))

    def chunk(j, carry, masked):
        start = pl.multiple_of(j * tk, tk)
        kc = k_ref[0, pl.ds(start, tk), :]
        vc = v_ref[0, pl.ds(start, tk), :]
        cb = cb_ref[0, 0, :, pl.ds(start, tk)]
        s = lax.dot_general(qs_ref[...], kc, _NT, preferred_element_type=F32)
        s = jnp.concatenate([s[:2 * tq] + cb[0:1], s[2 * tq:] + cb[1:2]], axis=0)
        if masked:
            col = start + lax.broadcasted_iota(jnp.int32, s.shape, 1)
            row = qi * tq + (lax.broadcasted_iota(jnp.int32, s.shape, 0) & (tq - 1))
            s = jnp.where(col <= row, s, MASK_VALUE)
        return _softmax_step(s, vc, *carry)

    init = (jnp.full((4 * tq, 1), MASK_VALUE, F32), jnp.zeros((4 * tq, 1), F32),
            jnp.zeros((4 * tq, LANES), F32))
    n_full = (qi * tq) // tk
    carry = lax.fori_loop(0, n_full, lambda j, c: chunk(j, c, False), init)
    _, l, acc = chunk(n_full, carry, True)

    on = acc * (1.0 / l)
    lam = lam_ref[0]
    o = jnp.where(lane < DIFF_V_DIM, on[0:tq] - lam * on[tq:2 * tq],
                  on[2 * tq:3 * tq] - lam * on[3 * tq:4 * tq])
    sq = o * o
    ss_lo = jnp.sum(jnp.where(lane < DIFF_V_DIM, sq, 0.0), axis=-1, keepdims=True)
    ss_hi = jnp.sum(sq, axis=-1, keepdims=True) - ss_lo
    ms = jnp.where(lane < DIFF_V_DIM, ss_lo, ss_hi) * (1.0 / DIFF_V_DIM)
    o_ref[0] = (o * lax.rsqrt(ms + RMS_EPS) * g_ref[...]).astype(o_ref.dtype)


def _diff_attn_call(lam, qd, kd, vd, cb, g2, tq, tk):
    b, s, _ = qd.shape
    pairs = DIFF_HEADS // 2
    return pl.pallas_call(
        functools.partial(_diff_attn_kernel, tq=tq, tk=tk),
        grid=(b, pairs, s // tq),
        in_specs=[pl.BlockSpec(memory_space=pltpu.SMEM),
                  pl.BlockSpec((1, tq, LANES), lambda bi, p, qi: (bi, qi, p)),
                  pl.BlockSpec((1, s, LANES), lambda bi, p, qi: (bi, 0, p)),
                  pl.BlockSpec((1, s, LANES), lambda bi, p, qi: (bi, 0, p)),
                  pl.BlockSpec((1, 1, 2, s), lambda bi, p, qi: (bi, p, 0, 0)),
                  pl.BlockSpec((1, LANES), lambda bi, p, qi: (0, 0))],
        out_specs=pl.BlockSpec((1, tq, LANES), lambda bi, p, qi: (bi, qi, p)),
        out_shape=jax.ShapeDtypeStruct((b, s, DIFF_WIDTH), BF16),
        scratch_shapes=[pltpu.VMEM((4 * tq, LANES), BF16)],
        compiler_params=_params("parallel", "parallel", "arbitrary"),
        name="diff_attn",
    )(lam, qd, kd, vd, cb, g2)


def _mla_attn_kernel(q_ref, k_ref, v_ref, o_ref, *, tq, tk):
    qi = pl.program_id(2)
    n_full = (qi * tq) // tk
    outs = []
    for hh in range(2):
        sl = slice(hh * MLA_SLAB, (hh + 1) * MLA_SLAB)
        q = q_ref[0, :, sl]

        def chunk(j, carry, masked, q=q, sl=sl):
            start = pl.multiple_of(j * tk, tk)
            kc = k_ref[0, pl.ds(start, tk), sl]
            vc = v_ref[0, pl.ds(start, tk), :]
            s = lax.dot_general(q, kc, _NT, preferred_element_type=F32)
            if masked:
                col = start + lax.broadcasted_iota(jnp.int32, s.shape, 1)
                row = qi * tq + lax.broadcasted_iota(jnp.int32, s.shape, 0)
                s = jnp.where(col <= row, s, MASK_VALUE)
            return _softmax_step(s, vc, *carry)

        init = (jnp.full((tq, 1), MASK_VALUE, F32), jnp.zeros((tq, 1), F32),
                jnp.zeros((tq, LANES), F32))
        carry = lax.fori_loop(0, n_full, lambda j, c, f=chunk: f(j, c, False), init)
        _, l, acc = chunk(n_full, carry, True)
        outs.append(acc * (1.0 / l))
    lane = lax.broadcasted_iota(jnp.int32, (tq, LANES), 1)
    o_ref[0] = jnp.where(lane < MLA_V_DIM, outs[0], outs[1]).astype(o_ref.dtype)


def _mla_attn_call(qm, km, vm, tq, tk):
    b, s, _ = qm.shape
    pairs = MLA_HEADS // 2
    return pl.pallas_call(
        functools.partial(_mla_attn_kernel, tq=tq, tk=tk),
        grid=(b, pairs, s // tq),
        in_specs=[pl.BlockSpec((1, tq, 2 * MLA_SLAB), lambda bi, p, qi: (bi, qi, p)),
                  pl.BlockSpec((1, s, 2 * MLA_SLAB), lambda bi, p, qi: (bi, 0, p)),
                  pl.BlockSpec((1, s, LANES), lambda bi, p, qi: (bi, 0, p))],
        out_specs=pl.BlockSpec((1, tq, LANES), lambda bi, p, qi: (bi, qi, p)),
        out_shape=jax.ShapeDtypeStruct((b, s, MLA_WIDTH), BF16),
        compiler_params=_params("parallel", "parallel", "arbitrary"),
        name="mla_attn",
    )(qm, km, vm)


def _mem_kv_kernel(mem_ref, g_ref, w_ref, k_ref, v_ref):
    kv = _dot(_rms(mem_ref[0], g_ref[...]).astype(BF16), w_ref[...])
    k_ref[0] = kv[:, :MEM_WIDTH].astype(BF16)
    v_ref[0] = kv[:, MEM_WIDTH:].astype(BF16)


def _mem_kv_call(mem, g, w_kv):
    b, n_mem, d = mem.shape
    blk = pl.BlockSpec((1, n_mem, MEM_WIDTH), lambda bi: (bi, 0, 0))
    return pl.pallas_call(
        _mem_kv_kernel,
        grid=(b,),
        in_specs=[pl.BlockSpec((1, n_mem, d), lambda bi: (bi, 0, 0)),
                  pl.BlockSpec(g.shape, lambda bi: (0, 0)),
                  pl.BlockSpec(w_kv.shape, lambda bi: (0, 0))],
        out_specs=[blk, blk],
        out_shape=[jax.ShapeDtypeStruct((b, n_mem, MEM_WIDTH), BF16)] * 2,
        compiler_params=_params("parallel"),
        name="mem_kv",
    )(mem, g, w_kv)


def _mix_kernel(x_ref, d_ref, m_ref, gm_ref, wo_ref, gc_ref, wq_ref, km_ref, vm_ref, wmo_ref,
                gf_ref, wr_ref, br_ref, x2_ref, h3_ref, r_ref, cnt_ref, run_ref):
    rows = x_ref.shape[0]

    @pl.when(pl.program_id(0) == 0)
    def _():
        run_ref[...] = jnp.zeros_like(run_ref)

    mla_n = _rms(m_ref[...].astype(F32), gm_ref[...]).astype(BF16)
    x1 = (x_ref[...] + _dot(d_ref[...], wo_ref[0:DIFF_WIDTH, :])
          + _dot(mla_n, wo_ref[DIFF_WIDTH:DIFF_WIDTH + MLA_WIDTH, :]))

    qx = _dot(_rms(x1, gc_ref[...]).astype(BF16), wq_ref[...])
    heads = []
    for h in range(MEM_HEADS):
        sl = slice(h * MEM_HEAD_DIM, (h + 1) * MEM_HEAD_DIM)
        sc = lax.dot_general(qx[:, sl].astype(BF16), km_ref[0, :, sl], _NT,
                             preferred_element_type=F32) * MEM_HEAD_DIM ** -0.5
        p = jnp.exp(sc - jnp.max(sc, axis=-1, keepdims=True))
        o = _dot(p.astype(BF16), vm_ref[0, :, sl]) * (1.0 / jnp.sum(p, axis=-1, keepdims=True))
        heads.append(o.astype(BF16))
    x2 = x1 + _dot(jnp.concatenate(heads, axis=1), wmo_ref[...])
    x2_ref[...] = x2

    h3 = _rms(x2, gf_ref[...])
    h3_ref[...] = h3
    logits = jnp.dot(h3, wr_ref[...], preferred_element_type=F32,
                     precision=lax.Precision.HIGHEST) + br_ref[...]
    lane = lax.broadcasted_iota(jnp.int32, (rows, LANES), 1).astype(F32)
    far = float(4 * LANES)

    def first_argmax(vals):
        top = jnp.max(vals, axis=-1, keepdims=True)
        return top, jnp.min(jnp.where(vals == top, lane, far), axis=-1, keepdims=True)

    g_logits = jnp.where(lane < N_GROUPS, logits, MASK_VALUE)
    g_top, g_sel = first_argmax(g_logits)
    g_gate = 1.0 / jnp.sum(jnp.exp(g_logits - g_top), axis=-1, keepdims=True)
    lo = ROUTER_LANE0 + EXPERTS_PER_GROUP * g_sel
    e_logits = jnp.where(lane >= lo, jnp.where(lane < lo + EXPERTS_PER_GROUP, logits, MASK_VALUE),
                         MASK_VALUE)
    top1, lane1 = first_argmax(e_logits)
    top2, lane2 = first_argmax(jnp.where(lane == lane1, MASK_VALUE, e_logits))
    t = jnp.exp(top2 - top1)
    gate1 = g_gate / (1.0 + t)
    gate2 = gate1 * t

    hot = jnp.where(lane == lane1, 1.0, 0.0) + jnp.where(lane == lane2, 1.0, 0.0)
    tri = (lax.broadcasted_iota(jnp.int32, (rows, rows), 1)
           < lax.broadcasted_iota(jnp.int32, (rows, rows), 0))
    before = _dot(jnp.where(tri, 1.0, 0.0).astype(BF16), hot.astype(BF16)) + run_ref[0:1, :]
    rank1 = jnp.sum(jnp.where(lane == lane1, before, 0.0), axis=-1, keepdims=True)
    rank2 = jnp.sum(jnp.where(lane == lane2, before, 0.0), axis=-1, keepdims=True)
    run_ref[...] = run_ref[...] + jnp.sum(hot, axis=0, keepdims=True)
    cnt_ref[...] = run_ref[...]

    fields = (lane1 - ROUTER_LANE0, lane2 - ROUTER_LANE0, gate1, gate2, rank1, rank2)
    out = jnp.zeros((rows, LANES), F32)
    for i, f in enumerate(fields):
        out = jnp.where(lane == float(i), f, out)
    r_ref[...] = out


def _mix_call(x2d, diff, mla, gm, wo, gc, wq, kmem, vmem, wmo, gf, wr, br, seq):
    t, d = x2d.shape
    rows = min(MIX_ROWS, seq)
    per_seq = seq // rows
    row = lambda i: (i, 0)
    fixed = lambda i: (0, 0)
    full = lambda a: pl.BlockSpec(a.shape, fixed)
    memblk = pl.BlockSpec((1,) + kmem.shape[1:], lambda i: (i // per_seq, 0, 0))
    return pl.pallas_call(
        _mix_kernel,
        grid=(t // rows,),
        in_specs=[pl.BlockSpec((rows, d), row), pl.BlockSpec((rows, DIFF_WIDTH), row),
                  pl.BlockSpec((rows, MLA_WIDTH), row), full(gm), full(wo), full(gc), full(wq),
                  memblk, memblk, full(wmo), full(gf), full(wr), full(br)],
        out_specs=[pl.BlockSpec((rows, d), row), pl.BlockSpec((rows, d), row),
                   pl.BlockSpec((rows, LANES), row), pl.BlockSpec((8, LANES), fixed)],
        out_shape=[jax.ShapeDtypeStruct((t, d), F32), jax.ShapeDtypeStruct((t, d), F32),
                   jax.ShapeDtypeStruct((t, LANES), F32), jax.ShapeDtypeStruct((8, LANES), F32)],
        scratch_shapes=[pltpu.VMEM((8, LANES), F32)],
        compiler_params=_params("arbitrary"),
        name="mix",
    )(x2d, diff, mla, gm, wo, gc, wq, kmem, vmem, wmo, gf, wr, br)


def _dispatch_kernel(d0_ref, d1_ref, h_ref, xs_in_ref, xs_ref, sem):
    del xs_in_ref
    rows = h_ref.shape[0]

    def issue(r, c):
        src = h_ref.at[pl.ds(r, 1)]
        pltpu.make_async_copy(src, xs_ref.at[pl.ds(d0_ref[0, 0, r], 1)], sem).start()
        pltpu.make_async_copy(src, xs_ref.at[pl.ds(d1_ref[0, 0, r], 1)], sem).start()
        return c

    lax.fori_loop(0, rows, issue, 0)
    for _ in range(2):
        pltpu.make_async_copy(h_ref, xs_ref.at[pl.ds(0, rows)], sem).wait()


def _dispatch_call(d0, d1, h3, xs0, rows):
    t, d = h3.shape
    idx = pl.BlockSpec((1, 1, rows), lambda i: (i, 0, 0), memory_space=pltpu.SMEM)
    return pl.pallas_call(
        _dispatch_kernel,
        grid=(t // rows,),
        in_specs=[idx, idx, pl.BlockSpec((rows, d), lambda i: (i, 0)),
                  pl.BlockSpec(memory_space=pl.ANY)],
        out_specs=pl.BlockSpec(memory_space=pl.ANY),
        out_shape=jax.ShapeDtypeStruct(xs0.shape, xs0.dtype),
        scratch_shapes=[pltpu.SemaphoreType.DMA(())],
        input_output_aliases={3: 0},
        compiler_params=_params("arbitrary"),
        name="dispatch",
    )(d0, d1, h3, xs0)


def _expert_kernel(be_ref, x_ref, wg_ref, wu_ref, wd_ref, y_ref):
    del be_ref
    x = x_ref[...].astype(BF16)
    gate = _dot(x, wg_ref[0])
    up = _dot(x, wu_ref[0])
    hidden = gate * (1.0 / (1.0 + jnp.exp(-gate))) * up
    y_ref[...] = _dot(hidden.astype(BF16), wd_ref[0])


def _expert_call(block_expert, xs, wg, wu, wd, rows):
    n_slots, d = xs.shape
    ff = wg.shape[-1]
    grid_spec = pltpu.PrefetchScalarGridSpec(
        num_scalar_prefetch=1,
        grid=(n_slots // rows,),
        in_specs=[pl.BlockSpec((rows, d), lambda i, be: (i, 0)),
                  pl.BlockSpec((1, d, ff), lambda i, be: (be[i], 0, 0)),
                  pl.BlockSpec((1, d, ff), lambda i, be: (be[i], 0, 0)),
                  pl.BlockSpec((1, ff, d), lambda i, be: (be[i], 0, 0))],
        out_specs=pl.BlockSpec((rows, d), lambda i, be: (i, 0)),
    )
    return pl.pallas_call(
        _expert_kernel,
        grid_spec=grid_spec,
        out_shape=jax.ShapeDtypeStruct((n_slots, d), F32),
        compiler_params=_params("arbitrary"),
        name="experts",
    )(block_expert, xs, wg, wu, wd)


def _combine_kernel(d0_ref, d1_ref, x_ref, r_ref, g_ref, y_ref, o_ref, buf0, buf1, sem):
    rows = x_ref.shape[0]

    def issue(r, c):
        pltpu.make_async_copy(y_ref.at[pl.ds(d0_ref[0, 0, r], 1)], buf0.at[pl.ds(r, 1)], sem).start()
        pltpu.make_async_copy(y_ref.at[pl.ds(d1_ref[0, 0, r], 1)], buf1.at[pl.ds(r, 1)], sem).start()
        return c

    lax.fori_loop(0, rows, issue, 0)
    for buf in (buf0, buf1):
        pltpu.make_async_copy(y_ref.at[pl.ds(0, rows)], buf, sem).wait()
    gates = r_ref[...]
    x3 = x_ref[...] + gates[:, 2:3] * buf0[...] + gates[:, 3:4] * buf1[...]
    o_ref[...] = _rms(x3, g_ref[...])


def _combine_call(d0, d1, x2, router, g, y, rows):
    t, d = x2.shape
    idx = pl.BlockSpec((1, 1, rows), lambda i: (i, 0, 0), memory_space=pltpu.SMEM)
    row = lambda i: (i, 0)
    return pl.pallas_call(
        _combine_kernel,
        grid=(t // rows,),
        in_specs=[idx, idx, pl.BlockSpec((rows, d), row), pl.BlockSpec((rows, LANES), row),
                  pl.BlockSpec(g.shape, lambda i: (0, 0)), pl.BlockSpec(memory_space=pl.ANY)],
        out_specs=pl.BlockSpec((rows, d), row),
        out_shape=jax.ShapeDtypeStruct((t, d), F32),
        scratch_shapes=[pltpu.VMEM((rows, d), F32), pltpu.VMEM((rows, d), F32),
                        pltpu.SemaphoreType.DMA(())],
        compiler_params=_params("arbitrary"),
        name="combine",
    )(d0, d1, x2, router, g, y)


def _slab(cols, lo):
    return jnp.pad(cols, ((0, 0), (lo, MLA_SLAB - lo - cols.shape[1])))


def _layer_weights(w_in, w_uq, w_ukv):
    half = MLA_ROPE_DIM // 2
    o = 3 * DIFF_WIDTH + MLA_Q_RANK + MLA_KV_RANK
    kpe1, kpe2 = w_in[:, o:o + half], w_in[:, o + half:o + 2 * half]
    w_all = jnp.concatenate(
        [w_in[:, :o],
         _slab(jnp.concatenate([kpe1, kpe2], axis=1), MLA_NOPE_DIM),
         _slab(jnp.concatenate([kpe2, kpe1], axis=1), MLA_NOPE_DIM)], axis=1).astype(BF16)
    q_main, q_swap, k_nope, v_cols = [], [], [], []
    for h in range(MLA_HEADS):
        c = w_uq[:, h * MLA_QK_DIM:(h + 1) * MLA_QK_DIM]
        pe1 = c[:, MLA_NOPE_DIM:MLA_NOPE_DIM + half]
        pe2 = c[:, MLA_NOPE_DIM + half:]
        q_main.append(_slab(c, 0))
        q_swap.append(_slab(jnp.concatenate([pe2, pe1], axis=1), MLA_NOPE_DIM))
        kv = w_ukv[:, h * (MLA_NOPE_DIM + MLA_V_DIM):(h + 1) * (MLA_NOPE_DIM + MLA_V_DIM)]
        k_nope.append(_slab(kv[:, :MLA_NOPE_DIM], 0))
        v_cols.append(kv[:, MLA_NOPE_DIM:])
    wq2 = jnp.concatenate(q_main + q_swap, axis=1).astype(BF16)
    wkv2 = jnp.concatenate(k_nope + v_cols, axis=1).astype(BF16)
    return w_all, wq2, wkv2


def _rope_tables(positions):
    inv_freq = ROPE_THETA ** (-jnp.arange(0, MLA_ROPE_DIM, 2, dtype=F32) / MLA_ROPE_DIM)
    ang = positions.astype(F32).reshape(-1, 1) * inv_freq
    cos, sin = jnp.cos(ang), jnp.sin(ang)
    t = ang.shape[0]
    pad = jnp.zeros((t, MLA_SLAB - MLA_QK_DIM), F32)
    cos_t = jnp.concatenate([jnp.ones((t, MLA_NOPE_DIM), F32), cos, cos, pad], axis=1)
    sin_t = jnp.concatenate([jnp.zeros((t, MLA_NOPE_DIM), F32), -sin, sin, pad], axis=1)
    return cos_t, sin_t


def kernel(x, mem, positions, norm_mix_g, w_in, diff_lambda_q1, diff_lambda_k1, diff_lambda_q2, diff_lambda_k2, diff_out_g, mla_q_norm_g, w_mla_uq, mla_kv_norm_g, w_mla_ukv, mla_out_g, w_out, norm_cross_g, norm_mem_g, w_mem_q, w_mem_kv, w_mem_o, norm_ffn_g, w_group_router, b_group_router, w_expert_router, b_expert_router, w_expert_gate, w_expert_up, w_expert_down, norm_final_g):
    b, s, d = x.shape
    t = b * s
    depth = w_in.shape[0]
    row2 = lambda v: v.reshape(1, -1).astype(F32)
    cos_t, sin_t = _rope_tables(positions)
    pos_f = positions.astype(F32)
    slopes = 2.0 ** (-8.0 * jnp.arange(1, DIFF_HEADS + 1, dtype=F32) / DIFF_HEADS)
    col_bias = (slopes[None, :, None] * (pos_f - pos_f[:, :1])[:, None, :]).reshape(
        b, DIFF_HEADS // 2, 2, s)

    diff_tq, diff_tk = min(DIFF_TQ, s), min(DIFF_TK, s)
    mla_tq, mla_tk = min(MLA_TQ, s), min(MLA_TK, s)
    moe_rows = min(MOE_ROWS, t)
    blk = EXPERT_ROWS
    n_assign = 2 * t
    n_blocks = -(-(n_assign + N_EXPERTS * (blk - 1)) // blk)
    n_slots = n_blocks * blk

    assert depth == 1, "single-layer problem: the combine kernel applies the final RMSNorm"
    l = 0
    x2d = x.reshape(t, d)
    w_all, wq2, wkv2 = _layer_weights(w_in[l], w_mla_uq[l], w_mla_ukv[l])
    qd, kd, vd, qm, km, vm = _proj_call(
        x2d, row2(norm_mix_g[l]), w_all, row2(mla_q_norm_g[l]), wq2,
        row2(mla_kv_norm_g[l]), wkv2, cos_t, sin_t)

    lambda_init = 0.8 - 0.6 * math.exp(-0.3 * l)
    lam = (jnp.exp(jnp.sum(diff_lambda_q1[l] * diff_lambda_k1[l]))
           - jnp.exp(jnp.sum(diff_lambda_q2[l] * diff_lambda_k2[l])) + lambda_init)
    gain2 = row2(jnp.tile(diff_out_g[l] * (1.0 - lambda_init), 2))
    shp = lambda a: a.reshape(b, s, a.shape[-1])
    diff_out = _diff_attn_call(lam.reshape(1).astype(F32), shp(qd), shp(kd), shp(vd),
                               col_bias, gain2, diff_tq, diff_tk)
    mla_out = _mla_attn_call(shp(qm), shp(km), shp(vm), mla_tq, mla_tk)

    kmem, vmem = _mem_kv_call(mem, row2(norm_mem_g[l]), w_mem_kv[l].astype(BF16))
    w_router = jnp.pad(
        jnp.concatenate([w_group_router[l], w_expert_router[l]], axis=1),
        ((0, 0), (0, LANES - N_GROUPS - N_EXPERTS)))
    b_router = jnp.pad(jnp.concatenate([b_group_router[l], b_expert_router[l]]),
                       (0, LANES - N_GROUPS - N_EXPERTS))
    x2, h3, router, counts = _mix_call(
        x2d, diff_out.reshape(t, DIFF_WIDTH), mla_out.reshape(t, MLA_WIDTH),
        row2(mla_out_g[l]), w_out[l].astype(BF16), row2(norm_cross_g[l]),
        w_mem_q[l].astype(BF16), kmem, vmem, w_mem_o[l].astype(BF16),
        row2(norm_ffn_g[l]), w_router, row2(b_router), s)

    cnt = counts[0, ROUTER_LANE0:ROUTER_LANE0 + N_EXPERTS].astype(jnp.int32)
    padded = ((cnt + blk - 1) // blk) * blk
    seg_end = jnp.cumsum(padded)
    seg_start = seg_end - padded
    ridx = router[:, 0:6].astype(jnp.int32)
    dest0 = (seg_start[ridx[:, 0]] + ridx[:, 4]).reshape(t // moe_rows, 1, moe_rows)
    dest1 = (seg_start[ridx[:, 1]] + ridx[:, 5]).reshape(t // moe_rows, 1, moe_rows)
    block_expert = jnp.minimum(
        jnp.searchsorted(seg_end, jnp.arange(n_blocks, dtype=jnp.int32) * blk, side='right'),
        N_EXPERTS - 1).astype(jnp.int32)

    xs = _dispatch_call(dest0, dest1, h3, jnp.zeros((n_slots, d), F32), moe_rows)
    y = _expert_call(block_expert, xs, w_expert_gate[l].astype(BF16),
                     w_expert_up[l].astype(BF16), w_expert_down[l].astype(BF16), blk)
    out = _combine_call(dest0, dest1, x2, router, row2(norm_final_g), y, moe_rows)
    return out.reshape(b, s, d)
```

```python
import functools
import math

import jax
import jax.numpy as jnp
from jax import lax
from jax.experimental import pallas as pl
from jax.experimental.pallas import tpu as pltpu

F32 = jnp.float32
BF16 = jnp.bfloat16

RMS_EPS = 1e-6
ROPE_THETA = 10000.0
MASK_VALUE = -1e30

DIFF_HEADS = 8
DIFF_HALF_DIM = 32
DIFF_V_DIM = 64
DIFF_WIDTH = DIFF_HEADS * DIFF_V_DIM

MLA_HEADS = 8
MLA_Q_RANK = 256
MLA_KV_RANK = 128
MLA_NOPE_DIM = 64
MLA_ROPE_DIM = 32
MLA_QK_DIM = MLA_NOPE_DIM + MLA_ROPE_DIM
MLA_V_DIM = 64
MLA_WIDTH = MLA_HEADS * MLA_V_DIM

MEM_HEADS = 4
MEM_HEAD_DIM = 128
MEM_WIDTH = MEM_HEADS * MEM_HEAD_DIM

N_GROUPS = 4
EXPERTS_PER_GROUP = 8
N_EXPERTS = N_GROUPS * EXPERTS_PER_GROUP
EXPERT_FF = 256

LANES = 128
MLA_SLAB = LANES
ROUTER_LANE0 = N_GROUPS

PROJ_ROWS = 512
MIX_ROWS = 256
MOE_ROWS = 256
EXPERT_ROWS = 256
ATTN_TK, ATTN_TK_BIG = 256, 1024
DIFF_TQ = 128
MLA_TQ = 256
LOG2E = math.log2(math.e)
VMEM_LIMIT = 48 * 1024 * 1024

_NT = (((1,), (1,)), ((), ()))


def _rms(x, gain):
    return x * lax.rsqrt(jnp.mean(x * x, axis=-1, keepdims=True) + RMS_EPS) * gain


def _dot(a, b):
    return jnp.dot(a, b, preferred_element_type=F32)


def _params(*sem):
    return pltpu.CompilerParams(dimension_semantics=sem, vmem_limit_bytes=VMEM_LIMIT)


def _proj_kernel(x_ref, g_ref, w_ref, gq_ref, wq_ref, gkv_ref, wkv_ref, cos_ref, sin_ref,
                 qd_ref, kd_ref, vd_ref, qm_ref, km_ref, vm_ref):
    xn = _rms(x_ref[...], g_ref[...]).astype(BF16)
    r = _dot(xn, w_ref[...])
    w = DIFF_WIDTH
    qd_ref[...] = (r[:, 0:w] * (DIFF_HALF_DIM ** -0.5 * LOG2E)).astype(BF16)
    kd_ref[...] = r[:, w:2 * w].astype(BF16)
    vd_ref[...] = r[:, 2 * w:3 * w].astype(BF16)
    o = 3 * w
    cq = r[:, o:o + MLA_Q_RANK]
    o += MLA_Q_RANK
    ckv = r[:, o:o + MLA_KV_RANK]
    o += MLA_KV_RANK
    cos = cos_ref[...]
    sin = sin_ref[...]
    kpe = r[:, o:o + MLA_SLAB] * cos + r[:, o + MLA_SLAB:o + 2 * MLA_SLAB] * sin
    q2 = _dot(_rms(cq, gq_ref[...]).astype(BF16), wq_ref[...])
    kv2 = _dot(_rms(ckv, gkv_ref[...]).astype(BF16), wkv_ref[...])
    sw = MLA_HEADS * MLA_SLAB
    for h in range(MLA_HEADS):
        lo = h * MLA_SLAB
        qh = q2[:, lo:lo + MLA_SLAB] * cos + q2[:, sw + lo:sw + lo + MLA_SLAB] * sin
        qm_ref[:, lo:lo + MLA_SLAB] = (qh * (MLA_QK_DIM ** -0.5 * LOG2E)).astype(BF16)
        km_ref[:, lo:lo + MLA_SLAB] = (kv2[:, lo:lo + MLA_SLAB] + kpe).astype(BF16)
    vm_ref[...] = kv2[:, sw:sw + MLA_WIDTH].astype(BF16)


def _proj_call(x2d, g, w_all, gq, wq2, gkv, wkv2, cos_t, sin_t):
    t, d = x2d.shape
    rows = min(PROJ_ROWS, t)
    row = lambda i: (i, 0)
    fixed = lambda i: (0, 0)
    full = lambda a: pl.BlockSpec(a.shape, fixed)
    outs = [(DIFF_WIDTH, BF16)] * 3 + [(MLA_HEADS * MLA_SLAB, BF16)] * 2 + [(MLA_WIDTH, BF16)]
    return pl.pallas_call(
        _proj_kernel,
        grid=(t // rows,),
        in_specs=[pl.BlockSpec((rows, d), row), full(g), full(w_all), full(gq), full(wq2),
                  full(gkv), full(wkv2), pl.BlockSpec((rows, MLA_SLAB), row),
                  pl.BlockSpec((rows, MLA_SLAB), row)],
        out_specs=[pl.BlockSpec((rows, n), row) for n, _ in outs],
        out_shape=[jax.ShapeDtypeStruct((t, n), dt) for n, dt in outs],
        compiler_params=_params("parallel"),
        name="proj",
    )(x2d, g, w_all, gq, wq2, gkv, wkv2, cos_t, sin_t)


def _softmax_step(s, v_chunk, m, l, acc):
    m_new = jnp.maximum(m, jnp.max(s, axis=-1, keepdims=True))
    alpha = jnp.exp2(m - m_new)
    p = jnp.exp2(s - m_new)
    l = alpha * l + jnp.sum(p, axis=-1, keepdims=True)
    acc = alpha * acc + _dot(p.astype(BF16), v_chunk)
    return m_new, l, acc


def _causal_sweep(chunk, carry, kv_end, big, small):
    n_big = kv_end // big
    carry = lax.fori_loop(
        0, n_big, lambda j, c: chunk(pl.multiple_of(j * big, big), big, c, False), carry)
    base = n_big * big
    n_small = (kv_end - base) // small
    carry = lax.fori_loop(
        0, n_small, lambda j, c: chunk(pl.multiple_of(base + j * small, small), small, c, False),
        carry)
    return carry, pl.multiple_of(base + n_small * small, small)


def _diff_attn_kernel(lam_ref, q_ref, k_ref, v_ref, cb_ref, g_ref, o_ref, qs_ref, *, tq, tk,
                      tk_big):
    qi = pl.program_id(2)
    q = q_ref[0]
    lane = lax.broadcasted_iota(jnp.int32, (tq, LANES), 1)
    for v in range(4):
        keep = (lane >= v * DIFF_HALF_DIM) & (lane < (v + 1) * DIFF_HALF_DIM)
        qs_ref[v * tq:(v + 1) * tq, :] = jnp.where(keep, q, jnp.zeros_like(q))

    def chunk(start, width, carry, masked):
        kc = k_ref[0, pl.ds(start, width), :]
        vc = v_ref[0, pl.ds(start, width), :]
        cb = cb_ref[0, 0, :, pl.ds(start, width)]
        s = lax.dot_general(qs_ref[...], kc, _NT, preferred_element_type=F32)
        s = jnp.concatenate([s[:2 * tq] + cb[0:1], s[2 * tq:] + cb[1:2]], axis=0)
        if masked:
            col = start + lax.broadcasted_iota(jnp.int32, s.shape, 1)
            row = qi * tq + (lax.broadcasted_iota(jnp.int32, s.shape, 0) & (tq - 1))
            s = jnp.where(col <= row, s, MASK_VALUE)
        return _softmax_step(s, vc, *carry)

    init = (jnp.full((4 * tq, 1), MASK_VALUE, F32), jnp.zeros((4 * tq, 1), F32),
            jnp.zeros((4 * tq, LANES), F32))
    carry, tail = _causal_sweep(chunk, init, qi * tq, tk_big, tk)
    _, l, acc = chunk(tail, tk, carry, True)

    on = acc * (1.0 / l)
    lam = lam_ref[0]
    o = jnp.where(lane < DIFF_V_DIM, on[0:tq] - lam * on[tq:2 * tq],
                  on[2 * tq:3 * tq] - lam * on[3 * tq:4 * tq])
    sq = o * o
    ss_lo = jnp.sum(jnp.where(lane < DIFF_V_DIM, sq, 0.0), axis=-1, keepdims=True)
    ss_hi = jnp.sum(sq, axis=-1, keepdims=True) - ss_lo
    ms = jnp.where(lane < DIFF_V_DIM, ss_lo, ss_hi) * (1.0 / DIFF_V_DIM)
    o_ref[0] = (o * lax.rsqrt(ms + RMS_EPS) * g_ref[...]).astype(o_ref.dtype)


def _diff_attn_call(lam, qd, kd, vd, cb, g2, tq, tk, tk_big):
    b, s, _ = qd.shape
    pairs = DIFF_HEADS // 2
    return pl.pallas_call(
        functools.partial(_diff_attn_kernel, tq=tq, tk=tk, tk_big=tk_big),
        grid=(b, pairs, s // tq),
        in_specs=[pl.BlockSpec(memory_space=pltpu.SMEM),
                  pl.BlockSpec((1, tq, LANES), lambda bi, p, qi: (bi, qi, p)),
                  pl.BlockSpec((1, s, LANES), lambda bi, p, qi: (bi, 0, p)),
                  pl.BlockSpec((1, s, LANES), lambda bi, p, qi: (bi, 0, p)),
                  pl.BlockSpec((1, 1, 2, s), lambda bi, p, qi: (bi, p, 0, 0)),
                  pl.BlockSpec((1, LANES), lambda bi, p, qi: (0, 0))],
        out_specs=pl.BlockSpec((1, tq, LANES), lambda bi, p, qi: (bi, qi, p)),
        out_shape=jax.ShapeDtypeStruct((b, s, DIFF_WIDTH), BF16),
        scratch_shapes=[pltpu.VMEM((4 * tq, LANES), BF16)],
        compiler_params=_params("parallel", "parallel", "arbitrary"),
        name="diff_attn",
    )(lam, qd, kd, vd, cb, g2)


def _mla_attn_kernel(q_ref, k_ref, v_ref, o_ref, *, tq, tk, tk_big):
    qi = pl.program_id(2)
    slabs = [slice(hh * MLA_SLAB, (hh + 1) * MLA_SLAB) for hh in range(2)]

    def chunk(start, width, carry, masked):
        vc = v_ref[0, pl.ds(start, width), :]
        out = []
        for hh, sl in enumerate(slabs):
            s = lax.dot_general(q_ref[0, :, sl], k_ref[0, pl.ds(start, width), sl], _NT,
                                preferred_element_type=F32)
            if masked:
                col = start + lax.broadcasted_iota(jnp.int32, s.shape, 1)
                row = qi * tq + lax.broadcasted_iota(jnp.int32, s.shape, 0)
                s = jnp.where(col <= row, s, MASK_VALUE)
            out.extend(_softmax_step(s, vc, *carry[3 * hh:3 * hh + 3]))
        return tuple(out)

    init = (jnp.full((tq, 1), MASK_VALUE, F32), jnp.zeros((tq, 1), F32),
            jnp.zeros((tq, LANES), F32)) * 2
    carry, tail = _causal_sweep(chunk, init, qi * tq, tk_big, tk)
    _, l0, acc0, _, l1, acc1 = chunk(tail, tk, carry, True)
    lane = lax.broadcasted_iota(jnp.int32, (tq, LANES), 1)
    o_ref[0] = jnp.where(lane < MLA_V_DIM, acc0 * (1.0 / l0), acc1 * (1.0 / l1)).astype(o_ref.dtype)


def _mla_attn_call(qm, km, vm, tq, tk, tk_big):
    b, s, _ = qm.shape
    pairs = MLA_HEADS // 2
    return pl.pallas_call(
        functools.partial(_mla_attn_kernel, tq=tq, tk=tk, tk_big=tk_big),
        grid=(b, pairs, s // tq),
        in_specs=[pl.BlockSpec((1, tq, 2 * MLA_SLAB), lambda bi, p, qi: (bi, qi, p)),
                  pl.BlockSpec((1, s, 2 * MLA_SLAB), lambda bi, p, qi: (bi, 0, p)),
                  pl.BlockSpec((1, s, LANES), lambda bi, p, qi: (bi, 0, p))],
        out_specs=pl.BlockSpec((1, tq, LANES), lambda bi, p, qi: (bi, qi, p)),
        out_shape=jax.ShapeDtypeStruct((b, s, MLA_WIDTH), BF16),
        compiler_params=_params("parallel", "parallel", "arbitrary"),
        name="mla_attn",
    )(qm, km, vm)


def _mem_kv_kernel(mem_ref, g_ref, w_ref, k_ref, v_ref):
    kv = _dot(_rms(mem_ref[0], g_ref[...]).astype(BF16), w_ref[...])
    k_ref[0] = kv[:, :MEM_WIDTH].astype(BF16)
    v_ref[0] = kv[:, MEM_WIDTH:].astype(BF16)


def _mem_kv_call(mem, g, w_kv):
    b, n_mem, d = mem.shape
    blk = pl.BlockSpec((1, n_mem, MEM_WIDTH), lambda bi: (bi, 0, 0))
    return pl.pallas_call(
        _mem_kv_kernel,
        grid=(b,),
        in_specs=[pl.BlockSpec((1, n_mem, d), lambda bi: (bi, 0, 0)),
                  pl.BlockSpec(g.shape, lambda bi: (0, 0)),
                  pl.BlockSpec(w_kv.shape, lambda bi: (0, 0))],
        out_specs=[blk, blk],
        out_shape=[jax.ShapeDtypeStruct((b, n_mem, MEM_WIDTH), BF16)] * 2,
        compiler_params=_params("parallel"),
        name="mem_kv",
    )(mem, g, w_kv)


def _mix_kernel(x_ref, d_ref, m_ref, gm_ref, wo_ref, gc_ref, wq_ref, km_ref, vm_ref, wmo_ref,
                gf_ref, wr_ref, br_ref, x2_ref, h3_ref, r_ref, cnt_ref, run_ref):
    rows = x_ref.shape[0]

    @pl.when(pl.program_id(0) == 0)
    def _():
        run_ref[...] = jnp.zeros_like(run_ref)

    mla_n = _rms(m_ref[...].astype(F32), gm_ref[...]).astype(BF16)
    x1 = (x_ref[...] + _dot(d_ref[...], wo_ref[0:DIFF_WIDTH, :])
          + _dot(mla_n, wo_ref[DIFF_WIDTH:DIFF_WIDTH + MLA_WIDTH, :]))

    qx = _dot(_rms(x1, gc_ref[...]).astype(BF16), wq_ref[...])
    heads = []
    for h in range(MEM_HEADS):
        sl = slice(h * MEM_HEAD_DIM, (h + 1) * MEM_HEAD_DIM)
        sc = lax.dot_general(qx[:, sl].astype(BF16), km_ref[0, :, sl], _NT,
                             preferred_element_type=F32) * MEM_HEAD_DIM ** -0.5
        p = jnp.exp(sc - jnp.max(sc, axis=-1, keepdims=True))
        o = _dot(p.astype(BF16), vm_ref[0, :, sl]) * (1.0 / jnp.sum(p, axis=-1, keepdims=True))
        heads.append(o.astype(BF16))
    x2 = x1 + _dot(jnp.concatenate(heads, axis=1), wmo_ref[...])
    x2_ref[...] = x2

    h3 = _rms(x2, gf_ref[...])
    h3_ref[...] = h3
    logits = jnp.dot(h3, wr_ref[...], preferred_element_type=F32,
                     precision=lax.Precision.HIGHEST) + br_ref[...]
    lane = lax.broadcasted_iota(jnp.int32, (rows, LANES), 1).astype(F32)
    far = float(4 * LANES)

    def first_argmax(vals):
        top = jnp.max(vals, axis=-1, keepdims=True)
        return top, jnp.min(jnp.where(vals == top, lane, far), axis=-1, keepdims=True)

    g_logits = jnp.where(lane < N_GROUPS, logits, MASK_VALUE)
    g_top, g_sel = first_argmax(g_logits)
    g_gate = 1.0 / jnp.sum(jnp.exp(g_logits - g_top), axis=-1, keepdims=True)
    lo = ROUTER_LANE0 + EXPERTS_PER_GROUP * g_sel
    e_logits = jnp.where(lane >= lo, jnp.where(lane < lo + EXPERTS_PER_GROUP, logits, MASK_VALUE),
                         MASK_VALUE)
    top1, lane1 = first_argmax(e_logits)
    top2, lane2 = first_argmax(jnp.where(lane == lane1, MASK_VALUE, e_logits))
    t = jnp.exp(top2 - top1)
    gate1 = g_gate / (1.0 + t)
    gate2 = gate1 * t

    hot = jnp.where(lane == lane1, 1.0, 0.0) + jnp.where(lane == lane2, 1.0, 0.0)
    tri = (lax.broadcasted_iota(jnp.int32, (rows, rows), 1)
           < lax.broadcasted_iota(jnp.int32, (rows, rows), 0))
    before = _dot(jnp.where(tri, 1.0, 0.0).astype(BF16), hot.astype(BF16)) + run_ref[0:1, :]
    rank1 = jnp.sum(jnp.where(lane == lane1, before, 0.0), axis=-1, keepdims=True)
    rank2 = jnp.sum(jnp.where(lane == lane2, before, 0.0), axis=-1, keepdims=True)
    run_ref[...] = run_ref[...] + jnp.sum(hot, axis=0, keepdims=True)
    cnt_ref[...] = run_ref[...]

    fields = (lane1 - ROUTER_LANE0, lane2 - ROUTER_LANE0, gate1, gate2, rank1, rank2)
    out = jnp.zeros((rows, LANES), F32)
    for i, f in enumerate(fields):
        out = jnp.where(lane == float(i), f, out)
    r_ref[...] = out


def _mix_call(x2d, diff, mla, gm, wo, gc, wq, kmem, vmem, wmo, gf, wr, br, seq):
    t, d = x2d.shape
    rows = min(MIX_ROWS, seq)
    per_seq = seq // rows
    row = lambda i: (i, 0)
    fixed = lambda i: (0, 0)
    full = lambda a: pl.BlockSpec(a.shape, fixed)
    memblk = pl.BlockSpec((1,) + kmem.shape[1:], lambda i: (i // per_seq, 0, 0))
    return pl.pallas_call(
        _mix_kernel,
        grid=(t // rows,),
        in_specs=[pl.BlockSpec((rows, d), row), pl.BlockSpec((rows, DIFF_WIDTH), row),
                  pl.BlockSpec((rows, MLA_WIDTH), row), full(gm), full(wo), full(gc), full(wq),
                  memblk, memblk, full(wmo), full(gf), full(wr), full(br)],
        out_specs=[pl.BlockSpec((rows, d), row), pl.BlockSpec((rows, d), row),
                   pl.BlockSpec((rows, LANES), row), pl.BlockSpec((8, LANES), fixed)],
        out_shape=[jax.ShapeDtypeStruct((t, d), F32), jax.ShapeDtypeStruct((t, d), F32),
                   jax.ShapeDtypeStruct((t, LANES), F32), jax.ShapeDtypeStruct((8, LANES), F32)],
        scratch_shapes=[pltpu.VMEM((8, LANES), F32)],
        compiler_params=_params("arbitrary"),
        name="mix",
    )(x2d, diff, mla, gm, wo, gc, wq, kmem, vmem, wmo, gf, wr, br)


def _dispatch_kernel(d0_ref, d1_ref, h_ref, xs_in_ref, xs_ref, sem):
    del xs_in_ref
    rows = h_ref.shape[0]

    def issue(r, c):
        src = h_ref.at[pl.ds(r, 1)]
        pltpu.make_async_copy(src, xs_ref.at[pl.ds(d0_ref[0, 0, r], 1)], sem).start()
        pltpu.make_async_copy(src, xs_ref.at[pl.ds(d1_ref[0, 0, r], 1)], sem).start()
        return c

    lax.fori_loop(0, rows, issue, 0)
    for _ in range(2):
        pltpu.make_async_copy(h_ref, xs_ref.at[pl.ds(0, rows)], sem).wait()


def _dispatch_call(d0, d1, h3, xs0, rows):
    t, d = h3.shape
    idx = pl.BlockSpec((1, 1, rows), lambda i: (i, 0, 0), memory_space=pltpu.SMEM)
    return pl.pallas_call(
        _dispatch_kernel,
        grid=(t // rows,),
        in_specs=[idx, idx, pl.BlockSpec((rows, d), lambda i: (i, 0)),
                  pl.BlockSpec(memory_space=pl.ANY)],
        out_specs=pl.BlockSpec(memory_space=pl.ANY),
        out_shape=jax.ShapeDtypeStruct(xs0.shape, xs0.dtype),
        scratch_shapes=[pltpu.SemaphoreType.DMA(())],
        input_output_aliases={3: 0},
        compiler_params=_params("arbitrary"),
        name="dispatch",
    )(d0, d1, h3, xs0)


def _expert_kernel(be_ref, x_ref, wg_ref, wu_ref, wd_ref, y_ref):
    del be_ref
    x = x_ref[...].astype(BF16)
    gate = _dot(x, wg_ref[0])
    up = _dot(x, wu_ref[0])
    hidden = gate * (1.0 / (1.0 + jnp.exp(-gate))) * up
    y_ref[...] = _dot(hidden.astype(BF16), wd_ref[0])


def _expert_call(block_expert, xs, wg, wu, wd, rows):
    n_slots, d = xs.shape
    ff = wg.shape[-1]
    grid_spec = pltpu.PrefetchScalarGridSpec(
        num_scalar_prefetch=1,
        grid=(n_slots // rows,),
        in_specs=[pl.BlockSpec((rows, d), lambda i, be: (i, 0)),
                  pl.BlockSpec((1, d, ff), lambda i, be: (be[i], 0, 0)),
                  pl.BlockSpec((1, d, ff), lambda i, be: (be[i], 0, 0)),
                  pl.BlockSpec((1, ff, d), lambda i, be: (be[i], 0, 0))],
        out_specs=pl.BlockSpec((rows, d), lambda i, be: (i, 0)),
    )
    return pl.pallas_call(
        _expert_kernel,
        grid_spec=grid_spec,
        out_shape=jax.ShapeDtypeStruct((n_slots, d), F32),
        compiler_params=_params("arbitrary"),
        name="experts",
    )(block_expert, xs, wg, wu, wd)


def _combine_kernel(d0_ref, d1_ref, x_ref, r_ref, g_ref, y_ref, o_ref, buf0, buf1, sem):
    rows = x_ref.shape[0]

    def issue(r, c):
        pltpu.make_async_copy(y_ref.at[pl.ds(d0_ref[0, 0, r], 1)], buf0.at[pl.ds(r, 1)], sem).start()
        pltpu.make_async_copy(y_ref.at[pl.ds(d1_ref[0, 0, r], 1)], buf1.at[pl.ds(r, 1)], sem).start()
        return c

    lax.fori_loop(0, rows, issue, 0)
    for buf in (buf0, buf1):
        pltpu.make_async_copy(y_ref.at[pl.ds(0, rows)], buf, sem).wait()
    gates = r_ref[...]
    x3 = x_ref[...] + gates[:, 2:3] * buf0[...] + gates[:, 3:4] * buf1[...]
    o_ref[...] = _rms(x3, g_ref[...])


def _combine_call(d0, d1, x2, router, g, y, rows):
    t, d = x2.shape
    idx = pl.BlockSpec((1, 1, rows), lambda i: (i, 0, 0), memory_space=pltpu.SMEM)
    row = lambda i: (i, 0)
    return pl.pallas_call(
        _combine_kernel,
        grid=(t // rows,),
        in_specs=[idx, idx, pl.BlockSpec((rows, d), row), pl.BlockSpec((rows, LANES), row),
                  pl.BlockSpec(g.shape, lambda i: (0, 0)), pl.BlockSpec(memory_space=pl.ANY)],
        out_specs=pl.BlockSpec((rows, d), row),
        out_shape=jax.ShapeDtypeStruct((t, d), F32),
        scratch_shapes=[pltpu.VMEM((rows, d), F32), pltpu.VMEM((rows, d), F32),
                        pltpu.SemaphoreType.DMA(())],
        compiler_params=_params("arbitrary"),
        name="combine",
    )(d0, d1, x2, router, g, y)


def _slab(cols, lo):
    return jnp.pad(cols, ((0, 0), (lo, MLA_SLAB - lo - cols.shape[1])))


def _layer_weights(w_in, w_uq, w_ukv):
    half = MLA_ROPE_DIM // 2
    o = 3 * DIFF_WIDTH + MLA_Q_RANK + MLA_KV_RANK
    kpe1, kpe2 = w_in[:, o:o + half], w_in[:, o + half:o + 2 * half]
    w_all = jnp.concatenate(
        [w_in[:, :o],
         _slab(jnp.concatenate([kpe1, kpe2], axis=1), MLA_NOPE_DIM),
         _slab(jnp.concatenate([kpe2, kpe1], axis=1), MLA_NOPE_DIM)], axis=1).astype(BF16)
    q_main, q_swap, k_nope, v_cols = [], [], [], []
    for h in range(MLA_HEADS):
        c = w_uq[:, h * MLA_QK_DIM:(h + 1) * MLA_QK_DIM]
        pe1 = c[:, MLA_NOPE_DIM:MLA_NOPE_DIM + half]
        pe2 = c[:, MLA_NOPE_DIM + half:]
        q_main.append(_slab(c, 0))
        q_swap.append(_slab(jnp.concatenate([pe2, pe1], axis=1), MLA_NOPE_DIM))
        kv = w_ukv[:, h * (MLA_NOPE_DIM + MLA_V_DIM):(h + 1) * (MLA_NOPE_DIM + MLA_V_DIM)]
        k_nope.append(_slab(kv[:, :MLA_NOPE_DIM], 0))
        v_cols.append(kv[:, MLA_NOPE_DIM:])
    wq2 = jnp.concatenate(q_main + q_swap, axis=1).astype(BF16)
    wkv2 = jnp.concatenate(k_nope + v_cols, axis=1).astype(BF16)
    return w_all, wq2, wkv2


def _rope_tables(positions):
    inv_freq = ROPE_THETA ** (-jnp.arange(0, MLA_ROPE_DIM, 2, dtype=F32) / MLA_ROPE_DIM)
    ang = positions.astype(F32).reshape(-1, 1) * inv_freq
    cos, sin = jnp.cos(ang), jnp.sin(ang)
    t = ang.shape[0]
    pad = jnp.zeros((t, MLA_SLAB - MLA_QK_DIM), F32)
    cos_t = jnp.concatenate([jnp.ones((t, MLA_NOPE_DIM), F32), cos, cos, pad], axis=1)
    sin_t = jnp.concatenate([jnp.zeros((t, MLA_NOPE_DIM), F32), -sin, sin, pad], axis=1)
    return cos_t, sin_t


def kernel(x, mem, positions, norm_mix_g, w_in, diff_lambda_q1, diff_lambda_k1, diff_lambda_q2, diff_lambda_k2, diff_out_g, mla_q_norm_g, w_mla_uq, mla_kv_norm_g, w_mla_ukv, mla_out_g, w_out, norm_cross_g, norm_mem_g, w_mem_q, w_mem_kv, w_mem_o, norm_ffn_g, w_group_router, b_group_router, w_expert_router, b_expert_router, w_expert_gate, w_expert_up, w_expert_down, norm_final_g):
    b, s, d = x.shape
    t = b * s
    depth = w_in.shape[0]
    row2 = lambda v: v.reshape(1, -1).astype(F32)
    cos_t, sin_t = _rope_tables(positions)
    pos_f = positions.astype(F32)
    slopes = 2.0 ** (-8.0 * jnp.arange(1, DIFF_HEADS + 1, dtype=F32) / DIFF_HEADS)
    col_bias = ((LOG2E * slopes)[None, :, None] * (pos_f - pos_f[:, :1])[:, None, :]).reshape(
        b, DIFF_HEADS // 2, 2, s)

    diff_tq, mla_tq = min(DIFF_TQ, s), min(MLA_TQ, s)
    attn_tk, attn_tk_big = min(ATTN_TK, s), min(ATTN_TK_BIG, s)
    moe_rows = min(MOE_ROWS, t)
    blk = EXPERT_ROWS
    n_assign = 2 * t
    n_blocks = -(-(n_assign + N_EXPERTS * (blk - 1)) // blk)
    n_slots = n_blocks * blk

    assert depth == 1, "single-layer problem: the combine kernel applies the final RMSNorm"
    l = 0
    x2d = x.reshape(t, d)
    w_all, wq2, wkv2 = _layer_weights(w_in[l], w_mla_uq[l], w_mla_ukv[l])
    qd, kd, vd, qm, km, vm = _proj_call(
        x2d, row2(norm_mix_g[l]), w_all, row2(mla_q_norm_g[l]), wq2,
        row2(mla_kv_norm_g[l]), wkv2, cos_t, sin_t)

    lambda_init = 0.8 - 0.6 * math.exp(-0.3 * l)
    lam = (jnp.exp(jnp.sum(diff_lambda_q1[l] * diff_lambda_k1[l]))
           - jnp.exp(jnp.sum(diff_lambda_q2[l] * diff_lambda_k2[l])) + lambda_init)
    gain2 = row2(jnp.tile(diff_out_g[l] * (1.0 - lambda_init), 2))
    shp = lambda a: a.reshape(b, s, a.shape[-1])
    diff_out = _diff_attn_call(lam.reshape(1).astype(F32), shp(qd), shp(kd), shp(vd),
                               col_bias, gain2, diff_tq, attn_tk, attn_tk_big)
    mla_out = _mla_attn_call(shp(qm), shp(km), shp(vm), mla_tq, attn_tk, attn_tk_big)

    kmem, vmem = _mem_kv_call(mem, row2(norm_mem_g[l]), w_mem_kv[l].astype(BF16))
    w_router = jnp.pad(
        jnp.concatenate([w_group_router[l], w_expert_router[l]], axis=1),
        ((0, 0), (0, LANES - N_GROUPS - N_EXPERTS)))
    b_router = jnp.pad(jnp.concatenate([b_group_router[l], b_expert_router[l]]),
                       (0, LANES - N_GROUPS - N_EXPERTS))
    x2, h3, router, counts = _mix_call(
        x2d, diff_out.reshape(t, DIFF_WIDTH), mla_out.reshape(t, MLA_WIDTH),
        row2(mla_out_g[l]), w_out[l].astype(BF16), row2(norm_cross_g[l]),
        w_mem_q[l].astype(BF16), kmem, vmem, w_mem_o[l].astype(BF16),
        row2(norm_ffn_g[l]), w_router, row2(b_router), s)

    cnt = counts[0, ROUTER_LANE0:ROUTER_LANE0 + N_EXPERTS].astype(jnp.int32)
    padded = ((cnt + blk - 1) // blk) * blk
    seg_end = jnp.cumsum(padded)
    seg_start = seg_end - padded
    ridx = router[:, 0:6].astype(jnp.int32)
    dest0 = (seg_start[ridx[:, 0]] + ridx[:, 4]).reshape(t // moe_rows, 1, moe_rows)
    dest1 = (seg_start[ridx[:, 1]] + ridx[:, 5]).reshape(t // moe_rows, 1, moe_rows)
    block_start = jnp.arange(n_blocks, dtype=jnp.int32) * blk
    block_expert = jnp.minimum(
        jnp.sum((seg_end[None, :] <= block_start[:, None]).astype(jnp.int32), axis=1),
        N_EXPERTS - 1)

    xs = _dispatch_call(dest0, dest1, h3, jnp.zeros((n_slots, d), F32), moe_rows)
    y = _expert_call(block_expert, xs, w_expert_gate[l].astype(BF16),
                     w_expert_up[l].astype(BF16), w_expert_down[l].astype(BF16), blk)
    out = _combine_call(dest0, dest1, x2, router, row2(norm_final_g), y, moe_rows)
    return out.reshape(b, s, d)
```

```python
import functools
import math

import jax
import jax.numpy as jnp
from jax import lax
from jax.experimental import pallas as pl
from jax.experimental.pallas import tpu as pltpu

F32 = jnp.float32
BF16 = jnp.bfloat16

RMS_EPS = 1e-6
ROPE_THETA = 10000.0
MASK_VALUE = -1e30

DIFF_HEADS = 8
DIFF_HALF_DIM = 32
DIFF_V_DIM = 64
DIFF_WIDTH = DIFF_HEADS * DIFF_V_DIM

MLA_HEADS = 8
MLA_Q_RANK = 256
MLA_KV_RANK = 128
MLA_NOPE_DIM = 64
MLA_ROPE_DIM = 32
MLA_QK_DIM = MLA_NOPE_DIM + MLA_ROPE_DIM
MLA_V_DIM = 64
MLA_WIDTH = MLA_HEADS * MLA_V_DIM

MEM_HEADS = 4
MEM_HEAD_DIM = 128
MEM_WIDTH = MEM_HEADS * MEM_HEAD_DIM

N_GROUPS = 4
EXPERTS_PER_GROUP = 8
N_EXPERTS = N_GROUPS * EXPERTS_PER_GROUP
EXPERT_FF = 256

LANES = 128
MLA_SLAB = LANES
ROUTER_LANE0 = N_GROUPS

PROJ_ROWS = 512
MIX_ROWS = 256
MOE_ROWS = 256
EXPERT_ROWS = 256
ATTN_TK = 512
DIFF_TQ = 128
MLA_TQ = 256
LOG2E = math.log2(math.e)
VMEM_LIMIT = 48 * 1024 * 1024

_NT = (((1,), (1,)), ((), ()))


def _rms(x, gain):
    return x * lax.rsqrt(jnp.mean(x * x, axis=-1, keepdims=True) + RMS_EPS) * gain


def _dot(a, b):
    return jnp.dot(a, b, preferred_element_type=F32)


def _params(*sem):
    return pltpu.CompilerParams(dimension_semantics=sem, vmem_limit_bytes=VMEM_LIMIT)


def _proj_kernel(x_ref, g_ref, w_ref, gq_ref, wq_ref, gkv_ref, wkv_ref, cos_ref, sin_ref,
                 pos_ref, slope_ref, qd_ref, kd_ref, vd_ref, qm_ref, km_ref, vm_ref):
    xn = _rms(x_ref[...], g_ref[...]).astype(BF16)
    r = _dot(xn, w_ref[...])
    w = DIFF_WIDTH
    qd_ref[...] = (r[:, 0:w] * (DIFF_HALF_DIM ** -0.5 * LOG2E)).astype(BF16)
    lane = lax.broadcasted_iota(jnp.int32, (x_ref.shape[0], LANES), 1)
    for p in range(DIFF_HEADS // 2):
        bias = pos_ref[...] * slope_ref[p:p + 1, :]
        hi = bias.astype(BF16)
        mid = (bias - hi.astype(F32)).astype(BF16)
        low = (bias - hi.astype(F32) - mid.astype(F32)).astype(BF16)
        term = jnp.where((lane == 0) | (lane == 3), hi, jnp.where((lane == 1) | (lane == 4), mid, low))
        kd_ref[:, 2 * p * LANES:(2 * p + 1) * LANES] = r[:, w + p * LANES:w + (p + 1) * LANES].astype(BF16)
        kd_ref[:, (2 * p + 1) * LANES:(2 * p + 2) * LANES] = term
    vd_ref[0] = r[:, 2 * w:3 * w].T.astype(BF16)
    o = 3 * w
    cq = r[:, o:o + MLA_Q_RANK]
    o += MLA_Q_RANK
    ckv = r[:, o:o + MLA_KV_RANK]
    o += MLA_KV_RANK
    cos = cos_ref[...]
    sin = sin_ref[...]
    kpe = r[:, o:o + MLA_SLAB] * cos + r[:, o + MLA_SLAB:o + 2 * MLA_SLAB] * sin
    q2 = _dot(_rms(cq, gq_ref[...]).astype(BF16), wq_ref[...])
    kv2 = _dot(_rms(ckv, gkv_ref[...]).astype(BF16), wkv_ref[...])
    sw = MLA_HEADS * MLA_SLAB
    for h in range(MLA_HEADS):
        lo = h * MLA_SLAB
        qh = q2[:, lo:lo + MLA_SLAB] * cos + q2[:, sw + lo:sw + lo + MLA_SLAB] * sin
        qm_ref[:, lo:lo + MLA_SLAB] = (qh * (MLA_QK_DIM ** -0.5 * LOG2E)).astype(BF16)
        km_ref[:, lo:lo + MLA_SLAB] = (kv2[:, lo:lo + MLA_SLAB] + kpe).astype(BF16)
    vm_ref[0] = kv2[:, sw:sw + MLA_WIDTH].T.astype(BF16)


def _proj_call(x2d, g, w_all, gq, wq2, gkv, wkv2, cos_t, sin_t, pos_rel, slope_rows, seq):
    t, d = x2d.shape
    rows = min(PROJ_ROWS, seq)
    per_seq = seq // rows
    row = lambda i: (i, 0)
    fixed = lambda i: (0, 0)
    full = lambda a: pl.BlockSpec(a.shape, fixed)
    row_out = lambda n: (pl.BlockSpec((rows, n), row), jax.ShapeDtypeStruct((t, n), BF16))
    vt_out = lambda n: (pl.BlockSpec((1, n, rows), lambda i: (i // per_seq, 0, i % per_seq)),
                        jax.ShapeDtypeStruct((t // seq, n, seq), BF16))
    outs = [row_out(DIFF_WIDTH), row_out(DIFF_HEADS * LANES), vt_out(DIFF_WIDTH),
            row_out(MLA_HEADS * MLA_SLAB), row_out(MLA_HEADS * MLA_SLAB), vt_out(MLA_WIDTH)]
    return pl.pallas_call(
        _proj_kernel,
        grid=(t // rows,),
        in_specs=[pl.BlockSpec((rows, d), row), full(g), full(w_all), full(gq), full(wq2),
                  full(gkv), full(wkv2), pl.BlockSpec((rows, MLA_SLAB), row),
                  pl.BlockSpec((rows, MLA_SLAB), row), pl.BlockSpec((rows, 1), row),
                  full(slope_rows)],
        out_specs=[spec for spec, _ in outs],
        out_shape=[shape for _, shape in outs],
        compiler_params=_params("parallel"),
        name="proj",
    )(x2d, g, w_all, gq, wq2, gkv, wkv2, cos_t, sin_t, pos_rel, slope_rows)


def _softmax_step(s, vt_chunk, m, l, acc):
    m_new = jnp.maximum(m, jnp.max(s, axis=0, keepdims=True))
    alpha = jnp.exp2(m - m_new)
    p = jnp.exp2(s - m_new)
    l = alpha * l + jnp.sum(p, axis=0, keepdims=True)
    acc = alpha * acc + _dot(vt_chunk, p.astype(BF16))
    return m_new, l, acc


def _causal_mask(s, key0, query0, tq):
    key = key0 + lax.broadcasted_iota(jnp.int32, s.shape, 0)
    query = query0 + (lax.broadcasted_iota(jnp.int32, s.shape, 1) & (tq - 1))
    return jnp.where(key <= query, s, MASK_VALUE)


def _causal_sweep(scores, consume, carry, n_full, slot_a, slot_b):
    scores(0, slot_a)

    def pair(i, c):
        j = 2 * i
        scores(j + 1, slot_b)
        c = consume(j, slot_a, c, False)
        scores(j + 2, slot_a)
        return consume(j + 1, slot_b, c, False)

    carry = lax.fori_loop(0, lax.shift_right_logical(n_full, 1), pair, carry)

    def odd_tail(c):
        scores(n_full, slot_b)
        c = consume(n_full - 1, slot_a, c, False)
        return consume(n_full, slot_b, c, True)

    def even_tail(c):
        return consume(n_full, slot_a, c, True)

    return lax.cond((n_full & 1) == 1, odd_tail, even_tail, carry)


def _diff_attn_kernel(lam_ref, q_ref, k_ref, vt_ref, g_ref, o_ref, qs_ref, sa_ref, sb_ref,
                      *, tq, tk):
    qi = pl.program_id(2)
    q = q_ref[0]
    lane = lax.broadcasted_iota(jnp.int32, (tq, LANES), 1)
    for v in range(4):
        keep = (lane >= v * DIFF_HALF_DIM) & (lane < (v + 1) * DIFF_HALF_DIM)
        ones = (lane >= 3 * (v // 2)) & (lane < 3 * (v // 2) + 3)
        qs_ref[v * tq:(v + 1) * tq, 0:LANES] = jnp.where(keep, q, jnp.zeros_like(q))
        qs_ref[v * tq:(v + 1) * tq, LANES:2 * LANES] = jnp.where(ones, 1.0, 0.0).astype(BF16)

    def scores(j, slot):
        start = pl.multiple_of(j * tk, tk)
        slot[...] = lax.dot_general(k_ref[0, pl.ds(start, tk), :], qs_ref[...], _NT,
                                    preferred_element_type=F32)

    def consume(j, slot, carry, masked):
        start = pl.multiple_of(j * tk, tk)
        s = slot[...]
        if masked:
            s = _causal_mask(s, start, qi * tq, tq)
        return _softmax_step(s, vt_ref[0, :, pl.ds(start, tk)], *carry)

    init = (jnp.full((1, 4 * tq), MASK_VALUE, F32), jnp.zeros((1, 4 * tq), F32),
            jnp.zeros((LANES, 4 * tq), F32))
    _, l, acc = _causal_sweep(scores, consume, init, (qi * tq) // tk, sa_ref, sb_ref)

    on = acc * (1.0 / l)
    lam = lam_ref[0]
    halves = []
    for h in range(2):
        rows = slice(h * DIFF_V_DIM, (h + 1) * DIFF_V_DIM)
        o = on[rows, 2 * h * tq:(2 * h + 1) * tq] - lam * on[rows, (2 * h + 1) * tq:(2 * h + 2) * tq]
        halves.append(o * lax.rsqrt(jnp.mean(o * o, axis=0, keepdims=True) + RMS_EPS))
    o_ref[0] = (jnp.concatenate(halves, axis=0) * g_ref[...]).T.astype(o_ref.dtype)


def _diff_attn_call(lam, qd, kd, vdt, g_col, tq, tk):
    b, s, _ = qd.shape
    pairs = DIFF_HEADS // 2
    return pl.pallas_call(
        functools.partial(_diff_attn_kernel, tq=tq, tk=tk),
        grid=(b, pairs, s // tq),
        in_specs=[pl.BlockSpec(memory_space=pltpu.SMEM),
                  pl.BlockSpec((1, tq, LANES), lambda bi, p, qi: (bi, qi, p)),
                  pl.BlockSpec((1, s, 2 * LANES), lambda bi, p, qi: (bi, 0, p)),
                  pl.BlockSpec((1, LANES, s), lambda bi, p, qi: (bi, p, 0)),
                  pl.BlockSpec((LANES, 1), lambda bi, p, qi: (0, 0))],
        out_specs=pl.BlockSpec((1, tq, LANES), lambda bi, p, qi: (bi, qi, p)),
        out_shape=jax.ShapeDtypeStruct((b, s, DIFF_WIDTH), BF16),
        scratch_shapes=[pltpu.VMEM((4 * tq, 2 * LANES), BF16), pltpu.VMEM((tk, 4 * tq), F32),
                        pltpu.VMEM((tk, 4 * tq), F32)],
        compiler_params=_params("parallel", "parallel", "arbitrary"),
        name="diff_attn",
    )(lam, qd, kd, vdt, g_col)


def _mla_attn_kernel(q_ref, k_ref, vt_ref, o_ref, sa_ref, sb_ref, *, tq, tk):
    qi = pl.program_id(2)
    slabs = [slice(hh * MLA_SLAB, (hh + 1) * MLA_SLAB) for hh in range(2)]

    def scores(j, slot):
        start = pl.multiple_of(j * tk, tk)
        for hh, sl in enumerate(slabs):
            slot[hh] = lax.dot_general(k_ref[0, pl.ds(start, tk), sl], q_ref[0, :, sl], _NT,
                                       preferred_element_type=F32)

    def consume(j, slot, carry, masked):
        start = pl.multiple_of(j * tk, tk)
        vtc = vt_ref[0, :, pl.ds(start, tk)]
        out = []
        for hh in range(2):
            s = slot[hh]
            if masked:
                s = _causal_mask(s, start, qi * tq, tq)
            out.extend(_softmax_step(s, vtc, *carry[3 * hh:3 * hh + 3]))
        return tuple(out)

    init = (jnp.full((1, tq), MASK_VALUE, F32), jnp.zeros((1, tq), F32),
            jnp.zeros((LANES, tq), F32)) * 2
    _, l0, acc0, _, l1, acc1 = _causal_sweep(scores, consume, init, (qi * tq) // tk,
                                             sa_ref, sb_ref)
    o = jnp.concatenate([(acc0 * (1.0 / l0))[:MLA_V_DIM], (acc1 * (1.0 / l1))[MLA_V_DIM:]], axis=0)
    o_ref[0] = o.T.astype(o_ref.dtype)


def _mla_attn_call(qm, km, vmt, tq, tk):
    b, s, _ = qm.shape
    pairs = MLA_HEADS // 2
    return pl.pallas_call(
        functools.partial(_mla_attn_kernel, tq=tq, tk=tk),
        grid=(b, pairs, s // tq),
        in_specs=[pl.BlockSpec((1, tq, 2 * MLA_SLAB), lambda bi, p, qi: (bi, qi, p)),
                  pl.BlockSpec((1, s, 2 * MLA_SLAB), lambda bi, p, qi: (bi, 0, p)),
                  pl.BlockSpec((1, LANES, s), lambda bi, p, qi: (bi, p, 0))],
        out_specs=pl.BlockSpec((1, tq, LANES), lambda bi, p, qi: (bi, qi, p)),
        out_shape=jax.ShapeDtypeStruct((b, s, MLA_WIDTH), BF16),
        scratch_shapes=[pltpu.VMEM((2, tk, tq), F32), pltpu.VMEM((2, tk, tq), F32)],
        compiler_params=_params("parallel", "parallel", "arbitrary"),
        name="mla_attn",
    )(qm, km, vmt)


def _mem_kv_kernel(mem_ref, g_ref, w_ref, k_ref, v_ref):
    kv = _dot(_rms(mem_ref[0], g_ref[...]).astype(BF16), w_ref[...])
    k_ref[0] = kv[:, :MEM_WIDTH].astype(BF16)
    v_ref[0] = kv[:, MEM_WIDTH:].astype(BF16)


def _mem_kv_call(mem, g, w_kv):
    b, n_mem, d = mem.shape
    blk = pl.BlockSpec((1, n_mem, MEM_WIDTH), lambda bi: (bi, 0, 0))
    return pl.pallas_call(
        _mem_kv_kernel,
        grid=(b,),
        in_specs=[pl.BlockSpec((1, n_mem, d), lambda bi: (bi, 0, 0)),
                  pl.BlockSpec(g.shape, lambda bi: (0, 0)),
                  pl.BlockSpec(w_kv.shape, lambda bi: (0, 0))],
        out_specs=[blk, blk],
        out_shape=[jax.ShapeDtypeStruct((b, n_mem, MEM_WIDTH), BF16)] * 2,
        compiler_params=_params("parallel"),
        name="mem_kv",
    )(mem, g, w_kv)


def _mix_kernel(x_ref, d_ref, m_ref, gm_ref, wo_ref, gc_ref, wq_ref, km_ref, vm_ref, wmo_ref,
                gf_ref, wr_ref, br_ref, x2_ref, h3_ref, r_ref, rt_ref, cnt_ref, run_ref):
    rows = x_ref.shape[0]

    @pl.when(pl.program_id(0) == 0)
    def _():
        run_ref[...] = jnp.zeros_like(run_ref)

    mla_n = _rms(m_ref[...].astype(F32), gm_ref[...]).astype(BF16)
    x1 = (x_ref[...] + _dot(d_ref[...], wo_ref[0:DIFF_WIDTH, :])
          + _dot(mla_n, wo_ref[DIFF_WIDTH:DIFF_WIDTH + MLA_WIDTH, :]))

    qx = _dot(_rms(x1, gc_ref[...]).astype(BF16), wq_ref[...])
    heads = []
    for h in range(MEM_HEADS):
        sl = slice(h * MEM_HEAD_DIM, (h + 1) * MEM_HEAD_DIM)
        sc = lax.dot_general(qx[:, sl].astype(BF16), km_ref[0, :, sl], _NT,
                             preferred_element_type=F32) * MEM_HEAD_DIM ** -0.5
        p = jnp.exp(sc - jnp.max(sc, axis=-1, keepdims=True))
        o = _dot(p.astype(BF16), vm_ref[0, :, sl]) * (1.0 / jnp.sum(p, axis=-1, keepdims=True))
        heads.append(o.astype(BF16))
    x2 = x1 + _dot(jnp.concatenate(heads, axis=1), wmo_ref[...])
    x2_ref[...] = x2

    h3 = _rms(x2, gf_ref[...])
    h3_ref[...] = h3
    logits = jnp.dot(h3, wr_ref[...], preferred_element_type=F32,
                     precision=lax.Precision.HIGHEST) + br_ref[...]
    lane = lax.broadcasted_iota(jnp.int32, (rows, LANES), 1).astype(F32)
    far = float(4 * LANES)

    def first_argmax(vals):
        top = jnp.max(vals, axis=-1, keepdims=True)
        return top, jnp.min(jnp.where(vals == top, lane, far), axis=-1, keepdims=True)

    g_logits = jnp.where(lane < N_GROUPS, logits, MASK_VALUE)
    g_top, g_sel = first_argmax(g_logits)
    g_gate = 1.0 / jnp.sum(jnp.exp(g_logits - g_top), axis=-1, keepdims=True)
    lo = ROUTER_LANE0 + EXPERTS_PER_GROUP * g_sel
    e_logits = jnp.where(lane >= lo, jnp.where(lane < lo + EXPERTS_PER_GROUP, logits, MASK_VALUE),
                         MASK_VALUE)
    top1, lane1 = first_argmax(e_logits)
    top2, lane2 = first_argmax(jnp.where(lane == lane1, MASK_VALUE, e_logits))
    t = jnp.exp(top2 - top1)
    gate1 = g_gate / (1.0 + t)
    gate2 = gate1 * t

    hot = jnp.where(lane == lane1, 1.0, 0.0) + jnp.where(lane == lane2, 1.0, 0.0)
    tri = (lax.broadcasted_iota(jnp.int32, (rows, rows), 1)
           < lax.broadcasted_iota(jnp.int32, (rows, rows), 0))
    before = _dot(jnp.where(tri, 1.0, 0.0).astype(BF16), hot.astype(BF16)) + run_ref[0:1, :]
    rank1 = jnp.sum(jnp.where(lane == lane1, before, 0.0), axis=-1, keepdims=True)
    rank2 = jnp.sum(jnp.where(lane == lane2, before, 0.0), axis=-1, keepdims=True)
    run_ref[...] = run_ref[...] + jnp.sum(hot, axis=0, keepdims=True)
    cnt_ref[...] = run_ref[...]

    fields = (lane1 - ROUTER_LANE0, lane2 - ROUTER_LANE0, gate1, gate2, rank1, rank2)
    out = jnp.zeros((rows, LANES), F32)
    for i, f in enumerate(fields):
        out = jnp.where(lane == float(i), f, out)
    r_ref[...] = out
    rt_ref[...] = out.T[0:8, :]


def _mix_call(x2d, diff, mla, gm, wo, gc, wq, kmem, vmem, wmo, gf, wr, br, seq):
    t, d = x2d.shape
    rows = min(MIX_ROWS, seq)
    per_seq = seq // rows
    row = lambda i: (i, 0)
    fixed = lambda i: (0, 0)
    full = lambda a: pl.BlockSpec(a.shape, fixed)
    memblk = pl.BlockSpec((1,) + kmem.shape[1:], lambda i: (i // per_seq, 0, 0))
    return pl.pallas_call(
        _mix_kernel,
        grid=(t // rows,),
        in_specs=[pl.BlockSpec((rows, d), row), pl.BlockSpec((rows, DIFF_WIDTH), row),
                  pl.BlockSpec((rows, MLA_WIDTH), row), full(gm), full(wo), full(gc), full(wq),
                  memblk, memblk, full(wmo), full(gf), full(wr), full(br)],
        out_specs=[pl.BlockSpec((rows, d), row), pl.BlockSpec((rows, d), row),
                   pl.BlockSpec((rows, LANES), row), pl.BlockSpec((8, rows), lambda i: (0, i)),
                   pl.BlockSpec((8, LANES), fixed)],
        out_shape=[jax.ShapeDtypeStruct((t, d), F32), jax.ShapeDtypeStruct((t, d), F32),
                   jax.ShapeDtypeStruct((t, LANES), F32), jax.ShapeDtypeStruct((8, t), F32),
                   jax.ShapeDtypeStruct((8, LANES), F32)],
        scratch_shapes=[pltpu.VMEM((8, LANES), F32)],
        compiler_params=_params("arbitrary"),
        name="mix",
    )(x2d, diff, mla, gm, wo, gc, wq, kmem, vmem, wmo, gf, wr, br)


def _dispatch_kernel(d0_ref, d1_ref, h_ref, xs_in_ref, xs_ref, sem):
    del xs_in_ref
    rows = h_ref.shape[0]

    def issue(r, c):
        src = h_ref.at[pl.ds(r, 1)]
        pltpu.make_async_copy(src, xs_ref.at[pl.ds(d0_ref[0, 0, r], 1)], sem).start()
        pltpu.make_async_copy(src, xs_ref.at[pl.ds(d1_ref[0, 0, r], 1)], sem).start()
        return c

    lax.fori_loop(0, rows, issue, 0)
    for _ in range(2):
        pltpu.make_async_copy(h_ref, xs_ref.at[pl.ds(0, rows)], sem).wait()


def _dispatch_call(d0, d1, h3, xs0, rows):
    t, d = h3.shape
    idx = pl.BlockSpec((1, 1, rows), lambda i: (i, 0, 0), memory_space=pltpu.SMEM)
    return pl.pallas_call(
        _dispatch_kernel,
        grid=(t // rows,),
        in_specs=[idx, idx, pl.BlockSpec((rows, d), lambda i: (i, 0)),
                  pl.BlockSpec(memory_space=pl.ANY)],
        out_specs=pl.BlockSpec(memory_space=pl.ANY),
        out_shape=jax.ShapeDtypeStruct(xs0.shape, xs0.dtype),
        scratch_shapes=[pltpu.SemaphoreType.DMA(())],
        input_output_aliases={3: 0},
        compiler_params=_params("arbitrary"),
        name="dispatch",
    )(d0, d1, h3, xs0)


def _expert_kernel(be_ref, x_ref, wg_ref, wu_ref, wd_ref, y_ref):
    del be_ref
    x = x_ref[...].astype(BF16)
    gate = _dot(x, wg_ref[0])
    up = _dot(x, wu_ref[0])
    hidden = gate * (1.0 / (1.0 + jnp.exp(-gate))) * up
    y_ref[...] = _dot(hidden.astype(BF16), wd_ref[0])


def _expert_call(block_expert, xs, wg, wu, wd, rows):
    n_slots, d = xs.shape
    ff = wg.shape[-1]
    grid_spec = pltpu.PrefetchScalarGridSpec(
        num_scalar_prefetch=1,
        grid=(n_slots // rows,),
        in_specs=[pl.BlockSpec((rows, d), lambda i, be: (i, 0)),
                  pl.BlockSpec((1, d, ff), lambda i, be: (be[i], 0, 0)),
                  pl.BlockSpec((1, d, ff), lambda i, be: (be[i], 0, 0)),
                  pl.BlockSpec((1, ff, d), lambda i, be: (be[i], 0, 0))],
        out_specs=pl.BlockSpec((rows, d), lambda i, be: (i, 0)),
    )
    return pl.pallas_call(
        _expert_kernel,
        grid_spec=grid_spec,
        out_shape=jax.ShapeDtypeStruct((n_slots, d), F32),
        compiler_params=_params("arbitrary"),
        name="experts",
    )(block_expert, xs, wg, wu, wd)


def _combine_kernel(d0_ref, d1_ref, x_ref, r_ref, g_ref, y_ref, o_ref, buf0, buf1, sem):
    rows = x_ref.shape[0]

    def issue(r, c):
        pltpu.make_async_copy(y_ref.at[pl.ds(d0_ref[0, 0, r], 1)], buf0.at[pl.ds(r, 1)], sem).start()
        pltpu.make_async_copy(y_ref.at[pl.ds(d1_ref[0, 0, r], 1)], buf1.at[pl.ds(r, 1)], sem).start()
        return c

    lax.fori_loop(0, rows, issue, 0)
    for buf in (buf0, buf1):
        pltpu.make_async_copy(y_ref.at[pl.ds(0, rows)], buf, sem).wait()
    gates = r_ref[...]
    x3 = x_ref[...] + gates[:, 2:3] * buf0[...] + gates[:, 3:4] * buf1[...]
    o_ref[...] = _rms(x3, g_ref[...])


def _combine_call(d0, d1, x2, router, g, y, rows):
    t, d = x2.shape
    idx = pl.BlockSpec((1, 1, rows), lambda i: (i, 0, 0), memory_space=pltpu.SMEM)
    row = lambda i: (i, 0)
    return pl.pallas_call(
        _combine_kernel,
        grid=(t // rows,),
        in_specs=[idx, idx, pl.BlockSpec((rows, d), row), pl.BlockSpec((rows, LANES), row),
                  pl.BlockSpec(g.shape, lambda i: (0, 0)), pl.BlockSpec(memory_space=pl.ANY)],
        out_specs=pl.BlockSpec((rows, d), row),
        out_shape=jax.ShapeDtypeStruct((t, d), F32),
        scratch_shapes=[pltpu.VMEM((rows, d), F32), pltpu.VMEM((rows, d), F32),
                        pltpu.SemaphoreType.DMA(())],
        compiler_params=_params("arbitrary"),
        name="combine",
    )(d0, d1, x2, router, g, y)


def _slab(cols, lo):
    return jnp.pad(cols, ((0, 0), (lo, MLA_SLAB - lo - cols.shape[1])))


def _layer_weights(w_in, w_uq, w_ukv):
    half = MLA_ROPE_DIM // 2
    o = 3 * DIFF_WIDTH + MLA_Q_RANK + MLA_KV_RANK
    kpe1, kpe2 = w_in[:, o:o + half], w_in[:, o + half:o + 2 * half]
    w_all = jnp.concatenate(
        [w_in[:, :o],
         _slab(jnp.concatenate([kpe1, kpe2], axis=1), MLA_NOPE_DIM),
         _slab(jnp.concatenate([kpe2, kpe1], axis=1), MLA_NOPE_DIM)], axis=1).astype(BF16)
    q_main, q_swap, k_nope, v_cols = [], [], [], []
    for h in range(MLA_HEADS):
        c = w_uq[:, h * MLA_QK_DIM:(h + 1) * MLA_QK_DIM]
        pe1 = c[:, MLA_NOPE_DIM:MLA_NOPE_DIM + half]
        pe2 = c[:, MLA_NOPE_DIM + half:]
        q_main.append(_slab(c, 0))
        q_swap.append(_slab(jnp.concatenate([pe2, pe1], axis=1), MLA_NOPE_DIM))
        kv = w_ukv[:, h * (MLA_NOPE_DIM + MLA_V_DIM):(h + 1) * (MLA_NOPE_DIM + MLA_V_DIM)]
        k_nope.append(_slab(kv[:, :MLA_NOPE_DIM], 0))
        v_cols.append(kv[:, MLA_NOPE_DIM:])
    wq2 = jnp.concatenate(q_main + q_swap, axis=1).astype(BF16)
    wkv2 = jnp.concatenate(k_nope + v_cols, axis=1).astype(BF16)
    return w_all, wq2, wkv2


def _rope_tables(positions):
    inv_freq = ROPE_THETA ** (-jnp.arange(0, MLA_ROPE_DIM, 2, dtype=F32) / MLA_ROPE_DIM)
    ang = positions.astype(F32).reshape(-1, 1) * inv_freq
    cos, sin = jnp.cos(ang), jnp.sin(ang)
    t = ang.shape[0]
    pad = jnp.zeros((t, MLA_SLAB - MLA_QK_DIM), F32)
    cos_t = jnp.concatenate([jnp.ones((t, MLA_NOPE_DIM), F32), cos, cos, pad], axis=1)
    sin_t = jnp.concatenate([jnp.zeros((t, MLA_NOPE_DIM), F32), -sin, sin, pad], axis=1)
    return cos_t, sin_t


def kernel(x, mem, positions, norm_mix_g, w_in, diff_lambda_q1, diff_lambda_k1, diff_lambda_q2, diff_lambda_k2, diff_out_g, mla_q_norm_g, w_mla_uq, mla_kv_norm_g, w_mla_ukv, mla_out_g, w_out, norm_cross_g, norm_mem_g, w_mem_q, w_mem_kv, w_mem_o, norm_ffn_g, w_group_router, b_group_router, w_expert_router, b_expert_router, w_expert_gate, w_expert_up, w_expert_down, norm_final_g):
    b, s, d = x.shape
    t = b * s
    depth = w_in.shape[0]
    row2 = lambda v: v.reshape(1, -1).astype(F32)
    cos_t, sin_t = _rope_tables(positions)
    pos_f = positions.astype(F32)
    slopes = 2.0 ** (-8.0 * jnp.arange(1, DIFF_HEADS + 1, dtype=F32) / DIFF_HEADS)
    pos_rel = (pos_f - pos_f[:, :1]).reshape(t, 1)
    slope_rows = jnp.pad(jnp.repeat((LOG2E * slopes).reshape(DIFF_HEADS // 2, 2), 3, axis=1),
                         ((0, 0), (0, LANES - 6)))

    diff_tq, mla_tq = min(DIFF_TQ, s), min(MLA_TQ, s)
    attn_tk = min(ATTN_TK, s)
    moe_rows = min(MOE_ROWS, t)
    blk = EXPERT_ROWS
    n_assign = 2 * t
    n_blocks = -(-(n_assign + N_EXPERTS * (blk - 1)) // blk)
    n_slots = n_blocks * blk

    assert depth == 1, "single-layer problem: the combine kernel applies the final RMSNorm"
    l = 0
    x2d = x.reshape(t, d)
    w_all, wq2, wkv2 = _layer_weights(w_in[l], w_mla_uq[l], w_mla_ukv[l])
    qd, kd, vdt, qm, km, vmt = _proj_call(
        x2d, row2(norm_mix_g[l]), w_all, row2(mla_q_norm_g[l]), wq2,
        row2(mla_kv_norm_g[l]), wkv2, cos_t, sin_t, pos_rel, slope_rows, s)

    lambda_init = 0.8 - 0.6 * math.exp(-0.3 * l)
    lam = (jnp.exp(jnp.sum(diff_lambda_q1[l] * diff_lambda_k1[l]))
           - jnp.exp(jnp.sum(diff_lambda_q2[l] * diff_lambda_k2[l])) + lambda_init)
    gain_col = jnp.tile(diff_out_g[l] * (1.0 - lambda_init), 2).reshape(-1, 1).astype(F32)
    shp = lambda a: a.reshape(b, s, a.shape[-1])
    diff_out = _diff_attn_call(lam.reshape(1).astype(F32), shp(qd), shp(kd), vdt, gain_col,
                               diff_tq, attn_tk)
    mla_out = _mla_attn_call(shp(qm), shp(km), vmt, mla_tq, attn_tk)

    kmem, vmem = _mem_kv_call(mem, row2(norm_mem_g[l]), w_mem_kv[l].astype(BF16))
    w_router = jnp.pad(
        jnp.concatenate([w_group_router[l], w_expert_router[l]], axis=1),
        ((0, 0), (0, LANES - N_GROUPS - N_EXPERTS)))
    b_router = jnp.pad(jnp.concatenate([b_group_router[l], b_expert_router[l]]),
                       (0, LANES - N_GROUPS - N_EXPERTS))
    x2, h3, router, router_t, counts = _mix_call(
        x2d, diff_out.reshape(t, DIFF_WIDTH), mla_out.reshape(t, MLA_WIDTH),
        row2(mla_out_g[l]), w_out[l].astype(BF16), row2(norm_cross_g[l]),
        w_mem_q[l].astype(BF16), kmem, vmem, w_mem_o[l].astype(BF16),
        row2(norm_ffn_g[l]), w_router, row2(b_router), s)

    cnt = counts[0, ROUTER_LANE0:ROUTER_LANE0 + N_EXPERTS].astype(jnp.int32)
    padded = ((cnt + blk - 1) // blk) * blk
    seg_end = jnp.cumsum(padded)
    seg_start = seg_end - padded
    ridx = router_t.astype(jnp.int32)
    expert_ids = jnp.arange(N_EXPERTS, dtype=jnp.int32)[:, None]

    def slot_of(choice):
        start = jnp.sum(jnp.where(ridx[choice][None, :] == expert_ids, seg_start[:, None], 0),
                        axis=0)
        return (start + ridx[4 + choice]).reshape(t // moe_rows, 1, moe_rows)

    dest0, dest1 = slot_of(0), slot_of(1)
    block_start = jnp.arange(n_blocks, dtype=jnp.int32) * blk
    block_expert = jnp.minimum(
        jnp.sum((seg_end[None, :] <= block_start[:, None]).astype(jnp.int32), axis=1),
        N_EXPERTS - 1)

    xs = _dispatch_call(dest0, dest1, h3, jnp.zeros((n_slots, d), F32), moe_rows)
    y = _expert_call(block_expert, xs, w_expert_gate[l].astype(BF16),
                     w_expert_up[l].astype(BF16), w_expert_down[l].astype(BF16), blk)
    out = _combine_call(dest0, dest1, x2, router, row2(norm_final_g), y, moe_rows)
    return out.reshape(b, s, d)
```

```python
import functools
import math

import jax
import jax.numpy as jnp
from jax import lax
from jax.experimental import pallas as pl
from jax.experimental.pallas import tpu as pltpu

F32 = jnp.float32
BF16 = jnp.bfloat16

RMS_EPS = 1e-6
ROPE_THETA = 10000.0
MASK_VALUE = -1e30

DIFF_HEADS = 8
DIFF_HALF_DIM = 32
DIFF_V_DIM = 64
DIFF_WIDTH = DIFF_HEADS * DIFF_V_DIM

MLA_HEADS = 8
MLA_Q_RANK = 256
MLA_KV_RANK = 128
MLA_NOPE_DIM = 64
MLA_ROPE_DIM = 32
MLA_QK_DIM = MLA_NOPE_DIM + MLA_ROPE_DIM
MLA_V_DIM = 64
MLA_WIDTH = MLA_HEADS * MLA_V_DIM

MEM_HEADS = 4
MEM_HEAD_DIM = 128
MEM_WIDTH = MEM_HEADS * MEM_HEAD_DIM

N_GROUPS = 4
EXPERTS_PER_GROUP = 8
N_EXPERTS = N_GROUPS * EXPERTS_PER_GROUP
EXPERT_FF = 256

LANES = 128
MLA_SLAB = LANES
ROUTER_LANE0 = N_GROUPS

PROJ_ROWS = 512
MIX_ROWS = 512
MOE_ROWS = 256
EXPERT_ROWS = 256
ATTN_TK = 512
DIFF_TQ = 512
MLA_TQ = 512
LOG2E = math.log2(math.e)
VMEM_LIMIT = 48 * 1024 * 1024

_NT = (((1,), (1,)), ((), ()))


def _rms(x, gain):
    return x * lax.rsqrt(jnp.mean(x * x, axis=-1, keepdims=True) + RMS_EPS) * gain


def _dot(a, b):
    return jnp.dot(a, b, preferred_element_type=F32)


def _params(*sem):
    return pltpu.CompilerParams(dimension_semantics=sem, vmem_limit_bytes=VMEM_LIMIT)


def _proj_kernel(x_ref, g_ref, w_ref, gq_ref, wq_ref, gkv_ref, wkv_ref, cos_ref, sin_ref,
                 pos_ref, slope_ref, qd_ref, kd_ref, vd_ref, qm_ref, km_ref, vm_ref):
    xn = _rms(x_ref[...], g_ref[...]).astype(BF16)
    r = _dot(xn, w_ref[...])
    w = DIFF_WIDTH
    qd_ref[...] = (r[:, 0:w] * (DIFF_HALF_DIM ** -0.5 * LOG2E)).astype(BF16)
    lane = lax.broadcasted_iota(jnp.int32, (x_ref.shape[0], LANES), 1)
    for p in range(DIFF_HEADS // 2):
        bias = pos_ref[...] * slope_ref[p:p + 1, :]
        hi = bias.astype(BF16)
        mid = (bias - hi.astype(F32)).astype(BF16)
        low = (bias - hi.astype(F32) - mid.astype(F32)).astype(BF16)
        term = jnp.where((lane == 0) | (lane == 3), hi, jnp.where((lane == 1) | (lane == 4), mid, low))
        kd_ref[:, 2 * p * LANES:(2 * p + 1) * LANES] = r[:, w + p * LANES:w + (p + 1) * LANES].astype(BF16)
        kd_ref[:, (2 * p + 1) * LANES:(2 * p + 2) * LANES] = term
    vd_ref[0] = r[:, 2 * w:3 * w].T.astype(BF16)
    o = 3 * w
    cq = r[:, o:o + MLA_Q_RANK]
    o += MLA_Q_RANK
    ckv = r[:, o:o + MLA_KV_RANK]
    o += MLA_KV_RANK
    cos = cos_ref[...]
    sin = sin_ref[...]
    kpe = r[:, o:o + MLA_SLAB] * cos + r[:, o + MLA_SLAB:o + 2 * MLA_SLAB] * sin
    q2 = _dot(_rms(cq, gq_ref[...]).astype(BF16), wq_ref[...])
    kv2 = _dot(_rms(ckv, gkv_ref[...]).astype(BF16), wkv_ref[...])
    sw = MLA_HEADS * MLA_SLAB
    for h in range(MLA_HEADS):
        lo = h * MLA_SLAB
        qh = q2[:, lo:lo + MLA_SLAB] * cos + q2[:, sw + lo:sw + lo + MLA_SLAB] * sin
        qm_ref[:, lo:lo + MLA_SLAB] = (qh * (MLA_QK_DIM ** -0.5 * LOG2E)).astype(BF16)
        km_ref[:, lo:lo + MLA_SLAB] = (kv2[:, lo:lo + MLA_SLAB] + kpe).astype(BF16)
    vm_ref[0] = kv2[:, sw:sw + MLA_WIDTH].T.astype(BF16)


def _proj_call(x2d, g, w_all, gq, wq2, gkv, wkv2, cos_t, sin_t, pos_rel, slope_rows, seq):
    t, d = x2d.shape
    rows = min(PROJ_ROWS, seq)
    per_seq = seq // rows
    row = lambda i: (i, 0)
    fixed = lambda i: (0, 0)
    full = lambda a: pl.BlockSpec(a.shape, fixed)
    row_out = lambda n: (pl.BlockSpec((rows, n), row), jax.ShapeDtypeStruct((t, n), BF16))
    vt_out = lambda n: (pl.BlockSpec((1, n, rows), lambda i: (i // per_seq, 0, i % per_seq)),
                        jax.ShapeDtypeStruct((t // seq, n, seq), BF16))
    outs = [row_out(DIFF_WIDTH), row_out(DIFF_HEADS * LANES), vt_out(DIFF_WIDTH),
            row_out(MLA_HEADS * MLA_SLAB), row_out(MLA_HEADS * MLA_SLAB), vt_out(MLA_WIDTH)]
    return pl.pallas_call(
        _proj_kernel,
        grid=(t // rows,),
        in_specs=[pl.BlockSpec((rows, d), row), full(g), full(w_all), full(gq), full(wq2),
                  full(gkv), full(wkv2), pl.BlockSpec((rows, MLA_SLAB), row),
                  pl.BlockSpec((rows, MLA_SLAB), row), pl.BlockSpec((rows, 1), row),
                  full(slope_rows)],
        out_specs=[spec for spec, _ in outs],
        out_shape=[shape for _, shape in outs],
        compiler_params=_params("parallel"),
        name="proj",
    )(x2d, g, w_all, gq, wq2, gkv, wkv2, cos_t, sin_t, pos_rel, slope_rows)


def _softmax_step(s, vt_chunk, m, l, acc):
    m_new = jnp.maximum(m, jnp.max(s, axis=0, keepdims=True))
    alpha = jnp.exp2(m - m_new)
    p = jnp.exp2(s - m_new)
    l = alpha * l + jnp.sum(p, axis=0, keepdims=True)
    acc = alpha * acc + _dot(vt_chunk, p.astype(BF16))
    return m_new, l, acc


def _causal_mask(s, key0, query0, tq):
    key = key0 + lax.broadcasted_iota(jnp.int32, s.shape, 0)
    query = query0 + (lax.broadcasted_iota(jnp.int32, s.shape, 1) & (tq - 1))
    return jnp.where(key <= query, s, MASK_VALUE)


def _causal_sweep(scores, consume, carry, n_full, slot_a, slot_b):
    scores(0, slot_a)

    def pair(i, c):
        j = 2 * i
        scores(j + 1, slot_b)
        c = consume(j, slot_a, c, False)
        scores(j + 2, slot_a)
        return consume(j + 1, slot_b, c, False)

    carry = lax.fori_loop(0, lax.shift_right_logical(n_full, 1), pair, carry)

    def odd_tail(c):
        scores(n_full, slot_b)
        c = consume(n_full - 1, slot_a, c, False)
        return consume(n_full, slot_b, c, True)

    def even_tail(c):
        return consume(n_full, slot_a, c, True)

    return lax.cond((n_full & 1) == 1, odd_tail, even_tail, carry)


def _diff_attn_kernel(lam_ref, q_ref, k_ref, vt_ref, g_ref, o_ref, qs_ref, sa_ref, sb_ref,
                      *, tq, tk):
    qi = pl.program_id(2)
    q = q_ref[0]
    lane = lax.broadcasted_iota(jnp.int32, (tq, LANES), 1)
    for v in range(4):
        keep = (lane >= v * DIFF_HALF_DIM) & (lane < (v + 1) * DIFF_HALF_DIM)
        ones = (lane >= 3 * (v // 2)) & (lane < 3 * (v // 2) + 3)
        qs_ref[v * tq:(v + 1) * tq, 0:LANES] = jnp.where(keep, q, jnp.zeros_like(q))
        qs_ref[v * tq:(v + 1) * tq, LANES:2 * LANES] = jnp.where(ones, 1.0, 0.0).astype(BF16)

    def scores(j, slot):
        start = pl.multiple_of(j * tk, tk)
        slot[...] = lax.dot_general(k_ref[0, pl.ds(start, tk), :], qs_ref[...], _NT,
                                    preferred_element_type=F32)

    def consume(j, slot, carry, masked):
        start = pl.multiple_of(j * tk, tk)
        s = slot[...]
        if masked:
            s = _causal_mask(s, start, qi * tq, tq)
        return _softmax_step(s, vt_ref[0, :, pl.ds(start, tk)], *carry)

    init = (jnp.full((1, 4 * tq), MASK_VALUE, F32), jnp.zeros((1, 4 * tq), F32),
            jnp.zeros((LANES, 4 * tq), F32))
    _, l, acc = _causal_sweep(scores, consume, init, (qi * tq) // tk, sa_ref, sb_ref)

    on = acc * (1.0 / l)
    lam = lam_ref[0]
    halves = []
    for h in range(2):
        rows = slice(h * DIFF_V_DIM, (h + 1) * DIFF_V_DIM)
        o = on[rows, 2 * h * tq:(2 * h + 1) * tq] - lam * on[rows, (2 * h + 1) * tq:(2 * h + 2) * tq]
        halves.append(o * lax.rsqrt(jnp.mean(o * o, axis=0, keepdims=True) + RMS_EPS))
    o_ref[0] = (jnp.concatenate(halves, axis=0) * g_ref[...]).T.astype(o_ref.dtype)


def _diff_attn_call(lam, qd, kd, vdt, g_col, tq, tk):
    b, s, _ = qd.shape
    pairs = DIFF_HEADS // 2
    return pl.pallas_call(
        functools.partial(_diff_attn_kernel, tq=tq, tk=tk),
        grid=(b, pairs, s // tq),
        in_specs=[pl.BlockSpec(memory_space=pltpu.SMEM),
                  pl.BlockSpec((1, tq, LANES), lambda bi, p, qi: (bi, qi, p)),
                  pl.BlockSpec((1, s, 2 * LANES), lambda bi, p, qi: (bi, 0, p)),
                  pl.BlockSpec((1, LANES, s), lambda bi, p, qi: (bi, p, 0)),
                  pl.BlockSpec((LANES, 1), lambda bi, p, qi: (0, 0))],
        out_specs=pl.BlockSpec((1, tq, LANES), lambda bi, p, qi: (bi, qi, p)),
        out_shape=jax.ShapeDtypeStruct((b, s, DIFF_WIDTH), BF16),
        scratch_shapes=[pltpu.VMEM((4 * tq, 2 * LANES), BF16), pltpu.VMEM((tk, 4 * tq), F32),
                        pltpu.VMEM((tk, 4 * tq), F32)],
        compiler_params=_params("parallel", "parallel", "arbitrary"),
        name="diff_attn",
    )(lam, qd, kd, vdt, g_col)


def _mla_attn_kernel(q_ref, k_ref, vt_ref, o_ref, sa_ref, sb_ref, *, tq, tk):
    qi = pl.program_id(2)
    slabs = [slice(hh * MLA_SLAB, (hh + 1) * MLA_SLAB) for hh in range(2)]

    def scores(j, slot):
        start = pl.multiple_of(j * tk, tk)
        for hh, sl in enumerate(slabs):
            slot[hh] = lax.dot_general(k_ref[0, pl.ds(start, tk), sl], q_ref[0, :, sl], _NT,
                                       preferred_element_type=F32)

    def consume(j, slot, carry, masked):
        start = pl.multiple_of(j * tk, tk)
        vtc = vt_ref[0, :, pl.ds(start, tk)]
        out = []
        for hh in range(2):
            s = slot[hh]
            if masked:
                s = _causal_mask(s, start, qi * tq, tq)
            out.extend(_softmax_step(s, vtc, *carry[3 * hh:3 * hh + 3]))
        return tuple(out)

    init = (jnp.full((1, tq), MASK_VALUE, F32), jnp.zeros((1, tq), F32),
            jnp.zeros((LANES, tq), F32)) * 2
    _, l0, acc0, _, l1, acc1 = _causal_sweep(scores, consume, init, (qi * tq) // tk,
                                             sa_ref, sb_ref)
    o = jnp.concatenate([(acc0 * (1.0 / l0))[:MLA_V_DIM], (acc1 * (1.0 / l1))[MLA_V_DIM:]], axis=0)
    o_ref[0] = o.T.astype(o_ref.dtype)


def _mla_attn_call(qm, km, vmt, tq, tk):
    b, s, _ = qm.shape
    pairs = MLA_HEADS // 2
    return pl.pallas_call(
        functools.partial(_mla_attn_kernel, tq=tq, tk=tk),
        grid=(b, pairs, s // tq),
        in_specs=[pl.BlockSpec((1, tq, 2 * MLA_SLAB), lambda bi, p, qi: (bi, qi, p)),
                  pl.BlockSpec((1, s, 2 * MLA_SLAB), lambda bi, p, qi: (bi, 0, p)),
                  pl.BlockSpec((1, LANES, s), lambda bi, p, qi: (bi, p, 0))],
        out_specs=pl.BlockSpec((1, tq, LANES), lambda bi, p, qi: (bi, qi, p)),
        out_shape=jax.ShapeDtypeStruct((b, s, MLA_WIDTH), BF16),
        scratch_shapes=[pltpu.VMEM((2, tk, tq), F32), pltpu.VMEM((2, tk, tq), F32)],
        compiler_params=_params("parallel", "parallel", "arbitrary"),
        name="mla_attn",
    )(qm, km, vmt)


def _mem_kv_kernel(mem_ref, g_ref, w_ref, k_ref, v_ref):
    kv = _dot(_rms(mem_ref[0], g_ref[...]).astype(BF16), w_ref[...])
    k_ref[0] = kv[:, :MEM_WIDTH].astype(BF16)
    v_ref[0] = kv[:, MEM_WIDTH:].astype(BF16)


def _mem_kv_call(mem, g, w_kv):
    b, n_mem, d = mem.shape
    blk = pl.BlockSpec((1, n_mem, MEM_WIDTH), lambda bi: (bi, 0, 0))
    return pl.pallas_call(
        _mem_kv_kernel,
        grid=(b,),
        in_specs=[pl.BlockSpec((1, n_mem, d), lambda bi: (bi, 0, 0)),
                  pl.BlockSpec(g.shape, lambda bi: (0, 0)),
                  pl.BlockSpec(w_kv.shape, lambda bi: (0, 0))],
        out_specs=[blk, blk],
        out_shape=[jax.ShapeDtypeStruct((b, n_mem, MEM_WIDTH), BF16)] * 2,
        compiler_params=_params("parallel"),
        name="mem_kv",
    )(mem, g, w_kv)


def _mix_kernel(x_ref, d_ref, m_ref, gm_ref, wo_ref, gc_ref, wq_ref, km_ref, vm_ref, wmo_ref,
                gf_ref, wr_ref, br_ref, x2_ref, h3_ref, r_ref, rt_ref, cnt_ref, run_ref):
    rows = x_ref.shape[0]

    @pl.when(pl.program_id(0) == 0)
    def _():
        run_ref[...] = jnp.zeros_like(run_ref)

    mla_n = _rms(m_ref[...].astype(F32), gm_ref[...]).astype(BF16)
    x1 = (x_ref[...] + _dot(d_ref[...], wo_ref[0:DIFF_WIDTH, :])
          + _dot(mla_n, wo_ref[DIFF_WIDTH:DIFF_WIDTH + MLA_WIDTH, :]))

    qx = _dot(_rms(x1, gc_ref[...]).astype(BF16), wq_ref[...])
    heads = []
    for h in range(MEM_HEADS):
        sl = slice(h * MEM_HEAD_DIM, (h + 1) * MEM_HEAD_DIM)
        sc = lax.dot_general(qx[:, sl].astype(BF16), km_ref[0, :, sl], _NT,
                             preferred_element_type=F32) * MEM_HEAD_DIM ** -0.5
        p = jnp.exp(sc - jnp.max(sc, axis=-1, keepdims=True))
        o = _dot(p.astype(BF16), vm_ref[0, :, sl]) * (1.0 / jnp.sum(p, axis=-1, keepdims=True))
        heads.append(o.astype(BF16))
    x2 = x1 + _dot(jnp.concatenate(heads, axis=1), wmo_ref[...])
    x2_ref[...] = x2

    h3 = _rms(x2, gf_ref[...])
    h3_ref[...] = h3
    h_hi = h3.astype(BF16)
    h_lo = (h3 - h_hi.astype(F32)).astype(BF16)
    logits = (_dot(h_hi, wr_ref[0]) + _dot(h_lo, wr_ref[0]) + _dot(h_hi, wr_ref[1])
              + br_ref[...])
    lane = lax.broadcasted_iota(jnp.int32, (rows, LANES), 1).astype(F32)
    far = float(4 * LANES)

    def first_argmax(vals):
        top = jnp.max(vals, axis=-1, keepdims=True)
        return top, jnp.min(jnp.where(vals == top, lane, far), axis=-1, keepdims=True)

    g_logits = jnp.where(lane < N_GROUPS, logits, MASK_VALUE)
    g_top, g_sel = first_argmax(g_logits)
    g_gate = 1.0 / jnp.sum(jnp.exp(g_logits - g_top), axis=-1, keepdims=True)
    lo = ROUTER_LANE0 + EXPERTS_PER_GROUP * g_sel
    e_logits = jnp.where(lane >= lo, jnp.where(lane < lo + EXPERTS_PER_GROUP, logits, MASK_VALUE),
                         MASK_VALUE)
    top1, lane1 = first_argmax(e_logits)
    top2, lane2 = first_argmax(jnp.where(lane == lane1, MASK_VALUE, e_logits))
    t = jnp.exp(top2 - top1)
    gate1 = g_gate / (1.0 + t)
    gate2 = gate1 * t

    hot = jnp.where(lane == lane1, 1.0, 0.0) + jnp.where(lane == lane2, 1.0, 0.0)
    tri = (lax.broadcasted_iota(jnp.int32, (rows, rows), 1)
           < lax.broadcasted_iota(jnp.int32, (rows, rows), 0))
    before = _dot(jnp.where(tri, 1.0, 0.0).astype(BF16), hot.astype(BF16)) + run_ref[0:1, :]
    rank1 = jnp.sum(jnp.where(lane == lane1, before, 0.0), axis=-1, keepdims=True)
    rank2 = jnp.sum(jnp.where(lane == lane2, before, 0.0), axis=-1, keepdims=True)
    run_ref[...] = run_ref[...] + jnp.sum(hot, axis=0, keepdims=True)
    cnt_ref[...] = run_ref[...]

    fields = (lane1 - ROUTER_LANE0, lane2 - ROUTER_LANE0, gate1, gate2, rank1, rank2)
    out = jnp.zeros((rows, LANES), F32)
    for i, f in enumerate(fields):
        out = jnp.where(lane == float(i), f, out)
    r_ref[...] = out
    rt_ref[...] = out.T[0:8, :]


def _mix_call(x2d, diff, mla, gm, wo, gc, wq, kmem, vmem, wmo, gf, wr, br, seq):
    t, d = x2d.shape
    rows = min(MIX_ROWS, seq)
    per_seq = seq // rows
    row = lambda i: (i, 0)
    fixed = lambda i: (0, 0)
    full = lambda a: pl.BlockSpec(a.shape, lambda i: (0,) * a.ndim)
    memblk = pl.BlockSpec((1,) + kmem.shape[1:], lambda i: (i // per_seq, 0, 0))
    return pl.pallas_call(
        _mix_kernel,
        grid=(t // rows,),
        in_specs=[pl.BlockSpec((rows, d), row), pl.BlockSpec((rows, DIFF_WIDTH), row),
                  pl.BlockSpec((rows, MLA_WIDTH), row), full(gm), full(wo), full(gc), full(wq),
                  memblk, memblk, full(wmo), full(gf), full(wr), full(br)],
        out_specs=[pl.BlockSpec((rows, d), row), pl.BlockSpec((rows, d), row),
                   pl.BlockSpec((rows, LANES), row), pl.BlockSpec((8, rows), lambda i: (0, i)),
                   pl.BlockSpec((8, LANES), fixed)],
        out_shape=[jax.ShapeDtypeStruct((t, d), F32), jax.ShapeDtypeStruct((t, d), F32),
                   jax.ShapeDtypeStruct((t, LANES), F32), jax.ShapeDtypeStruct((8, t), F32),
                   jax.ShapeDtypeStruct((8, LANES), F32)],
        scratch_shapes=[pltpu.VMEM((8, LANES), F32)],
        compiler_params=_params("arbitrary"),
        name="mix",
    )(x2d, diff, mla, gm, wo, gc, wq, kmem, vmem, wmo, gf, wr, br)


def _dispatch_kernel(d0_ref, d1_ref, h_ref, xs_in_ref, xs_ref, sem):
    del xs_in_ref
    rows = h_ref.shape[0]

    def issue(r, c):
        src = h_ref.at[pl.ds(r, 1)]
        pltpu.make_async_copy(src, xs_ref.at[pl.ds(d0_ref[0, 0, r], 1)], sem).start()
        pltpu.make_async_copy(src, xs_ref.at[pl.ds(d1_ref[0, 0, r], 1)], sem).start()
        return c

    lax.fori_loop(0, rows, issue, 0, unroll=8)
    for _ in range(2):
        pltpu.make_async_copy(h_ref, xs_ref.at[pl.ds(0, rows)], sem).wait()


def _dispatch_call(d0, d1, h3, xs0, rows):
    t, d = h3.shape
    idx = pl.BlockSpec((1, 1, rows), lambda i: (i, 0, 0), memory_space=pltpu.SMEM)
    return pl.pallas_call(
        _dispatch_kernel,
        grid=(t // rows,),
        in_specs=[idx, idx, pl.BlockSpec((rows, d), lambda i: (i, 0)),
                  pl.BlockSpec(memory_space=pl.ANY)],
        out_specs=pl.BlockSpec(memory_space=pl.ANY),
        out_shape=jax.ShapeDtypeStruct(xs0.shape, xs0.dtype),
        scratch_shapes=[pltpu.SemaphoreType.DMA(())],
        input_output_aliases={3: 0},
        compiler_params=_params("arbitrary"),
        name="dispatch",
    )(d0, d1, h3, xs0)


def _expert_kernel(be_ref, x_ref, wg_ref, wu_ref, wd_ref, y_ref):
    del be_ref
    x = x_ref[...].astype(BF16)
    gate = _dot(x, wg_ref[0])
    up = _dot(x, wu_ref[0])
    hidden = gate * (1.0 / (1.0 + jnp.exp(-gate))) * up
    y_ref[...] = _dot(hidden.astype(BF16), wd_ref[0])


def _expert_call(block_expert, xs, wg, wu, wd, rows):
    n_slots, d = xs.shape
    ff = wg.shape[-1]
    grid_spec = pltpu.PrefetchScalarGridSpec(
        num_scalar_prefetch=1,
        grid=(n_slots // rows,),
        in_specs=[pl.BlockSpec((rows, d), lambda i, be: (i, 0)),
                  pl.BlockSpec((1, d, ff), lambda i, be: (be[i], 0, 0)),
                  pl.BlockSpec((1, d, ff), lambda i, be: (be[i], 0, 0)),
                  pl.BlockSpec((1, ff, d), lambda i, be: (be[i], 0, 0))],
        out_specs=pl.BlockSpec((rows, d), lambda i, be: (i, 0)),
    )
    return pl.pallas_call(
        _expert_kernel,
        grid_spec=grid_spec,
        out_shape=jax.ShapeDtypeStruct((n_slots, d), F32),
        compiler_params=_params("arbitrary"),
        name="experts",
    )(block_expert, xs, wg, wu, wd)


def _combine_kernel(d0_ref, d1_ref, x_ref, r_ref, g_ref, y_ref, o_ref, buf0, buf1, sem):
    rows = x_ref.shape[0]

    def issue(r, c):
        pltpu.make_async_copy(y_ref.at[pl.ds(d0_ref[0, 0, r], 1)], buf0.at[pl.ds(r, 1)], sem).start()
        pltpu.make_async_copy(y_ref.at[pl.ds(d1_ref[0, 0, r], 1)], buf1.at[pl.ds(r, 1)], sem).start()
        return c

    lax.fori_loop(0, rows, issue, 0, unroll=8)
    for buf in (buf0, buf1):
        pltpu.make_async_copy(y_ref.at[pl.ds(0, rows)], buf, sem).wait()
    gates = r_ref[...]
    x3 = x_ref[...] + gates[:, 2:3] * buf0[...] + gates[:, 3:4] * buf1[...]
    o_ref[...] = _rms(x3, g_ref[...])


def _combine_call(d0, d1, x2, router, g, y, rows):
    t, d = x2.shape
    idx = pl.BlockSpec((1, 1, rows), lambda i: (i, 0, 0), memory_space=pltpu.SMEM)
    row = lambda i: (i, 0)
    return pl.pallas_call(
        _combine_kernel,
        grid=(t // rows,),
        in_specs=[idx, idx, pl.BlockSpec((rows, d), row), pl.BlockSpec((rows, LANES), row),
                  pl.BlockSpec(g.shape, lambda i: (0, 0)), pl.BlockSpec(memory_space=pl.ANY)],
        out_specs=pl.BlockSpec((rows, d), row),
        out_shape=jax.ShapeDtypeStruct((t, d), F32),
        scratch_shapes=[pltpu.VMEM((rows, d), F32), pltpu.VMEM((rows, d), F32),
                        pltpu.SemaphoreType.DMA(())],
        compiler_params=_params("arbitrary"),
        name="combine",
    )(d0, d1, x2, router, g, y)


def _slab(cols, lo):
    return jnp.pad(cols, ((0, 0), (lo, MLA_SLAB - lo - cols.shape[1])))


def _layer_weights(w_in, w_uq, w_ukv):
    half = MLA_ROPE_DIM // 2
    o = 3 * DIFF_WIDTH + MLA_Q_RANK + MLA_KV_RANK
    kpe1, kpe2 = w_in[:, o:o + half], w_in[:, o + half:o + 2 * half]
    w_all = jnp.concatenate(
        [w_in[:, :o],
         _slab(jnp.concatenate([kpe1, kpe2], axis=1), MLA_NOPE_DIM),
         _slab(jnp.concatenate([kpe2, kpe1], axis=1), MLA_NOPE_DIM)], axis=1).astype(BF16)
    q_main, q_swap, k_nope, v_cols = [], [], [], []
    for h in range(MLA_HEADS):
        c = w_uq[:, h * MLA_QK_DIM:(h + 1) * MLA_QK_DIM]
        pe1 = c[:, MLA_NOPE_DIM:MLA_NOPE_DIM + half]
        pe2 = c[:, MLA_NOPE_DIM + half:]
        q_main.append(_slab(c, 0))
        q_swap.append(_slab(jnp.concatenate([pe2, pe1], axis=1), MLA_NOPE_DIM))
        kv = w_ukv[:, h * (MLA_NOPE_DIM + MLA_V_DIM):(h + 1) * (MLA_NOPE_DIM + MLA_V_DIM)]
        k_nope.append(_slab(kv[:, :MLA_NOPE_DIM], 0))
        v_cols.append(kv[:, MLA_NOPE_DIM:])
    wq2 = jnp.concatenate(q_main + q_swap, axis=1).astype(BF16)
    wkv2 = jnp.concatenate(k_nope + v_cols, axis=1).astype(BF16)
    return w_all, wq2, wkv2


def _rope_tables(positions):
    inv_freq = ROPE_THETA ** (-jnp.arange(0, MLA_ROPE_DIM, 2, dtype=F32) / MLA_ROPE_DIM)
    ang = positions.astype(F32).reshape(-1, 1) * inv_freq
    cos, sin = jnp.cos(ang), jnp.sin(ang)
    t = ang.shape[0]
    pad = jnp.zeros((t, MLA_SLAB - MLA_QK_DIM), F32)
    cos_t = jnp.concatenate([jnp.ones((t, MLA_NOPE_DIM), F32), cos, cos, pad], axis=1)
    sin_t = jnp.concatenate([jnp.zeros((t, MLA_NOPE_DIM), F32), -sin, sin, pad], axis=1)
    return cos_t, sin_t


def kernel(x, mem, positions, norm_mix_g, w_in, diff_lambda_q1, diff_lambda_k1, diff_lambda_q2, diff_lambda_k2, diff_out_g, mla_q_norm_g, w_mla_uq, mla_kv_norm_g, w_mla_ukv, mla_out_g, w_out, norm_cross_g, norm_mem_g, w_mem_q, w_mem_kv, w_mem_o, norm_ffn_g, w_group_router, b_group_router, w_expert_router, b_expert_router, w_expert_gate, w_expert_up, w_expert_down, norm_final_g):
    b, s, d = x.shape
    t = b * s
    depth = w_in.shape[0]
    row2 = lambda v: v.reshape(1, -1).astype(F32)
    cos_t, sin_t = _rope_tables(positions)
    pos_f = positions.astype(F32)
    slopes = 2.0 ** (-8.0 * jnp.arange(1, DIFF_HEADS + 1, dtype=F32) / DIFF_HEADS)
    pos_rel = (pos_f - pos_f[:, :1]).reshape(t, 1)
    slope_rows = jnp.pad(jnp.repeat((LOG2E * slopes).reshape(DIFF_HEADS // 2, 2), 3, axis=1),
                         ((0, 0), (0, LANES - 6)))

    diff_tq, mla_tq = min(DIFF_TQ, s), min(MLA_TQ, s)
    attn_tk = min(ATTN_TK, s)
    moe_rows = min(MOE_ROWS, t)
    blk = EXPERT_ROWS
    n_assign = 2 * t
    n_blocks = -(-(n_assign + N_EXPERTS * (blk - 1)) // blk)
    n_slots = n_blocks * blk

    assert depth == 1, "single-layer problem: the combine kernel applies the final RMSNorm"
    l = 0
    x2d = x.reshape(t, d)
    w_all, wq2, wkv2 = _layer_weights(w_in[l], w_mla_uq[l], w_mla_ukv[l])
    qd, kd, vdt, qm, km, vmt = _proj_call(
        x2d, row2(norm_mix_g[l]), w_all, row2(mla_q_norm_g[l]), wq2,
        row2(mla_kv_norm_g[l]), wkv2, cos_t, sin_t, pos_rel, slope_rows, s)

    lambda_init = 0.8 - 0.6 * math.exp(-0.3 * l)
    lam = (jnp.exp(jnp.sum(diff_lambda_q1[l] * diff_lambda_k1[l]))
           - jnp.exp(jnp.sum(diff_lambda_q2[l] * diff_lambda_k2[l])) + lambda_init)
    gain_col = jnp.tile(diff_out_g[l] * (1.0 - lambda_init), 2).reshape(-1, 1).astype(F32)
    shp = lambda a: a.reshape(b, s, a.shape[-1])
    diff_out = _diff_attn_call(lam.reshape(1).astype(F32), shp(qd), shp(kd), vdt, gain_col,
                               diff_tq, attn_tk)
    mla_out = _mla_attn_call(shp(qm), shp(km), vmt, mla_tq, attn_tk)

    kmem, vmem = _mem_kv_call(mem, row2(norm_mem_g[l]), w_mem_kv[l].astype(BF16))
    w_router = jnp.pad(
        jnp.concatenate([w_group_router[l], w_expert_router[l]], axis=1),
        ((0, 0), (0, LANES - N_GROUPS - N_EXPERTS)))
    w_router_hi = w_router.astype(BF16)
    w_router = jnp.stack([w_router_hi, (w_router - w_router_hi.astype(F32)).astype(BF16)])
    b_router = jnp.pad(jnp.concatenate([b_group_router[l], b_expert_router[l]]),
                       (0, LANES - N_GROUPS - N_EXPERTS))
    x2, h3, router, router_t, counts = _mix_call(
        x2d, diff_out.reshape(t, DIFF_WIDTH), mla_out.reshape(t, MLA_WIDTH),
        row2(mla_out_g[l]), w_out[l].astype(BF16), row2(norm_cross_g[l]),
        w_mem_q[l].astype(BF16), kmem, vmem, w_mem_o[l].astype(BF16),
        row2(norm_ffn_g[l]), w_router, row2(b_router), s)

    cnt = counts[0, ROUTER_LANE0:ROUTER_LANE0 + N_EXPERTS].astype(jnp.int32)
    padded = ((cnt + blk - 1) // blk) * blk
    seg_end = jnp.cumsum(padded)
    seg_start = seg_end - padded
    ridx = router_t.astype(jnp.int32)
    expert_ids = jnp.arange(N_EXPERTS, dtype=jnp.int32)[:, None]

    def slot_of(choice):
        start = jnp.sum(jnp.where(ridx[choice][None, :] == expert_ids, seg_start[:, None], 0),
                        axis=0)
        return (start + ridx[4 + choice]).reshape(t // moe_rows, 1, moe_rows)

    dest0, dest1 = slot_of(0), slot_of(1)
    block_start = jnp.arange(n_blocks, dtype=jnp.int32) * blk
    block_expert = jnp.minimum(
        jnp.sum((seg_end[None, :] <= block_start[:, None]).astype(jnp.int32), axis=1),
        N_EXPERTS - 1)

    xs = _dispatch_call(dest0, dest1, h3, jnp.zeros((n_slots, d), F32), moe_rows)
    y = _expert_call(block_expert, xs, w_expert_gate[l].astype(BF16),
                     w_expert_up[l].astype(BF16), w_expert_down[l].astype(BF16), blk)
    out = _combine_call(dest0, dest1, x2, router, row2(norm_final_g), y, moe_rows)
    return out.reshape(b, s, d)
```

```python
import functools
import math

import jax
import jax.numpy as jnp
from jax import lax
from jax.experimental import pallas as pl
from jax.experimental.pallas import tpu as pltpu

F32 = jnp.float32
BF16 = jnp.bfloat16

RMS_EPS = 1e-6
ROPE_THETA = 10000.0
MASK_VALUE = -1e30

DIFF_HEADS = 8
DIFF_HALF_DIM = 32
DIFF_V_DIM = 64
DIFF_WIDTH = DIFF_HEADS * DIFF_V_DIM

MLA_HEADS = 8
MLA_Q_RANK = 256
MLA_KV_RANK = 128
MLA_NOPE_DIM = 64
MLA_ROPE_DIM = 32
MLA_QK_DIM = MLA_NOPE_DIM + MLA_ROPE_DIM
MLA_V_DIM = 64
MLA_WIDTH = MLA_HEADS * MLA_V_DIM

MEM_HEADS = 4
MEM_HEAD_DIM = 128
MEM_WIDTH = MEM_HEADS * MEM_HEAD_DIM

N_GROUPS = 4
EXPERTS_PER_GROUP = 8
N_EXPERTS = N_GROUPS * EXPERTS_PER_GROUP
EXPERT_FF = 256

LANES = 128
MLA_SLAB = LANES
ROUTER_LANE0 = N_GROUPS

PROJ_ROWS = 512
MIX_ROWS = 512
DISPATCH_ROWS = 512
COMBINE_ROWS = 256
EXPERT_ROWS = 256
ATTN_TK = 512
DIFF_TQ = 512
MLA_TQ = 512
MLA_HEADS_PER_STEP = 4
LOG2E = math.log2(math.e)
VMEM_LIMIT = 48 * 1024 * 1024

_NT = (((1,), (1,)), ((), ()))


def _rms(x, gain):
    return x * lax.rsqrt(jnp.mean(x * x, axis=-1, keepdims=True) + RMS_EPS) * gain


def _dot(a, b):
    return jnp.dot(a, b, preferred_element_type=F32)


def _params(*sem):
    return pltpu.CompilerParams(dimension_semantics=sem, vmem_limit_bytes=VMEM_LIMIT)


def _proj_kernel(x_ref, g_ref, w_ref, gq_ref, wq_ref, gkv_ref, wkv_ref, cos_ref, sin_ref,
                 pos_ref, slope_ref, qd_ref, kd_ref, vd_ref, qm_ref, km_ref, vm_ref):
    xn = _rms(x_ref[...], g_ref[...]).astype(BF16)
    r = _dot(xn, w_ref[...])
    w = DIFF_WIDTH
    qd_ref[...] = (r[:, 0:w] * (DIFF_HALF_DIM ** -0.5 * LOG2E)).astype(BF16)
    lane = lax.broadcasted_iota(jnp.int32, (x_ref.shape[0], LANES), 1)
    for p in range(DIFF_HEADS // 2):
        bias = pos_ref[...] * slope_ref[p:p + 1, :]
        hi = bias.astype(BF16)
        mid = (bias - hi.astype(F32)).astype(BF16)
        low = (bias - hi.astype(F32) - mid.astype(F32)).astype(BF16)
        term = jnp.where((lane == 0) | (lane == 3), hi, jnp.where((lane == 1) | (lane == 4), mid, low))
        kd_ref[:, 2 * p * LANES:(2 * p + 1) * LANES] = r[:, w + p * LANES:w + (p + 1) * LANES].astype(BF16)
        kd_ref[:, (2 * p + 1) * LANES:(2 * p + 2) * LANES] = term
    vd_ref[0] = r[:, 2 * w:3 * w].T.astype(BF16)
    o = 3 * w
    cq = r[:, o:o + MLA_Q_RANK]
    o += MLA_Q_RANK
    ckv = r[:, o:o + MLA_KV_RANK]
    o += MLA_KV_RANK
    cos = cos_ref[...]
    sin = sin_ref[...]
    kpe = r[:, o:o + MLA_SLAB] * cos + r[:, o + MLA_SLAB:o + 2 * MLA_SLAB] * sin
    q2 = _dot(_rms(cq, gq_ref[...]).astype(BF16), wq_ref[...])
    kv2 = _dot(_rms(ckv, gkv_ref[...]).astype(BF16), wkv_ref[...])
    sw = MLA_HEADS * MLA_SLAB
    for h in range(MLA_HEADS):
        lo = h * MLA_SLAB
        qh = q2[:, lo:lo + MLA_SLAB] * cos + q2[:, sw + lo:sw + lo + MLA_SLAB] * sin
        qm_ref[:, lo:lo + MLA_SLAB] = (qh * (MLA_QK_DIM ** -0.5 * LOG2E)).astype(BF16)
        km_ref[:, lo:lo + MLA_SLAB] = (kv2[:, lo:lo + MLA_SLAB] + kpe).astype(BF16)
    vm_ref[0] = kv2[:, sw:sw + MLA_WIDTH].T.astype(BF16)


def _proj_call(x2d, g, w_all, gq, wq2, gkv, wkv2, cos_t, sin_t, pos_rel, slope_rows, seq):
    t, d = x2d.shape
    rows = min(PROJ_ROWS, seq)
    per_seq = seq // rows
    row = lambda i: (i, 0)
    fixed = lambda i: (0, 0)
    full = lambda a: pl.BlockSpec(a.shape, fixed)
    row_out = lambda n: (pl.BlockSpec((rows, n), row), jax.ShapeDtypeStruct((t, n), BF16))
    vt_out = lambda n: (pl.BlockSpec((1, n, rows), lambda i: (i // per_seq, 0, i % per_seq)),
                        jax.ShapeDtypeStruct((t // seq, n, seq), BF16))
    outs = [row_out(DIFF_WIDTH), row_out(DIFF_HEADS * LANES), vt_out(DIFF_WIDTH),
            row_out(MLA_HEADS * MLA_SLAB), row_out(MLA_HEADS * MLA_SLAB), vt_out(MLA_WIDTH)]
    return pl.pallas_call(
        _proj_kernel,
        grid=(t // rows,),
        in_specs=[pl.BlockSpec((rows, d), row), full(g), full(w_all), full(gq), full(wq2),
                  full(gkv), full(wkv2), pl.BlockSpec((rows, MLA_SLAB), row),
                  pl.BlockSpec((rows, MLA_SLAB), row), pl.BlockSpec((rows, 1), row),
                  full(slope_rows)],
        out_specs=[spec for spec, _ in outs],
        out_shape=[shape for _, shape in outs],
        compiler_params=_params("parallel"),
        name="proj",
    )(x2d, g, w_all, gq, wq2, gkv, wkv2, cos_t, sin_t, pos_rel, slope_rows)


def _softmax_step(s, vt_chunk, parts, m, l, *accs):
    m_new = jnp.maximum(m, jnp.max(s, axis=0, keepdims=True))
    alpha = jnp.exp2(m - m_new)
    p = jnp.exp2(s - m_new)
    l = alpha * l + jnp.sum(p, axis=0, keepdims=True)
    pb = p.astype(BF16)
    accs = tuple(alpha[:, cols] * acc + _dot(vt_chunk[rows], pb[:, cols])
                 for (rows, cols), acc in zip(parts, accs))
    return (m_new, l) + accs


def _causal_mask(s, key0, query0, tq):
    key = key0 + lax.broadcasted_iota(jnp.int32, s.shape, 0)
    query = query0 + (lax.broadcasted_iota(jnp.int32, s.shape, 1) & (tq - 1))
    return jnp.where(key <= query, s, MASK_VALUE)


def _causal_sweep(scores, consume, carry, n_full, slot_a, slot_b):
    scores(0, slot_a)

    def pair(i, c):
        j = 2 * i
        scores(j + 1, slot_b)
        c = consume(j, slot_a, c, False)
        scores(j + 2, slot_a)
        return consume(j + 1, slot_b, c, False)

    carry = lax.fori_loop(0, lax.shift_right_logical(n_full, 1), pair, carry)

    def odd_tail(c):
        scores(n_full, slot_b)
        c = consume(n_full - 1, slot_a, c, False)
        return consume(n_full, slot_b, c, True)

    def even_tail(c):
        return consume(n_full, slot_a, c, True)

    return lax.cond((n_full & 1) == 1, odd_tail, even_tail, carry)


def _diff_attn_kernel(lam_ref, q_ref, k_ref, vt_ref, g_ref, o_ref, qs_ref, sa_ref, sb_ref,
                      *, tq, tk):
    qi = pl.program_id(2)
    q = q_ref[0]
    lane = lax.broadcasted_iota(jnp.int32, (tq, LANES), 1)
    for v in range(4):
        keep = (lane >= v * DIFF_HALF_DIM) & (lane < (v + 1) * DIFF_HALF_DIM)
        ones = (lane >= 3 * (v // 2)) & (lane < 3 * (v // 2) + 3)
        qs_ref[v * tq:(v + 1) * tq, 0:LANES] = jnp.where(keep, q, jnp.zeros_like(q))
        qs_ref[v * tq:(v + 1) * tq, LANES:2 * LANES] = jnp.where(ones, 1.0, 0.0).astype(BF16)

    def scores(j, slot):
        start = pl.multiple_of(j * tk, tk)
        slot[...] = lax.dot_general(k_ref[0, pl.ds(start, tk), :], qs_ref[...], _NT,
                                    preferred_element_type=F32)

    def consume(j, slot, carry, masked):
        start = pl.multiple_of(j * tk, tk)
        s = slot[...]
        if masked:
            s = _causal_mask(s, start, qi * tq, tq)
        return _softmax_step(s, vt_ref[0, :, pl.ds(start, tk)], parts, *carry)

    parts = [(slice(h * DIFF_V_DIM, (h + 1) * DIFF_V_DIM), slice(2 * h * tq, (2 * h + 2) * tq))
             for h in range(2)]
    init = (jnp.full((1, 4 * tq), MASK_VALUE, F32), jnp.zeros((1, 4 * tq), F32),
            jnp.zeros((DIFF_V_DIM, 2 * tq), F32), jnp.zeros((DIFF_V_DIM, 2 * tq), F32))
    _, l, *accs = _causal_sweep(scores, consume, init, (qi * tq) // tk, sa_ref, sb_ref)

    lam = lam_ref[0]
    halves = []
    for (_, cols), acc in zip(parts, accs):
        on = acc * (1.0 / l[:, cols])
        o = on[:, :tq] - lam * on[:, tq:]
        halves.append(o * lax.rsqrt(jnp.mean(o * o, axis=0, keepdims=True) + RMS_EPS))
    o_ref[0] = (jnp.concatenate(halves, axis=0) * g_ref[...]).T.astype(o_ref.dtype)


def _diff_attn_call(lam, qd, kd, vdt, g_col, tq, tk):
    b, s, _ = qd.shape
    pairs = DIFF_HEADS // 2
    return pl.pallas_call(
        functools.partial(_diff_attn_kernel, tq=tq, tk=tk),
        grid=(b, pairs, s // tq),
        in_specs=[pl.BlockSpec(memory_space=pltpu.SMEM),
                  pl.BlockSpec((1, tq, LANES), lambda bi, p, qi: (bi, qi, p)),
                  pl.BlockSpec((1, s, 2 * LANES), lambda bi, p, qi: (bi, 0, p)),
                  pl.BlockSpec((1, LANES, s), lambda bi, p, qi: (bi, p, 0)),
                  pl.BlockSpec((LANES, 1), lambda bi, p, qi: (0, 0))],
        out_specs=pl.BlockSpec((1, tq, LANES), lambda bi, p, qi: (bi, qi, p)),
        out_shape=jax.ShapeDtypeStruct((b, s, DIFF_WIDTH), BF16),
        scratch_shapes=[pltpu.VMEM((4 * tq, 2 * LANES), BF16), pltpu.VMEM((tk, 4 * tq), F32),
                        pltpu.VMEM((tk, 4 * tq), F32)],
        compiler_params=_params("parallel", "parallel", "arbitrary"),
        name="diff_attn",
    )(lam, qd, kd, vdt, g_col)


def _mla_attn_kernel(q_ref, k_ref, vt_ref, o_ref, sa_ref, sb_ref, *, tq, tk, heads):
    qi = pl.program_id(2)
    slabs = [slice(hh * MLA_SLAB, (hh + 1) * MLA_SLAB) for hh in range(heads)]

    def scores(j, slot):
        start = pl.multiple_of(j * tk, tk)
        for hh, sl in enumerate(slabs):
            slot[hh] = lax.dot_general(k_ref[0, pl.ds(start, tk), sl], q_ref[0, :, sl], _NT,
                                       preferred_element_type=F32)

    def consume(j, slot, carry, masked):
        start = pl.multiple_of(j * tk, tk)
        vtc = vt_ref[0, :, pl.ds(start, tk)]
        out = []
        for hh in range(heads):
            s = slot[hh]
            if masked:
                s = _causal_mask(s, start, qi * tq, tq)
            rows = slice(hh * MLA_V_DIM, (hh + 1) * MLA_V_DIM)
            out.extend(_softmax_step(s, vtc, [(rows, slice(None))], *carry[3 * hh:3 * hh + 3]))
        return tuple(out)

    init = (jnp.full((1, tq), MASK_VALUE, F32), jnp.zeros((1, tq), F32),
            jnp.zeros((MLA_V_DIM, tq), F32)) * heads
    carry = _causal_sweep(scores, consume, init, (qi * tq) // tk, sa_ref, sb_ref)
    o = jnp.concatenate([carry[3 * hh + 2] * (1.0 / carry[3 * hh + 1]) for hh in range(heads)],
                        axis=0)
    o_ref[0] = o.T.astype(o_ref.dtype)


def _mla_attn_call(qm, km, vmt, tq, tk, heads):
    b, s, _ = qm.shape
    groups = MLA_HEADS // heads
    return pl.pallas_call(
        functools.partial(_mla_attn_kernel, tq=tq, tk=tk, heads=heads),
        grid=(b, groups, s // tq),
        in_specs=[pl.BlockSpec((1, tq, heads * MLA_SLAB), lambda bi, p, qi: (bi, qi, p)),
                  pl.BlockSpec((1, s, heads * MLA_SLAB), lambda bi, p, qi: (bi, 0, p)),
                  pl.BlockSpec((1, heads * MLA_V_DIM, s), lambda bi, p, qi: (bi, p, 0))],
        out_specs=pl.BlockSpec((1, tq, heads * MLA_V_DIM), lambda bi, p, qi: (bi, qi, p)),
        out_shape=jax.ShapeDtypeStruct((b, s, MLA_WIDTH), BF16),
        scratch_shapes=[pltpu.VMEM((heads, tk, tq), F32), pltpu.VMEM((heads, tk, tq), F32)],
        compiler_params=_params("parallel", "parallel", "arbitrary"),
        name="mla_attn",
    )(qm, km, vmt)


def _mem_kv_kernel(mem_ref, g_ref, w_ref, k_ref, v_ref):
    kv = _dot(_rms(mem_ref[0], g_ref[...]).astype(BF16), w_ref[...])
    k_ref[0] = kv[:, :MEM_WIDTH].astype(BF16)
    v_ref[0] = kv[:, MEM_WIDTH:].astype(BF16)


def _mem_kv_call(mem, g, w_kv):
    b, n_mem, d = mem.shape
    blk = pl.BlockSpec((1, n_mem, MEM_WIDTH), lambda bi: (bi, 0, 0))
    return pl.pallas_call(
        _mem_kv_kernel,
        grid=(b,),
        in_specs=[pl.BlockSpec((1, n_mem, d), lambda bi: (bi, 0, 0)),
                  pl.BlockSpec(g.shape, lambda bi: (0, 0)),
                  pl.BlockSpec(w_kv.shape, lambda bi: (0, 0))],
        out_specs=[blk, blk],
        out_shape=[jax.ShapeDtypeStruct((b, n_mem, MEM_WIDTH), BF16)] * 2,
        compiler_params=_params("parallel"),
        name="mem_kv",
    )(mem, g, w_kv)


def _mix_kernel(x_ref, d_ref, m_ref, gm_ref, wo_ref, gc_ref, wq_ref, km_ref, vm_ref, wmo_ref,
                gf_ref, wr_ref, br_ref, x2_ref, h3_ref, r_ref, rt_ref, cnt_ref, run_ref):
    rows = x_ref.shape[0]

    @pl.when(pl.program_id(0) == 0)
    def _():
        run_ref[...] = jnp.zeros_like(run_ref)

    mla_n = _rms(m_ref[...].astype(F32), gm_ref[...]).astype(BF16)
    x1 = (x_ref[...] + _dot(d_ref[...], wo_ref[0:DIFF_WIDTH, :])
          + _dot(mla_n, wo_ref[DIFF_WIDTH:DIFF_WIDTH + MLA_WIDTH, :]))

    qx = _dot(_rms(x1, gc_ref[...]).astype(BF16), wq_ref[...])
    heads = []
    for h in range(MEM_HEADS):
        sl = slice(h * MEM_HEAD_DIM, (h + 1) * MEM_HEAD_DIM)
        sc = lax.dot_general(qx[:, sl].astype(BF16), km_ref[0, :, sl], _NT,
                             preferred_element_type=F32) * MEM_HEAD_DIM ** -0.5
        p = jnp.exp(sc - jnp.max(sc, axis=-1, keepdims=True))
        o = _dot(p.astype(BF16), vm_ref[0, :, sl]) * (1.0 / jnp.sum(p, axis=-1, keepdims=True))
        heads.append(o.astype(BF16))
    x2 = x1 + _dot(jnp.concatenate(heads, axis=1), wmo_ref[...])
    x2_ref[...] = x2

    h3 = _rms(x2, gf_ref[...])
    h3_ref[...] = h3
    h_hi = h3.astype(BF16)
    h_lo = (h3 - h_hi.astype(F32)).astype(BF16)
    logits = (_dot(h_hi, wr_ref[0]) + _dot(h_lo, wr_ref[0]) + _dot(h_hi, wr_ref[1])
              + br_ref[...])
    lane = lax.broadcasted_iota(jnp.int32, (rows, LANES), 1).astype(F32)
    far = float(4 * LANES)

    def first_argmax(vals):
        top = jnp.max(vals, axis=-1, keepdims=True)
        return top, jnp.min(jnp.where(vals == top, lane, far), axis=-1, keepdims=True)

    g_logits = jnp.where(lane < N_GROUPS, logits, MASK_VALUE)
    g_top, g_sel = first_argmax(g_logits)
    g_gate = 1.0 / jnp.sum(jnp.exp(g_logits - g_top), axis=-1, keepdims=True)
    lo = ROUTER_LANE0 + EXPERTS_PER_GROUP * g_sel
    e_logits = jnp.where(lane >= lo, jnp.where(lane < lo + EXPERTS_PER_GROUP, logits, MASK_VALUE),
                         MASK_VALUE)
    top1, lane1 = first_argmax(e_logits)
    top2, lane2 = first_argmax(jnp.where(lane == lane1, MASK_VALUE, e_logits))
    t = jnp.exp(top2 - top1)
    gate1 = g_gate / (1.0 + t)
    gate2 = gate1 * t

    hot = jnp.where(lane == lane1, 1.0, 0.0) + jnp.where(lane == lane2, 1.0, 0.0)
    tri = (lax.broadcasted_iota(jnp.int32, (rows, rows), 1)
           < lax.broadcasted_iota(jnp.int32, (rows, rows), 0))
    before = _dot(jnp.where(tri, 1.0, 0.0).astype(BF16), hot.astype(BF16)) + run_ref[0:1, :]
    rank1 = jnp.sum(jnp.where(lane == lane1, before, 0.0), axis=-1, keepdims=True)
    rank2 = jnp.sum(jnp.where(lane == lane2, before, 0.0), axis=-1, keepdims=True)
    run_ref[...] = run_ref[...] + jnp.sum(hot, axis=0, keepdims=True)
    cnt_ref[...] = run_ref[...]

    fields = (lane1 - ROUTER_LANE0, lane2 - ROUTER_LANE0, gate1, gate2, rank1, rank2)
    out = jnp.zeros((rows, LANES), F32)
    for i, f in enumerate(fields):
        out = jnp.where(lane == float(i), f, out)
    r_ref[...] = out
    rt_ref[...] = out.T[0:8, :]


def _mix_call(x2d, diff, mla, gm, wo, gc, wq, kmem, vmem, wmo, gf, wr, br, seq):
    t, d = x2d.shape
    rows = min(MIX_ROWS, seq)
    per_seq = seq // rows
    row = lambda i: (i, 0)
    fixed = lambda i: (0, 0)
    full = lambda a: pl.BlockSpec(a.shape, lambda i: (0,) * a.ndim)
    memblk = pl.BlockSpec((1,) + kmem.shape[1:], lambda i: (i // per_seq, 0, 0))
    return pl.pallas_call(
        _mix_kernel,
        grid=(t // rows,),
        in_specs=[pl.BlockSpec((rows, d), row), pl.BlockSpec((rows, DIFF_WIDTH), row),
                  pl.BlockSpec((rows, MLA_WIDTH), row), full(gm), full(wo), full(gc), full(wq),
                  memblk, memblk, full(wmo), full(gf), full(wr), full(br)],
        out_specs=[pl.BlockSpec((rows, d), row), pl.BlockSpec((rows, d), row),
                   pl.BlockSpec((rows, LANES), row), pl.BlockSpec((8, rows), lambda i: (0, i)),
                   pl.BlockSpec((8, LANES), fixed)],
        out_shape=[jax.ShapeDtypeStruct((t, d), F32), jax.ShapeDtypeStruct((t, d), F32),
                   jax.ShapeDtypeStruct((t, LANES), F32), jax.ShapeDtypeStruct((8, t), F32),
                   jax.ShapeDtypeStruct((8, LANES), F32)],
        scratch_shapes=[pltpu.VMEM((8, LANES), F32)],
        compiler_params=_params("arbitrary"),
        name="mix",
    )(x2d, diff, mla, gm, wo, gc, wq, kmem, vmem, wmo, gf, wr, br)


def _dispatch_kernel(d0_ref, d1_ref, h_ref, xs_in_ref, xs_ref, sem):
    del xs_in_ref
    rows = h_ref.shape[0]

    def issue(r, c):
        src = h_ref.at[pl.ds(r, 1)]
        pltpu.make_async_copy(src, xs_ref.at[pl.ds(d0_ref[0, 0, r], 1)], sem).start()
        pltpu.make_async_copy(src, xs_ref.at[pl.ds(d1_ref[0, 0, r], 1)], sem).start()
        return c

    lax.fori_loop(0, rows, issue, 0, unroll=8)
    for _ in range(2):
        pltpu.make_async_copy(h_ref, xs_ref.at[pl.ds(0, rows)], sem).wait()


def _dispatch_call(d0, d1, h3, xs0, rows):
    t, d = h3.shape
    d0, d1 = d0.reshape(t // rows, 1, rows), d1.reshape(t // rows, 1, rows)
    idx = pl.BlockSpec((1, 1, rows), lambda i: (i, 0, 0), memory_space=pltpu.SMEM)
    return pl.pallas_call(
        _dispatch_kernel,
        grid=(t // rows,),
        in_specs=[idx, idx, pl.BlockSpec((rows, d), lambda i: (i, 0)),
                  pl.BlockSpec(memory_space=pl.ANY)],
        out_specs=pl.BlockSpec(memory_space=pl.ANY),
        out_shape=jax.ShapeDtypeStruct(xs0.shape, xs0.dtype),
        scratch_shapes=[pltpu.SemaphoreType.DMA(())],
        input_output_aliases={3: 0},
        compiler_params=_params("arbitrary"),
        name="dispatch",
    )(d0, d1, h3, xs0)


def _expert_kernel(be_ref, x_ref, wg_ref, wu_ref, wd_ref, y_ref, wg_bf, wu_bf, wd_bf):
    i = pl.program_id(0)

    @pl.when((i == 0) | (be_ref[i] != be_ref[jnp.maximum(i - 1, 0)]))
    def _():
        wg_bf[...] = wg_ref[0].astype(BF16)
        wu_bf[...] = wu_ref[0].astype(BF16)
        wd_bf[...] = wd_ref[0].astype(BF16)

    x = x_ref[...].astype(BF16)
    gate = _dot(x, wg_bf[...])
    up = _dot(x, wu_bf[...])
    hidden = gate * (1.0 / (1.0 + jnp.exp(-gate))) * up
    y_ref[...] = _dot(hidden.astype(BF16), wd_bf[...])


def _expert_call(block_expert, xs, wg, wu, wd, rows):
    n_slots, d = xs.shape
    ff = wg.shape[-1]
    grid_spec = pltpu.PrefetchScalarGridSpec(
        num_scalar_prefetch=1,
        grid=(n_slots // rows,),
        in_specs=[pl.BlockSpec((rows, d), lambda i, be: (i, 0)),
                  pl.BlockSpec((1, d, ff), lambda i, be: (be[i], 0, 0)),
                  pl.BlockSpec((1, d, ff), lambda i, be: (be[i], 0, 0)),
                  pl.BlockSpec((1, ff, d), lambda i, be: (be[i], 0, 0))],
        out_specs=pl.BlockSpec((rows, d), lambda i, be: (i, 0)),
        scratch_shapes=[pltpu.VMEM((d, ff), BF16), pltpu.VMEM((d, ff), BF16),
                        pltpu.VMEM((ff, d), BF16)],
    )
    return pl.pallas_call(
        _expert_kernel,
        grid_spec=grid_spec,
        out_shape=jax.ShapeDtypeStruct((n_slots, d), F32),
        compiler_params=_params("arbitrary"),
        name="experts",
    )(block_expert, xs, wg, wu, wd)


def _combine_kernel(d0_ref, d1_ref, n0_ref, n1_ref, x_ref, r_ref, g_ref, y_ref, o_ref, buf0, buf1,
                    sems):
    rows = x_ref.shape[0]
    i = pl.program_id(0)
    slot = i & 1

    def start_gathers(i0_ref, i1_ref, sl):
        def issue(r, c):
            pltpu.make_async_copy(y_ref.at[pl.ds(i0_ref[0, 0, r], 1)], buf0.at[sl, pl.ds(r, 1)],
                                  sems.at[sl]).start()
            pltpu.make_async_copy(y_ref.at[pl.ds(i1_ref[0, 0, r], 1)], buf1.at[sl, pl.ds(r, 1)],
                                  sems.at[sl]).start()
            return c

        lax.fori_loop(0, rows, issue, 0, unroll=8)

    @pl.when(i == 0)
    def _():
        start_gathers(d0_ref, d1_ref, slot)

    @pl.when(i + 1 < pl.num_programs(0))
    def _():
        start_gathers(n0_ref, n1_ref, 1 - slot)

    for buf in (buf0, buf1):
        pltpu.make_async_copy(y_ref.at[pl.ds(0, rows)], buf.at[slot], sems.at[slot]).wait()
    gates = r_ref[...]
    x3 = x_ref[...] + gates[:, 2:3] * buf0[slot] + gates[:, 3:4] * buf1[slot]
    o_ref[...] = _rms(x3, g_ref[...])


def _combine_call(d0, d1, x2, router, g, y, rows):
    t, d = x2.shape
    d0, d1 = d0.reshape(t // rows, 1, rows), d1.reshape(t // rows, 1, rows)
    last = t // rows - 1
    idx = pl.BlockSpec((1, 1, rows), lambda i: (i, 0, 0), memory_space=pltpu.SMEM)
    nxt = pl.BlockSpec((1, 1, rows), lambda i: (jnp.minimum(i + 1, last), 0, 0),
                       memory_space=pltpu.SMEM)
    row = lambda i: (i, 0)
    return pl.pallas_call(
        _combine_kernel,
        grid=(t // rows,),
        in_specs=[idx, idx, nxt, nxt, pl.BlockSpec((rows, d), row),
                  pl.BlockSpec((rows, LANES), row), pl.BlockSpec(g.shape, lambda i: (0, 0)),
                  pl.BlockSpec(memory_space=pl.ANY)],
        out_specs=pl.BlockSpec((rows, d), row),
        out_shape=jax.ShapeDtypeStruct((t, d), F32),
        scratch_shapes=[pltpu.VMEM((2, rows, d), F32), pltpu.VMEM((2, rows, d), F32),
                        pltpu.SemaphoreType.DMA((2,))],
        compiler_params=_params("arbitrary"),
        name="combine",
    )(d0, d1, d0, d1, x2, router, g, y)


def _slab(cols, lo):
    return jnp.pad(cols, ((0, 0), (lo, MLA_SLAB - lo - cols.shape[1])))


def _layer_weights(w_in, w_uq, w_ukv):
    half = MLA_ROPE_DIM // 2
    o = 3 * DIFF_WIDTH + MLA_Q_RANK + MLA_KV_RANK
    kpe1, kpe2 = w_in[:, o:o + half], w_in[:, o + half:o + 2 * half]
    w_all = jnp.concatenate(
        [w_in[:, :o],
         _slab(jnp.concatenate([kpe1, kpe2], axis=1), MLA_NOPE_DIM),
         _slab(jnp.concatenate([kpe2, kpe1], axis=1), MLA_NOPE_DIM)], axis=1).astype(BF16)
    q_main, q_swap, k_nope, v_cols = [], [], [], []
    for h in range(MLA_HEADS):
        c = w_uq[:, h * MLA_QK_DIM:(h + 1) * MLA_QK_DIM]
        pe1 = c[:, MLA_NOPE_DIM:MLA_NOPE_DIM + half]
        pe2 = c[:, MLA_NOPE_DIM + half:]
        q_main.append(_slab(c, 0))
        q_swap.append(_slab(jnp.concatenate([pe2, pe1], axis=1), MLA_NOPE_DIM))
        kv = w_ukv[:, h * (MLA_NOPE_DIM + MLA_V_DIM):(h + 1) * (MLA_NOPE_DIM + MLA_V_DIM)]
        k_nope.append(_slab(kv[:, :MLA_NOPE_DIM], 0))
        v_cols.append(kv[:, MLA_NOPE_DIM:])
    wq2 = jnp.concatenate(q_main + q_swap, axis=1).astype(BF16)
    wkv2 = jnp.concatenate(k_nope + v_cols, axis=1).astype(BF16)
    return w_all, wq2, wkv2


def _rope_tables(positions):
    inv_freq = ROPE_THETA ** (-jnp.arange(0, MLA_ROPE_DIM, 2, dtype=F32) / MLA_ROPE_DIM)
    ang = positions.astype(F32).reshape(-1, 1) * inv_freq
    cos, sin = jnp.cos(ang), jnp.sin(ang)
    t = ang.shape[0]
    pad = jnp.zeros((t, MLA_SLAB - MLA_QK_DIM), F32)
    cos_t = jnp.concatenate([jnp.ones((t, MLA_NOPE_DIM), F32), cos, cos, pad], axis=1)
    sin_t = jnp.concatenate([jnp.zeros((t, MLA_NOPE_DIM), F32), -sin, sin, pad], axis=1)
    return cos_t, sin_t


def kernel(x, mem, positions, norm_mix_g, w_in, diff_lambda_q1, diff_lambda_k1, diff_lambda_q2, diff_lambda_k2, diff_out_g, mla_q_norm_g, w_mla_uq, mla_kv_norm_g, w_mla_ukv, mla_out_g, w_out, norm_cross_g, norm_mem_g, w_mem_q, w_mem_kv, w_mem_o, norm_ffn_g, w_group_router, b_group_router, w_expert_router, b_expert_router, w_expert_gate, w_expert_up, w_expert_down, norm_final_g):
    b, s, d = x.shape
    t = b * s
    depth = w_in.shape[0]
    row2 = lambda v: v.reshape(1, -1).astype(F32)
    cos_t, sin_t = _rope_tables(positions)
    pos_f = positions.astype(F32)
    slopes = 2.0 ** (-8.0 * jnp.arange(1, DIFF_HEADS + 1, dtype=F32) / DIFF_HEADS)
    pos_rel = (pos_f - pos_f[:, :1]).reshape(t, 1)
    slope_rows = jnp.pad(jnp.repeat((LOG2E * slopes).reshape(DIFF_HEADS // 2, 2), 3, axis=1),
                         ((0, 0), (0, LANES - 6)))

    diff_tq, mla_tq = min(DIFF_TQ, s), min(MLA_TQ, s)
    attn_tk = min(ATTN_TK, s)
    blk = EXPERT_ROWS
    n_assign = 2 * t
    n_blocks = -(-(n_assign + N_EXPERTS * (blk - 1)) // blk)
    n_slots = n_blocks * blk

    assert depth == 1, "single-layer problem: the combine kernel applies the final RMSNorm"
    l = 0
    x2d = x.reshape(t, d)
    w_all, wq2, wkv2 = _layer_weights(w_in[l], w_mla_uq[l], w_mla_ukv[l])
    qd, kd, vdt, qm, km, vmt = _proj_call(
        x2d, row2(norm_mix_g[l]), w_all, row2(mla_q_norm_g[l]), wq2,
        row2(mla_kv_norm_g[l]), wkv2, cos_t, sin_t, pos_rel, slope_rows, s)

    lambda_init = 0.8 - 0.6 * math.exp(-0.3 * l)
    lam = (jnp.exp(jnp.sum(diff_lambda_q1[l] * diff_lambda_k1[l]))
           - jnp.exp(jnp.sum(diff_lambda_q2[l] * diff_lambda_k2[l])) + lambda_init)
    gain_col = jnp.tile(diff_out_g[l] * (1.0 - lambda_init), 2).reshape(-1, 1).astype(F32)
    shp = lambda a: a.reshape(b, s, a.shape[-1])
    diff_out = _diff_attn_call(lam.reshape(1).astype(F32), shp(qd), shp(kd), vdt, gain_col,
                               diff_tq, attn_tk)
    mla_out = _mla_attn_call(shp(qm), shp(km), vmt, mla_tq, attn_tk, MLA_HEADS_PER_STEP)

    kmem, vmem = _mem_kv_call(mem, row2(norm_mem_g[l]), w_mem_kv[l].astype(BF16))
    w_router = jnp.pad(
        jnp.concatenate([w_group_router[l], w_expert_router[l]], axis=1),
        ((0, 0), (0, LANES - N_GROUPS - N_EXPERTS)))
    w_router_hi = w_router.astype(BF16)
    w_router = jnp.stack([w_router_hi, (w_router - w_router_hi.astype(F32)).astype(BF16)])
    b_router = jnp.pad(jnp.concatenate([b_group_router[l], b_expert_router[l]]),
                       (0, LANES - N_GROUPS - N_EXPERTS))
    x2, h3, router, router_t, counts = _mix_call(
        x2d, diff_out.reshape(t, DIFF_WIDTH), mla_out.reshape(t, MLA_WIDTH),
        row2(mla_out_g[l]), w_out[l].astype(BF16), row2(norm_cross_g[l]),
        w_mem_q[l].astype(BF16), kmem, vmem, w_mem_o[l].astype(BF16),
        row2(norm_ffn_g[l]), w_router, row2(b_router), s)

    cnt = counts[0, ROUTER_LANE0:ROUTER_LANE0 + N_EXPERTS].astype(jnp.int32)
    padded = ((cnt + blk - 1) // blk) * blk
    seg_end = jnp.cumsum(padded)
    seg_start = seg_end - padded
    ridx = router_t.astype(jnp.int32)
    expert_ids = jnp.arange(N_EXPERTS, dtype=jnp.int32)[:, None]

    def slot_of(choice):
        start = jnp.sum(jnp.where(ridx[choice][None, :] == expert_ids, seg_start[:, None], 0),
                        axis=0)
        return start + ridx[4 + choice]

    dest0, dest1 = slot_of(0), slot_of(1)
    block_start = jnp.arange(n_blocks, dtype=jnp.int32) * blk
    block_expert = jnp.minimum(
        jnp.sum((seg_end[None, :] <= block_start[:, None]).astype(jnp.int32), axis=1),
        N_EXPERTS - 1)

    xs = _dispatch_call(dest0, dest1, h3, jnp.zeros((n_slots, d), F32), min(DISPATCH_ROWS, t))
    y = _expert_call(block_expert, xs, w_expert_gate[l], w_expert_up[l], w_expert_down[l], blk)
    out = _combine_call(dest0, dest1, x2, router, row2(norm_final_g), y, min(COMBINE_ROWS, t))
    return out.reshape(b, s, d)
```

```python
import functools
import math

import jax
import jax.numpy as jnp
from jax import lax
from jax.experimental import pallas as pl
from jax.experimental.pallas import tpu as pltpu

F32 = jnp.float32
BF16 = jnp.bfloat16

RMS_EPS = 1e-6
ROPE_THETA = 10000.0
MASK_VALUE = -1e30

DIFF_HEADS = 8
DIFF_HALF_DIM = 32
DIFF_V_DIM = 64
DIFF_WIDTH = DIFF_HEADS * DIFF_V_DIM

MLA_HEADS = 8
MLA_Q_RANK = 256
MLA_KV_RANK = 128
MLA_NOPE_DIM = 64
MLA_ROPE_DIM = 32
MLA_QK_DIM = MLA_NOPE_DIM + MLA_ROPE_DIM
MLA_V_DIM = 64
MLA_WIDTH = MLA_HEADS * MLA_V_DIM

MEM_HEADS = 4
MEM_HEAD_DIM = 128
MEM_WIDTH = MEM_HEADS * MEM_HEAD_DIM

N_GROUPS = 4
EXPERTS_PER_GROUP = 8
N_EXPERTS = N_GROUPS * EXPERTS_PER_GROUP
EXPERT_FF = 256

LANES = 128
MLA_SLAB = LANES
ROUTER_LANE0 = N_GROUPS
BF16_SUBLANES = 16
V_ONES_ROW = DIFF_V_DIM
V_ROWS = DIFF_V_DIM + BF16_SUBLANES
ROW_CHUNKS = 8

PROJ_ROWS = 512
MIX_ROWS = 512
DISPATCH_ROWS = 512
COMBINE_ROWS = 256
EXPERT_ROWS = 256
ATTN_TK = 512
DIFF_TQ = 512
MLA_TQ = 512
MLA_HEADS_PER_STEP = 4
LOG2E = math.log2(math.e)
VMEM_LIMIT = 48 * 1024 * 1024

_NT = (((1,), (1,)), ((), ()))


def _rms(x, gain):
    return x * lax.rsqrt(jnp.mean(x * x, axis=-1, keepdims=True) + RMS_EPS) * gain


def _dot(a, b):
    return jnp.dot(a, b, preferred_element_type=F32)


def _params(*sem):
    return pltpu.CompilerParams(dimension_semantics=sem, vmem_limit_bytes=VMEM_LIMIT)


def _values_transposed(v, heads):
    vt = v.T
    n = v.shape[0]
    tail = jnp.where(lax.broadcasted_iota(jnp.int32, (V_ROWS - V_ONES_ROW, n), 0) == 0, 1.0, 0.0)
    pieces = []
    for h in range(heads):
        pieces += [vt[h * V_ONES_ROW:(h + 1) * V_ONES_ROW], tail]
    return jnp.concatenate(pieces, axis=0).astype(BF16)


def _proj_kernel(x_ref, g_ref, w_ref, gq_ref, wq_ref, gkv_ref, wkv_ref, cos_ref, sin_ref,
                 pos_ref, slope_ref, qd_ref, kd_ref, vd_ref, qm_ref, km_ref, vm_ref):
    xn = _rms(x_ref[...], g_ref[...]).astype(BF16)
    r = _dot(xn, w_ref[...])
    w = DIFF_WIDTH
    qd_ref[...] = (r[:, 0:w] * (DIFF_HALF_DIM ** -0.5 * LOG2E)).astype(BF16)
    lane = lax.broadcasted_iota(jnp.int32, (x_ref.shape[0], LANES), 1)
    for p in range(DIFF_HEADS // 2):
        bias = pos_ref[...] * slope_ref[p:p + 1, :]
        hi = bias.astype(BF16)
        mid = (bias - hi.astype(F32)).astype(BF16)
        low = (bias - hi.astype(F32) - mid.astype(F32)).astype(BF16)
        term = jnp.where((lane == 0) | (lane == 3), hi, jnp.where((lane == 1) | (lane == 4), mid, low))
        kd_ref[:, 2 * p * LANES:(2 * p + 1) * LANES] = r[:, w + p * LANES:w + (p + 1) * LANES].astype(BF16)
        kd_ref[:, (2 * p + 1) * LANES:(2 * p + 2) * LANES] = term
    vd_ref[0] = _values_transposed(r[:, 2 * w:3 * w], DIFF_HEADS)
    o = 3 * w
    cq = r[:, o:o + MLA_Q_RANK]
    o += MLA_Q_RANK
    ckv = r[:, o:o + MLA_KV_RANK]
    o += MLA_KV_RANK
    cos = cos_ref[...]
    sin = sin_ref[...]
    kpe = r[:, o:o + MLA_SLAB] * cos + r[:, o + MLA_SLAB:o + 2 * MLA_SLAB] * sin
    q2 = _dot(_rms(cq, gq_ref[...]).astype(BF16), wq_ref[...])
    kv2 = _dot(_rms(ckv, gkv_ref[...]).astype(BF16), wkv_ref[...])
    sw = MLA_HEADS * MLA_SLAB
    for h in range(MLA_HEADS):
        lo = h * MLA_SLAB
        qh = q2[:, lo:lo + MLA_SLAB] * cos + q2[:, sw + lo:sw + lo + MLA_SLAB] * sin
        qm_ref[:, lo:lo + MLA_SLAB] = (qh * (MLA_QK_DIM ** -0.5 * LOG2E)).astype(BF16)
        km_ref[:, lo:lo + MLA_SLAB] = (kv2[:, lo:lo + MLA_SLAB] + kpe).astype(BF16)
    vm_ref[0] = _values_transposed(kv2[:, sw:sw + MLA_WIDTH], MLA_HEADS)


def _proj_call(x2d, g, w_all, gq, wq2, gkv, wkv2, cos_t, sin_t, pos_rel, slope_rows, seq):
    t, d = x2d.shape
    rows = min(PROJ_ROWS, seq)
    per_seq = seq // rows
    row = lambda i: (i, 0)
    fixed = lambda i: (0, 0)
    full = lambda a: pl.BlockSpec(a.shape, fixed)
    row_out = lambda n: (pl.BlockSpec((rows, n), row), jax.ShapeDtypeStruct((t, n), BF16))
    vt_out = lambda n: (pl.BlockSpec((1, n, rows), lambda i: (i // per_seq, 0, i % per_seq)),
                        jax.ShapeDtypeStruct((t // seq, n, seq), BF16))
    outs = [row_out(DIFF_WIDTH), row_out(DIFF_HEADS * LANES), vt_out(DIFF_HEADS * V_ROWS),
            row_out(MLA_HEADS * MLA_SLAB), row_out(MLA_HEADS * MLA_SLAB),
            vt_out(MLA_HEADS * V_ROWS)]
    return pl.pallas_call(
        _proj_kernel,
        grid=(t // rows,),
        in_specs=[pl.BlockSpec((rows, d), row), full(g), full(w_all), full(gq), full(wq2),
                  full(gkv), full(wkv2), pl.BlockSpec((rows, MLA_SLAB), row),
                  pl.BlockSpec((rows, MLA_SLAB), row), pl.BlockSpec((rows, 1), row),
                  full(slope_rows)],
        out_specs=[spec for spec, _ in outs],
        out_shape=[shape for _, shape in outs],
        compiler_params=_params("parallel"),
        name="proj",
    )(x2d, g, w_all, gq, wq2, gkv, wkv2, cos_t, sin_t, pos_rel, slope_rows)


def _softmax_step(s, col_max, vt_chunk, parts, m, *accs):
    m_new = jnp.maximum(m, col_max)
    alpha = jnp.exp2(m - m_new)
    pb = jnp.exp2(s - m_new).astype(BF16)
    accs = tuple(alpha[:, cols] * acc + _dot(vt_chunk[rows], pb[:, cols])
                 for (rows, cols), acc in zip(parts, accs))
    return (m_new,) + accs


def _normalised(acc):
    return acc[:V_ONES_ROW] * (1.0 / acc[V_ONES_ROW:V_ONES_ROW + 1])


def _causal_mask(s, key0, query0, tq):
    key = key0 + lax.broadcasted_iota(jnp.int32, s.shape, 0)
    query = query0 + (lax.broadcasted_iota(jnp.int32, s.shape, 1) & (tq - 1))
    return jnp.where(key <= query, s, MASK_VALUE)


def _causal_sweep(scores, consume, carry, n_full, slot_a, slot_b):
    scores(0, slot_a)

    def pair(i, c):
        j = 2 * i
        scores(j + 1, slot_b)
        c = consume(j, slot_a, c, False)
        scores(j + 2, slot_a)
        return consume(j + 1, slot_b, c, False)

    carry = lax.fori_loop(0, lax.shift_right_logical(n_full, 1), pair, carry)

    def odd_tail(c):
        scores(n_full, slot_b)
        c = consume(n_full - 1, slot_a, c, False)
        return consume(n_full, slot_b, c, True)

    def even_tail(c):
        return consume(n_full, slot_a, c, True)

    return lax.cond((n_full & 1) == 1, odd_tail, even_tail, carry)


def _diff_attn_kernel(lam_ref, q_ref, k_ref, vt_ref, g_ref, o_ref, qs_ref, sa_ref, sb_ref,
                      ma_ref, mb_ref, *, tq, tk):
    qi = pl.program_id(2)
    q = q_ref[0]
    lane = lax.broadcasted_iota(jnp.int32, (tq, LANES), 1)
    for v in range(4):
        keep = (lane >= v * DIFF_HALF_DIM) & (lane < (v + 1) * DIFF_HALF_DIM)
        ones = (lane >= 3 * (v // 2)) & (lane < 3 * (v // 2) + 3)
        qs_ref[v * tq:(v + 1) * tq, 0:LANES] = jnp.where(keep, q, jnp.zeros_like(q))
        qs_ref[v * tq:(v + 1) * tq, LANES:2 * LANES] = jnp.where(ones, 1.0, 0.0).astype(BF16)

    def scores(j, slot):
        start = pl.multiple_of(j * tk, tk)
        s = lax.dot_general(k_ref[0, pl.ds(start, tk), :], qs_ref[...], _NT,
                            preferred_element_type=F32)
        slot[0][...] = s
        slot[1][...] = jnp.max(s, axis=0, keepdims=True)

    def consume(j, slot, carry, masked):
        start = pl.multiple_of(j * tk, tk)
        s = slot[0][...]
        if masked:
            s = _causal_mask(s, start, qi * tq, tq)
            col_max = jnp.max(s, axis=0, keepdims=True)
        else:
            col_max = slot[1][...]
        return _softmax_step(s, col_max, vt_ref[0, :, pl.ds(start, tk)], parts, *carry)

    parts = [(slice(h * V_ROWS, (h + 1) * V_ROWS), slice(2 * h * tq, (2 * h + 2) * tq))
             for h in range(2)]
    init = (jnp.full((1, 4 * tq), MASK_VALUE, F32),
            jnp.zeros((V_ROWS, 2 * tq), F32), jnp.zeros((V_ROWS, 2 * tq), F32))
    _, *accs = _causal_sweep(scores, consume, init, (qi * tq) // tk,
                             (sa_ref, ma_ref), (sb_ref, mb_ref))

    lam = lam_ref[0]
    halves = []
    for acc in accs:
        on = _normalised(acc)
        o = on[:, :tq] - lam * on[:, tq:]
        halves.append(o * lax.rsqrt(jnp.mean(o * o, axis=0, keepdims=True) + RMS_EPS))
    o_ref[0] = (jnp.concatenate(halves, axis=0) * g_ref[...]).T.astype(o_ref.dtype)


def _diff_attn_call(lam, qd, kd, vdt, g_col, tq, tk):
    b, s, _ = qd.shape
    pairs = DIFF_HEADS // 2
    return pl.pallas_call(
        functools.partial(_diff_attn_kernel, tq=tq, tk=tk),
        grid=(b, pairs, s // tq),
        in_specs=[pl.BlockSpec(memory_space=pltpu.SMEM),
                  pl.BlockSpec((1, tq, LANES), lambda bi, p, qi: (bi, qi, p)),
                  pl.BlockSpec((1, s, 2 * LANES), lambda bi, p, qi: (bi, 0, p)),
                  pl.BlockSpec((1, 2 * V_ROWS, s), lambda bi, p, qi: (bi, p, 0)),
                  pl.BlockSpec((LANES, 1), lambda bi, p, qi: (0, 0))],
        out_specs=pl.BlockSpec((1, tq, LANES), lambda bi, p, qi: (bi, qi, p)),
        out_shape=jax.ShapeDtypeStruct((b, s, DIFF_WIDTH), BF16),
        scratch_shapes=[pltpu.VMEM((4 * tq, 2 * LANES), BF16), pltpu.VMEM((tk, 4 * tq), F32),
                        pltpu.VMEM((tk, 4 * tq), F32), pltpu.VMEM((1, 4 * tq), F32),
                        pltpu.VMEM((1, 4 * tq), F32)],
        compiler_params=_params("parallel", "parallel", "arbitrary"),
        name="diff_attn",
    )(lam, qd, kd, vdt, g_col)


def _mla_attn_kernel(q_ref, k_ref, vt_ref, o_ref, sa_ref, sb_ref, ma_ref, mb_ref, *, tq, tk,
                     heads):
    qi = pl.program_id(2)
    slabs = [slice(hh * MLA_SLAB, (hh + 1) * MLA_SLAB) for hh in range(heads)]

    def scores(j, slot):
        start = pl.multiple_of(j * tk, tk)
        for hh, sl in enumerate(slabs):
            s = lax.dot_general(k_ref[0, pl.ds(start, tk), sl], q_ref[0, :, sl], _NT,
                                preferred_element_type=F32)
            slot[0][hh] = s
            slot[1][hh] = jnp.max(s, axis=0, keepdims=True)

    def consume(j, slot, carry, masked):
        start = pl.multiple_of(j * tk, tk)
        vtc = vt_ref[0, :, pl.ds(start, tk)]
        out = []
        for hh in range(heads):
            s = slot[0][hh]
            if masked:
                s = _causal_mask(s, start, qi * tq, tq)
                col_max = jnp.max(s, axis=0, keepdims=True)
            else:
                col_max = slot[1][hh]
            rows = slice(hh * V_ROWS, (hh + 1) * V_ROWS)
            out.extend(_softmax_step(s, col_max, vtc, [(rows, slice(None))],
                                     *carry[2 * hh:2 * hh + 2]))
        return tuple(out)

    init = (jnp.full((1, tq), MASK_VALUE, F32), jnp.zeros((V_ROWS, tq), F32)) * heads
    carry = _causal_sweep(scores, consume, init, (qi * tq) // tk,
                          (sa_ref, ma_ref), (sb_ref, mb_ref))
    o = jnp.concatenate([_normalised(carry[2 * hh + 1]) for hh in range(heads)], axis=0)
    o_ref[0] = o.T.astype(o_ref.dtype)


def _mla_attn_call(qm, km, vmt, tq, tk, heads):
    b, s, _ = qm.shape
    groups = MLA_HEADS // heads
    return pl.pallas_call(
        functools.partial(_mla_attn_kernel, tq=tq, tk=tk, heads=heads),
        grid=(b, groups, s // tq),
        in_specs=[pl.BlockSpec((1, tq, heads * MLA_SLAB), lambda bi, p, qi: (bi, qi, p)),
                  pl.BlockSpec((1, s, heads * MLA_SLAB), lambda bi, p, qi: (bi, 0, p)),
                  pl.BlockSpec((1, heads * V_ROWS, s), lambda bi, p, qi: (bi, p, 0))],
        out_specs=pl.BlockSpec((1, tq, heads * MLA_V_DIM), lambda bi, p, qi: (bi, qi, p)),
        out_shape=jax.ShapeDtypeStruct((b, s, MLA_WIDTH), BF16),
        scratch_shapes=[pltpu.VMEM((heads, tk, tq), F32), pltpu.VMEM((heads, tk, tq), F32),
                        pltpu.VMEM((heads, 1, tq), F32), pltpu.VMEM((heads, 1, tq), F32)],
        compiler_params=_params("parallel", "parallel", "arbitrary"),
        name="mla_attn",
    )(qm, km, vmt)


def _mem_kv_kernel(mem_ref, g_ref, w_ref, k_ref, v_ref):
    kv = _dot(_rms(mem_ref[0], g_ref[...]).astype(BF16), w_ref[...])
    k_ref[0] = kv[:, :MEM_WIDTH].astype(BF16)
    v_ref[0] = kv[:, MEM_WIDTH:].astype(BF16)


def _mem_kv_call(mem, g, w_kv):
    b, n_mem, d = mem.shape
    blk = pl.BlockSpec((1, n_mem, MEM_WIDTH), lambda bi: (bi, 0, 0))
    return pl.pallas_call(
        _mem_kv_kernel,
        grid=(b,),
        in_specs=[pl.BlockSpec((1, n_mem, d), lambda bi: (bi, 0, 0)),
                  pl.BlockSpec(g.shape, lambda bi: (0, 0)),
                  pl.BlockSpec(w_kv.shape, lambda bi: (0, 0))],
        out_specs=[blk, blk],
        out_shape=[jax.ShapeDtypeStruct((b, n_mem, MEM_WIDTH), BF16)] * 2,
        compiler_params=_params("parallel"),
        name="mem_kv",
    )(mem, g, w_kv)


def _store_tile_rows(ref, x):
    n = x.shape[0]
    chunks = x.shape[1] // LANES
    for c in range(chunks):
        ref[pl.ds(c, n, stride=chunks), :] = x[:, c * LANES:(c + 1) * LANES]


def _load_tile_rows(ref, chunks):
    n = ref.shape[0] // chunks
    return jnp.concatenate([ref[pl.ds(c, n, stride=chunks), :] for c in range(chunks)], axis=1)


def _tile_row(ref, r, chunks):
    return ref.at[pl.ds(pl.multiple_of(r * chunks, chunks), chunks)]


def _mix_kernel(x_ref, d_ref, m_ref, gm_ref, wo_ref, gc_ref, wq_ref, km_ref, vm_ref, wmo_ref,
                gf_ref, wr_ref, br_ref, x2_ref, h3_ref, r_ref, rt_ref, cnt_ref, run_ref):
    rows = x_ref.shape[0]

    @pl.when(pl.program_id(0) == 0)
    def _():
        run_ref[...] = jnp.zeros_like(run_ref)

    mla_n = _rms(m_ref[...].astype(F32), gm_ref[...]).astype(BF16)
    x1 = (x_ref[...] + _dot(d_ref[...], wo_ref[0:DIFF_WIDTH, :])
          + _dot(mla_n, wo_ref[DIFF_WIDTH:DIFF_WIDTH + MLA_WIDTH, :]))

    qx = _dot(_rms(x1, gc_ref[...]).astype(BF16), wq_ref[...])
    heads = []
    for h in range(MEM_HEADS):
        sl = slice(h * MEM_HEAD_DIM, (h + 1) * MEM_HEAD_DIM)
        sc = lax.dot_general(qx[:, sl].astype(BF16), km_ref[0, :, sl], _NT,
                             preferred_element_type=F32) * MEM_HEAD_DIM ** -0.5
        p = jnp.exp(sc - jnp.max(sc, axis=-1, keepdims=True))
        o = _dot(p.astype(BF16), vm_ref[0, :, sl]) * (1.0 / jnp.sum(p, axis=-1, keepdims=True))
        heads.append(o.astype(BF16))
    x2 = x1 + _dot(jnp.concatenate(heads, axis=1), wmo_ref[...])
    x2_ref[...] = x2

    h3 = _rms(x2, gf_ref[...])
    _store_tile_rows(h3_ref, h3)
    h_hi = h3.astype(BF16)
    h_lo = (h3 - h_hi.astype(F32)).astype(BF16)
    logits = (_dot(h_hi, wr_ref[0]) + _dot(h_lo, wr_ref[0]) + _dot(h_hi, wr_ref[1])
              + br_ref[...])
    lane = lax.broadcasted_iota(jnp.int32, (rows, LANES), 1).astype(F32)
    far = float(4 * LANES)

    def first_argmax(vals):
        top = jnp.max(vals, axis=-1, keepdims=True)
        return top, jnp.min(jnp.where(vals == top, lane, far), axis=-1, keepdims=True)

    g_logits = jnp.where(lane < N_GROUPS, logits, MASK_VALUE)
    g_top, g_sel = first_argmax(g_logits)
    g_gate = 1.0 / jnp.sum(jnp.exp(g_logits - g_top), axis=-1, keepdims=True)
    lo = ROUTER_LANE0 + EXPERTS_PER_GROUP * g_sel
    e_logits = jnp.where(lane >= lo, jnp.where(lane < lo + EXPERTS_PER_GROUP, logits, MASK_VALUE),
                         MASK_VALUE)
    top1, lane1 = first_argmax(e_logits)
    top2, lane2 = first_argmax(jnp.where(lane == lane1, MASK_VALUE, e_logits))
    t = jnp.exp(top2 - top1)
    gate1 = g_gate / (1.0 + t)
    gate2 = gate1 * t

    hot = jnp.where(lane == lane1, 1.0, 0.0) + jnp.where(lane == lane2, 1.0, 0.0)
    tri = (lax.broadcasted_iota(jnp.int32, (rows, rows), 1)
           < lax.broadcasted_iota(jnp.int32, (rows, rows), 0))
    before = _dot(jnp.where(tri, 1.0, 0.0).astype(BF16), hot.astype(BF16)) + run_ref[0:1, :]
    rank1 = jnp.sum(jnp.where(lane == lane1, before, 0.0), axis=-1, keepdims=True)
    rank2 = jnp.sum(jnp.where(lane == lane2, before, 0.0), axis=-1, keepdims=True)
    run_ref[...] = run_ref[...] + jnp.sum(hot, axis=0, keepdims=True)
    cnt_ref[...] = run_ref[...]

    fields = (lane1 - ROUTER_LANE0, lane2 - ROUTER_LANE0, gate1, gate2, rank1, rank2)
    out = jnp.zeros((rows, LANES), F32)
    for i, f in enumerate(fields):
        out = jnp.where(lane == float(i), f, out)
    r_ref[...] = out
    rt_ref[...] = out.T[0:8, :]


def _mix_call(x2d, diff, mla, gm, wo, gc, wq, kmem, vmem, wmo, gf, wr, br, seq):
    t, d = x2d.shape
    rows = min(MIX_ROWS, seq)
    per_seq = seq // rows
    row = lambda i: (i, 0)
    fixed = lambda i: (0, 0)
    full = lambda a: pl.BlockSpec(a.shape, lambda i: (0,) * a.ndim)
    memblk = pl.BlockSpec((1,) + kmem.shape[1:], lambda i: (i // per_seq, 0, 0))
    return pl.pallas_call(
        _mix_kernel,
        grid=(t // rows,),
        in_specs=[pl.BlockSpec((rows, d), row), pl.BlockSpec((rows, DIFF_WIDTH), row),
                  pl.BlockSpec((rows, MLA_WIDTH), row), full(gm), full(wo), full(gc), full(wq),
                  memblk, memblk, full(wmo), full(gf), full(wr), full(br)],
        out_specs=[pl.BlockSpec((rows, d), row), pl.BlockSpec((rows * d // LANES, LANES), row),
                   pl.BlockSpec((rows, LANES), row), pl.BlockSpec((8, rows), lambda i: (0, i)),
                   pl.BlockSpec((8, LANES), fixed)],
        out_shape=[jax.ShapeDtypeStruct((t, d), F32),
                   jax.ShapeDtypeStruct((t * d // LANES, LANES), F32),
                   jax.ShapeDtypeStruct((t, LANES), F32), jax.ShapeDtypeStruct((8, t), F32),
                   jax.ShapeDtypeStruct((8, LANES), F32)],
        scratch_shapes=[pltpu.VMEM((8, LANES), F32)],
        compiler_params=_params("arbitrary"),
        name="mix",
    )(x2d, diff, mla, gm, wo, gc, wq, kmem, vmem, wmo, gf, wr, br)


def _dispatch_kernel(d0_ref, d1_ref, h_ref, xs_in_ref, xs_ref, sem):
    del xs_in_ref
    rows = h_ref.shape[0] // ROW_CHUNKS

    def issue(r, c):
        src = _tile_row(h_ref, r, ROW_CHUNKS)
        pltpu.make_async_copy(src, _tile_row(xs_ref, d0_ref[0, 0, r], ROW_CHUNKS), sem).start()
        pltpu.make_async_copy(src, _tile_row(xs_ref, d1_ref[0, 0, r], ROW_CHUNKS), sem).start()
        return c

    lax.fori_loop(0, rows, issue, 0, unroll=8)
    for _ in range(2):
        pltpu.make_async_copy(h_ref, xs_ref.at[pl.ds(0, rows * ROW_CHUNKS)], sem).wait()


def _dispatch_call(d0, d1, h3, xs0, rows):
    t = h3.shape[0] // ROW_CHUNKS
    d0, d1 = d0.reshape(t // rows, 1, rows), d1.reshape(t // rows, 1, rows)
    idx = pl.BlockSpec((1, 1, rows), lambda i: (i, 0, 0), memory_space=pltpu.SMEM)
    return pl.pallas_call(
        _dispatch_kernel,
        grid=(t // rows,),
        in_specs=[idx, idx, pl.BlockSpec((rows * ROW_CHUNKS, LANES), lambda i: (i, 0)),
                  pl.BlockSpec(memory_space=pl.ANY)],
        out_specs=pl.BlockSpec(memory_space=pl.ANY),
        out_shape=jax.ShapeDtypeStruct(xs0.shape, xs0.dtype),
        scratch_shapes=[pltpu.SemaphoreType.DMA(())],
        input_output_aliases={3: 0},
        compiler_params=_params("arbitrary"),
        name="dispatch",
    )(d0, d1, h3, xs0)


def _expert_kernel(be_ref, x_ref, wg_ref, wu_ref, wd_ref, y_ref, wg_bf, wu_bf, wd_bf):
    i = pl.program_id(0)

    @pl.when((i == 0) | (be_ref[i] != be_ref[jnp.maximum(i - 1, 0)]))
    def _():
        wg_bf[...] = wg_ref[0].astype(BF16)
        wu_bf[...] = wu_ref[0].astype(BF16)
        wd_bf[...] = wd_ref[0].astype(BF16)

    x = _load_tile_rows(x_ref, ROW_CHUNKS).astype(BF16)
    gate = _dot(x, wg_bf[...])
    up = _dot(x, wu_bf[...])
    hidden = gate * (1.0 / (1.0 + jnp.exp(-gate))) * up
    _store_tile_rows(y_ref, _dot(hidden.astype(BF16), wd_bf[...]))


def _expert_call(block_expert, xs, wg, wu, wd, rows):
    _, d, ff = wg.shape
    n_slots = xs.shape[0] // ROW_CHUNKS
    grid_spec = pltpu.PrefetchScalarGridSpec(
        num_scalar_prefetch=1,
        grid=(n_slots // rows,),
        in_specs=[pl.BlockSpec((rows * ROW_CHUNKS, LANES), lambda i, be: (i, 0)),
                  pl.BlockSpec((1, d, ff), lambda i, be: (be[i], 0, 0)),
                  pl.BlockSpec((1, d, ff), lambda i, be: (be[i], 0, 0)),
                  pl.BlockSpec((1, ff, d), lambda i, be: (be[i], 0, 0))],
        out_specs=pl.BlockSpec((rows * ROW_CHUNKS, LANES), lambda i, be: (i, 0)),
        scratch_shapes=[pltpu.VMEM((d, ff), BF16), pltpu.VMEM((d, ff), BF16),
                        pltpu.VMEM((ff, d), BF16)],
    )
    return pl.pallas_call(
        _expert_kernel,
        grid_spec=grid_spec,
        out_shape=jax.ShapeDtypeStruct(xs.shape, F32),
        compiler_params=_params("arbitrary"),
        name="experts",
    )(block_expert, xs, wg, wu, wd)


def _combine_kernel(d0_ref, d1_ref, n0_ref, n1_ref, x_ref, r_ref, g_ref, y_ref, o_ref, buf0, buf1,
                    sems):
    rows = x_ref.shape[0]
    i = pl.program_id(0)
    slot = i & 1

    def start_gathers(i0_ref, i1_ref, sl):
        def issue(r, c):
            pltpu.make_async_copy(_tile_row(y_ref, i0_ref[0, 0, r], ROW_CHUNKS),
                                  _tile_row(buf0.at[sl], r, ROW_CHUNKS), sems.at[sl]).start()
            pltpu.make_async_copy(_tile_row(y_ref, i1_ref[0, 0, r], ROW_CHUNKS),
                                  _tile_row(buf1.at[sl], r, ROW_CHUNKS), sems.at[sl]).start()
            return c

        lax.fori_loop(0, rows, issue, 0, unroll=8)

    @pl.when(i == 0)
    def _():
        start_gathers(d0_ref, d1_ref, slot)

    @pl.when(i + 1 < pl.num_programs(0))
    def _():
        start_gathers(n0_ref, n1_ref, 1 - slot)

    for buf in (buf0, buf1):
        pltpu.make_async_copy(y_ref.at[pl.ds(0, rows * ROW_CHUNKS)], buf.at[slot],
                              sems.at[slot]).wait()
    gates = r_ref[...]
    x3 = (x_ref[...] + gates[:, 2:3] * _load_tile_rows(buf0.at[slot], ROW_CHUNKS)
          + gates[:, 3:4] * _load_tile_rows(buf1.at[slot], ROW_CHUNKS))
    o_ref[...] = _rms(x3, g_ref[...])


def _combine_call(d0, d1, x2, router, g, y, rows):
    t, d = x2.shape
    d0, d1 = d0.reshape(t // rows, 1, rows), d1.reshape(t // rows, 1, rows)
    last = t // rows - 1
    idx = pl.BlockSpec((1, 1, rows), lambda i: (i, 0, 0), memory_space=pltpu.SMEM)
    nxt = pl.BlockSpec((1, 1, rows), lambda i: (jnp.minimum(i + 1, last), 0, 0),
                       memory_space=pltpu.SMEM)
    row = lambda i: (i, 0)
    return pl.pallas_call(
        _combine_kernel,
        grid=(t // rows,),
        in_specs=[idx, idx, nxt, nxt, pl.BlockSpec((rows, d), row),
                  pl.BlockSpec((rows, LANES), row), pl.BlockSpec(g.shape, lambda i: (0, 0)),
                  pl.BlockSpec(memory_space=pl.ANY)],
        out_specs=pl.BlockSpec((rows, d), row),
        out_shape=jax.ShapeDtypeStruct((t, d), F32),
        scratch_shapes=[pltpu.VMEM((2, rows * ROW_CHUNKS, LANES), F32),
                        pltpu.VMEM((2, rows * ROW_CHUNKS, LANES), F32),
                        pltpu.SemaphoreType.DMA((2,))],
        compiler_params=_params("arbitrary"),
        name="combine",
    )(d0, d1, d0, d1, x2, router, g, y)


def _slab(cols, lo):
    return jnp.pad(cols, ((0, 0), (lo, MLA_SLAB - lo - cols.shape[1])))


def _layer_weights(w_in, w_uq, w_ukv):
    half = MLA_ROPE_DIM // 2
    o = 3 * DIFF_WIDTH + MLA_Q_RANK + MLA_KV_RANK
    kpe1, kpe2 = w_in[:, o:o + half], w_in[:, o + half:o + 2 * half]
    w_all = jnp.concatenate(
        [w_in[:, :o],
         _slab(jnp.concatenate([kpe1, kpe2], axis=1), MLA_NOPE_DIM),
         _slab(jnp.concatenate([kpe2, kpe1], axis=1), MLA_NOPE_DIM)], axis=1).astype(BF16)
    q_main, q_swap, k_nope, v_cols = [], [], [], []
    for h in range(MLA_HEADS):
        c = w_uq[:, h * MLA_QK_DIM:(h + 1) * MLA_QK_DIM]
        pe1 = c[:, MLA_NOPE_DIM:MLA_NOPE_DIM + half]
        pe2 = c[:, MLA_NOPE_DIM + half:]
        q_main.append(_slab(c, 0))
        q_swap.append(_slab(jnp.concatenate([pe2, pe1], axis=1), MLA_NOPE_DIM))
        kv = w_ukv[:, h * (MLA_NOPE_DIM + MLA_V_DIM):(h + 1) * (MLA_NOPE_DIM + MLA_V_DIM)]
        k_nope.append(_slab(kv[:, :MLA_NOPE_DIM], 0))
        v_cols.append(kv[:, MLA_NOPE_DIM:])
    wq2 = jnp.concatenate(q_main + q_swap, axis=1).astype(BF16)
    wkv2 = jnp.concatenate(k_nope + v_cols, axis=1).astype(BF16)
    return w_all, wq2, wkv2


def _rope_tables(positions):
    inv_freq = ROPE_THETA ** (-jnp.arange(0, MLA_ROPE_DIM, 2, dtype=F32) / MLA_ROPE_DIM)
    ang = positions.astype(F32).reshape(-1, 1) * inv_freq
    cos, sin = jnp.cos(ang), jnp.sin(ang)
    t = ang.shape[0]
    pad = jnp.zeros((t, MLA_SLAB - MLA_QK_DIM), F32)
    cos_t = jnp.concatenate([jnp.ones((t, MLA_NOPE_DIM), F32), cos, cos, pad], axis=1)
    sin_t = jnp.concatenate([jnp.zeros((t, MLA_NOPE_DIM), F32), -sin, sin, pad], axis=1)
    return cos_t, sin_t


def kernel(x, mem, positions, norm_mix_g, w_in, diff_lambda_q1, diff_lambda_k1, diff_lambda_q2, diff_lambda_k2, diff_out_g, mla_q_norm_g, w_mla_uq, mla_kv_norm_g, w_mla_ukv, mla_out_g, w_out, norm_cross_g, norm_mem_g, w_mem_q, w_mem_kv, w_mem_o, norm_ffn_g, w_group_router, b_group_router, w_expert_router, b_expert_router, w_expert_gate, w_expert_up, w_expert_down, norm_final_g):
    b, s, d = x.shape
    t = b * s
    depth = w_in.shape[0]
    row2 = lambda v: v.reshape(1, -1).astype(F32)
    cos_t, sin_t = _rope_tables(positions)
    pos_f = positions.astype(F32)
    slopes = 2.0 ** (-8.0 * jnp.arange(1, DIFF_HEADS + 1, dtype=F32) / DIFF_HEADS)
    pos_rel = (pos_f - pos_f[:, :1]).reshape(t, 1)
    slope_rows = jnp.pad(jnp.repeat((LOG2E * slopes).reshape(DIFF_HEADS // 2, 2), 3, axis=1),
                         ((0, 0), (0, LANES - 6)))

    diff_tq, mla_tq = min(DIFF_TQ, s), min(MLA_TQ, s)
    attn_tk = min(ATTN_TK, s)
    blk = EXPERT_ROWS
    n_assign = 2 * t
    n_blocks = -(-(n_assign + N_EXPERTS * (blk - 1)) // blk)
    n_slots = n_blocks * blk

    assert depth == 1, "single-layer problem: the combine kernel applies the final RMSNorm"
    l = 0
    x2d = x.reshape(t, d)
    w_all, wq2, wkv2 = _layer_weights(w_in[l], w_mla_uq[l], w_mla_ukv[l])
    qd, kd, vdt, qm, km, vmt = _proj_call(
        x2d, row2(norm_mix_g[l]), w_all, row2(mla_q_norm_g[l]), wq2,
        row2(mla_kv_norm_g[l]), wkv2, cos_t, sin_t, pos_rel, slope_rows, s)

    lambda_init = 0.8 - 0.6 * math.exp(-0.3 * l)
    lam = (jnp.exp(jnp.sum(diff_lambda_q1[l] * diff_lambda_k1[l]))
           - jnp.exp(jnp.sum(diff_lambda_q2[l] * diff_lambda_k2[l])) + lambda_init)
    gain_col = jnp.tile(diff_out_g[l] * (1.0 - lambda_init), 2).reshape(-1, 1).astype(F32)
    shp = lambda a: a.reshape(b, s, a.shape[-1])
    diff_out = _diff_attn_call(lam.reshape(1).astype(F32), shp(qd), shp(kd), vdt, gain_col,
                               diff_tq, attn_tk)
    mla_out = _mla_attn_call(shp(qm), shp(km), vmt, mla_tq, attn_tk, MLA_HEADS_PER_STEP)

    kmem, vmem = _mem_kv_call(mem, row2(norm_mem_g[l]), w_mem_kv[l].astype(BF16))
    w_router = jnp.pad(
        jnp.concatenate([w_group_router[l], w_expert_router[l]], axis=1),
        ((0, 0), (0, LANES - N_GROUPS - N_EXPERTS)))
    w_router_hi = w_router.astype(BF16)
    w_router = jnp.stack([w_router_hi, (w_router - w_router_hi.astype(F32)).astype(BF16)])
    b_router = jnp.pad(jnp.concatenate([b_group_router[l], b_expert_router[l]]),
                       (0, LANES - N_GROUPS - N_EXPERTS))
    x2, h3, router, router_t, counts = _mix_call(
        x2d, diff_out.reshape(t, DIFF_WIDTH), mla_out.reshape(t, MLA_WIDTH),
        row2(mla_out_g[l]), w_out[l].astype(BF16), row2(norm_cross_g[l]),
        w_mem_q[l].astype(BF16), kmem, vmem, w_mem_o[l].astype(BF16),
        row2(norm_ffn_g[l]), w_router, row2(b_router), s)

    cnt = counts[0, ROUTER_LANE0:ROUTER_LANE0 + N_EXPERTS].astype(jnp.int32)
    padded = ((cnt + blk - 1) // blk) * blk
    seg_end = jnp.cumsum(padded)
    seg_start = seg_end - padded
    ridx = router_t.astype(jnp.int32)
    expert_ids = jnp.arange(N_EXPERTS, dtype=jnp.int32)[:, None]

    def slot_of(choice):
        start = jnp.sum(jnp.where(ridx[choice][None, :] == expert_ids, seg_start[:, None], 0),
                        axis=0)
        return start + ridx[4 + choice]

    dest0, dest1 = slot_of(0), slot_of(1)
    block_start = jnp.arange(n_blocks, dtype=jnp.int32) * blk
    block_expert = jnp.minimum(
        jnp.sum((seg_end[None, :] <= block_start[:, None]).astype(jnp.int32), axis=1),
        N_EXPERTS - 1)

    assert d == ROW_CHUNKS * LANES
    xs = _dispatch_call(dest0, dest1, h3, jnp.zeros((n_slots * ROW_CHUNKS, LANES), F32),
                        min(DISPATCH_ROWS, t))
    y = _expert_call(block_expert, xs, w_expert_gate[l], w_expert_up[l], w_expert_down[l], blk)
    out = _combine_call(dest0, dest1, x2, router, row2(norm_final_g), y, min(COMBINE_ROWS, t))
    return out.reshape(b, s, d)
```

```python
import functools
import math

import jax
import jax.numpy as jnp
from jax import lax
from jax.experimental import pallas as pl
from jax.experimental.pallas import tpu as pltpu

F32 = jnp.float32
BF16 = jnp.bfloat16

RMS_EPS = 1e-6
ROPE_THETA = 10000.0
MASK_VALUE = -1e30

DIFF_HEADS = 8
DIFF_HALF_DIM = 32
DIFF_V_DIM = 64
DIFF_WIDTH = DIFF_HEADS * DIFF_V_DIM

MLA_HEADS = 8
MLA_Q_RANK = 256
MLA_KV_RANK = 128
MLA_NOPE_DIM = 64
MLA_ROPE_DIM = 32
MLA_QK_DIM = MLA_NOPE_DIM + MLA_ROPE_DIM
MLA_V_DIM = 64
MLA_WIDTH = MLA_HEADS * MLA_V_DIM

MEM_HEADS = 4
MEM_HEAD_DIM = 128
MEM_WIDTH = MEM_HEADS * MEM_HEAD_DIM

N_GROUPS = 4
EXPERTS_PER_GROUP = 8
N_EXPERTS = N_GROUPS * EXPERTS_PER_GROUP
EXPERT_FF = 256

LANES = 128
MLA_SLAB = LANES
ROUTER_LANE0 = N_GROUPS
BF16_SUBLANES = 16
V_ONES_ROW = DIFF_V_DIM
V_ROWS = DIFF_V_DIM + BF16_SUBLANES
ROW_CHUNKS = 8

PROJ_ROWS = 512
MIX_ROWS = 512
DISPATCH_ROWS = 512
COMBINE_ROWS = 256
EXPERT_ROWS = 256
ATTN_TK = 512
DIFF_TQ = 512
MLA_TQ = 512
MLA_HEADS_PER_STEP = 4
LOG2E = math.log2(math.e)
VMEM_LIMIT = 48 * 1024 * 1024

_NT = (((1,), (1,)), ((), ()))


def _rms(x, gain):
    return x * lax.rsqrt(jnp.mean(x * x, axis=-1, keepdims=True) + RMS_EPS) * gain


def _dot(a, b):
    return jnp.dot(a, b, preferred_element_type=F32)


def _params(*sem):
    return pltpu.CompilerParams(dimension_semantics=sem, vmem_limit_bytes=VMEM_LIMIT)


def _values_transposed(v, heads):
    vt = v.T
    n = v.shape[0]
    tail = jnp.where(lax.broadcasted_iota(jnp.int32, (V_ROWS - V_ONES_ROW, n), 0) == 0, 1.0, 0.0)
    pieces = []
    for h in range(heads):
        pieces += [vt[h * V_ONES_ROW:(h + 1) * V_ONES_ROW], tail]
    return jnp.concatenate(pieces, axis=0).astype(BF16)


def _proj_kernel(x_ref, g_ref, w_ref, gq_ref, wq_ref, gkv_ref, wkv_ref, cos_ref, sin_ref,
                 pos_ref, slope_ref, qd_ref, kd_ref, vd_ref, qm_ref, km_ref, vm_ref):
    xn = _rms(x_ref[...], g_ref[...]).astype(BF16)
    r = _dot(xn, w_ref[...])
    w = DIFF_WIDTH
    qd_ref[...] = (r[:, 0:w] * (DIFF_HALF_DIM ** -0.5 * LOG2E)).astype(BF16)
    lane = lax.broadcasted_iota(jnp.int32, (x_ref.shape[0], LANES), 1)
    for p in range(DIFF_HEADS // 2):
        bias = pos_ref[...] * slope_ref[p:p + 1, :]
        hi = bias.astype(BF16)
        mid = (bias - hi.astype(F32)).astype(BF16)
        low = (bias - hi.astype(F32) - mid.astype(F32)).astype(BF16)
        term = jnp.where((lane == 0) | (lane == 3), hi, jnp.where((lane == 1) | (lane == 4), mid, low))
        kd_ref[:, 2 * p * LANES:(2 * p + 1) * LANES] = r[:, w + p * LANES:w + (p + 1) * LANES].astype(BF16)
        kd_ref[:, (2 * p + 1) * LANES:(2 * p + 2) * LANES] = term
    vd_ref[0] = _values_transposed(r[:, 2 * w:3 * w], DIFF_HEADS)
    o = 3 * w
    cq = r[:, o:o + MLA_Q_RANK]
    o += MLA_Q_RANK
    ckv = r[:, o:o + MLA_KV_RANK]
    o += MLA_KV_RANK
    cos = cos_ref[...]
    sin = sin_ref[...]
    kpe = r[:, o:o + MLA_SLAB] * cos + r[:, o + MLA_SLAB:o + 2 * MLA_SLAB] * sin
    q2 = _dot(_rms(cq, gq_ref[...]).astype(BF16), wq_ref[...])
    kv2 = _dot(_rms(ckv, gkv_ref[...]).astype(BF16), wkv_ref[...])
    sw = MLA_HEADS * MLA_SLAB
    for h in range(MLA_HEADS):
        lo = h * MLA_SLAB
        qh = q2[:, lo:lo + MLA_SLAB] * cos + q2[:, sw + lo:sw + lo + MLA_SLAB] * sin
        qm_ref[:, lo:lo + MLA_SLAB] = (qh * (MLA_QK_DIM ** -0.5 * LOG2E)).astype(BF16)
        km_ref[:, lo:lo + MLA_SLAB] = (kv2[:, lo:lo + MLA_SLAB] + kpe).astype(BF16)
    vm_ref[0] = _values_transposed(kv2[:, sw:sw + MLA_WIDTH], MLA_HEADS)


def _proj_call(x2d, g, w_all, gq, wq2, gkv, wkv2, cos_t, sin_t, pos_rel, slope_rows, seq):
    t, d = x2d.shape
    rows = min(PROJ_ROWS, seq)
    per_seq = seq // rows
    row = lambda i: (i, 0)
    fixed = lambda i: (0, 0)
    full = lambda a: pl.BlockSpec(a.shape, fixed)
    row_out = lambda n: (pl.BlockSpec((rows, n), row), jax.ShapeDtypeStruct((t, n), BF16))
    vt_out = lambda n: (pl.BlockSpec((1, n, rows), lambda i: (i // per_seq, 0, i % per_seq)),
                        jax.ShapeDtypeStruct((t // seq, n, seq), BF16))
    outs = [row_out(DIFF_WIDTH), row_out(DIFF_HEADS * LANES), vt_out(DIFF_HEADS * V_ROWS),
            row_out(MLA_HEADS * MLA_SLAB), row_out(MLA_HEADS * MLA_SLAB),
            vt_out(MLA_HEADS * V_ROWS)]
    return pl.pallas_call(
        _proj_kernel,
        grid=(t // rows,),
        in_specs=[pl.BlockSpec((rows, d), row), full(g), full(w_all), full(gq), full(wq2),
                  full(gkv), full(wkv2), pl.BlockSpec((rows, MLA_SLAB), row),
                  pl.BlockSpec((rows, MLA_SLAB), row), pl.BlockSpec((rows, 1), row),
                  full(slope_rows)],
        out_specs=[spec for spec, _ in outs],
        out_shape=[shape for _, shape in outs],
        compiler_params=_params("parallel"),
        name="proj",
    )(x2d, g, w_all, gq, wq2, gkv, wkv2, cos_t, sin_t, pos_rel, slope_rows)


def _softmax_step(s, col_max, vt_chunk, parts, m, *accs):
    m_new = jnp.maximum(m, col_max)
    alpha = jnp.exp2(m - m_new)
    pb = jnp.exp2(s - m_new).astype(BF16)
    accs = tuple(alpha[:, cols] * acc + _dot(vt_chunk[rows], pb[:, cols])
                 for (rows, cols), acc in zip(parts, accs))
    return (m_new,) + accs


def _normalised(acc):
    return acc[:V_ONES_ROW] * (1.0 / acc[V_ONES_ROW:V_ONES_ROW + 1])


def _causal_mask(s, key0, query0, tq):
    key = key0 + lax.broadcasted_iota(jnp.int32, s.shape, 0)
    query = query0 + (lax.broadcasted_iota(jnp.int32, s.shape, 1) & (tq - 1))
    return jnp.where(key <= query, s, MASK_VALUE)


def _causal_sweep(scores, consume, carry, n_full, slot_a, slot_b):
    scores(0, slot_a)

    def pair(i, c):
        j = 2 * i
        scores(j + 1, slot_b)
        c = consume(j, slot_a, c, False)
        scores(j + 2, slot_a)
        return consume(j + 1, slot_b, c, False)

    carry = lax.fori_loop(0, lax.shift_right_logical(n_full, 1), pair, carry)

    def odd_tail(c):
        scores(n_full, slot_b)
        c = consume(n_full - 1, slot_a, c, False)
        return consume(n_full, slot_b, c, True)

    def even_tail(c):
        return consume(n_full, slot_a, c, True)

    return lax.cond((n_full & 1) == 1, odd_tail, even_tail, carry)


def _diff_attn_kernel(lam_ref, q_ref, k_ref, vt_ref, g_ref, o_ref, qs_ref, sa_ref, sb_ref,
                      ma_ref, mb_ref, *, tq, tk):
    qi = pl.program_id(2)
    q = q_ref[0]
    lane = lax.broadcasted_iota(jnp.int32, (tq, LANES), 1)
    for v in range(4):
        keep = (lane >= v * DIFF_HALF_DIM) & (lane < (v + 1) * DIFF_HALF_DIM)
        ones = (lane >= 3 * (v // 2)) & (lane < 3 * (v // 2) + 3)
        qs_ref[v * tq:(v + 1) * tq, 0:LANES] = jnp.where(keep, q, jnp.zeros_like(q))
        qs_ref[v * tq:(v + 1) * tq, LANES:2 * LANES] = jnp.where(ones, 1.0, 0.0).astype(BF16)

    def scores(j, slot):
        start = pl.multiple_of(j * tk, tk)
        s = lax.dot_general(k_ref[0, pl.ds(start, tk), :], qs_ref[...], _NT,
                            preferred_element_type=F32)
        slot[0][...] = s
        slot[1][...] = jnp.max(s, axis=0, keepdims=True)

    def consume(j, slot, carry, masked):
        start = pl.multiple_of(j * tk, tk)
        s = slot[0][...]
        if masked:
            s = _causal_mask(s, start, qi * tq, tq)
            col_max = jnp.max(s, axis=0, keepdims=True)
        else:
            col_max = slot[1][...]
        return _softmax_step(s, col_max, vt_ref[0, :, pl.ds(start, tk)], parts, *carry)

    parts = [(slice(h * V_ROWS, (h + 1) * V_ROWS), slice(2 * h * tq, (2 * h + 2) * tq))
             for h in range(2)]
    init = (jnp.full((1, 4 * tq), MASK_VALUE, F32),
            jnp.zeros((V_ROWS, 2 * tq), F32), jnp.zeros((V_ROWS, 2 * tq), F32))
    _, *accs = _causal_sweep(scores, consume, init, (qi * tq) // tk,
                             (sa_ref, ma_ref), (sb_ref, mb_ref))

    lam = lam_ref[0]
    halves = []
    for acc in accs:
        on = _normalised(acc)
        o = on[:, :tq] - lam * on[:, tq:]
        halves.append(o * lax.rsqrt(jnp.mean(o * o, axis=0, keepdims=True) + RMS_EPS))
    o_ref[0] = (jnp.concatenate(halves, axis=0) * g_ref[...]).T.astype(o_ref.dtype)


def _diff_attn_call(lam, qd, kd, vdt, g_col, tq, tk):
    b, s, _ = qd.shape
    pairs = DIFF_HEADS // 2
    return pl.pallas_call(
        functools.partial(_diff_attn_kernel, tq=tq, tk=tk),
        grid=(b, pairs, s // tq),
        in_specs=[pl.BlockSpec(memory_space=pltpu.SMEM),
                  pl.BlockSpec((1, tq, LANES), lambda bi, p, qi: (bi, qi, p)),
                  pl.BlockSpec((1, s, 2 * LANES), lambda bi, p, qi: (bi, 0, p)),
                  pl.BlockSpec((1, 2 * V_ROWS, s), lambda bi, p, qi: (bi, p, 0)),
                  pl.BlockSpec((LANES, 1), lambda bi, p, qi: (0, 0))],
        out_specs=pl.BlockSpec((1, tq, LANES), lambda bi, p, qi: (bi, qi, p)),
        out_shape=jax.ShapeDtypeStruct((b, s, DIFF_WIDTH), BF16),
        scratch_shapes=[pltpu.VMEM((4 * tq, 2 * LANES), BF16), pltpu.VMEM((tk, 4 * tq), F32),
                        pltpu.VMEM((tk, 4 * tq), F32), pltpu.VMEM((1, 4 * tq), F32),
                        pltpu.VMEM((1, 4 * tq), F32)],
        compiler_params=_params("parallel", "parallel", "arbitrary"),
        name="diff_attn",
    )(lam, qd, kd, vdt, g_col)


def _mla_attn_kernel(q_ref, k_ref, vt_ref, o_ref, sa_ref, sb_ref, ma_ref, mb_ref, *, tq, tk,
                     heads):
    qi = pl.program_id(2)
    slabs = [slice(hh * MLA_SLAB, (hh + 1) * MLA_SLAB) for hh in range(heads)]

    def scores(j, slot):
        start = pl.multiple_of(j * tk, tk)
        for hh, sl in enumerate(slabs):
            s = lax.dot_general(k_ref[0, pl.ds(start, tk), sl], q_ref[0, :, sl], _NT,
                                preferred_element_type=F32)
            slot[0][hh] = s
            slot[1][hh] = jnp.max(s, axis=0, keepdims=True)

    def consume(j, slot, carry, masked):
        start = pl.multiple_of(j * tk, tk)
        vtc = vt_ref[0, :, pl.ds(start, tk)]
        out = []
        for hh in range(heads):
            s = slot[0][hh]
            if masked:
                s = _causal_mask(s, start, qi * tq, tq)
                col_max = jnp.max(s, axis=0, keepdims=True)
            else:
                col_max = slot[1][hh]
            rows = slice(hh * V_ROWS, (hh + 1) * V_ROWS)
            out.extend(_softmax_step(s, col_max, vtc, [(rows, slice(None))],
                                     *carry[2 * hh:2 * hh + 2]))
        return tuple(out)

    init = (jnp.full((1, tq), MASK_VALUE, F32), jnp.zeros((V_ROWS, tq), F32)) * heads
    carry = _causal_sweep(scores, consume, init, (qi * tq) // tk,
                          (sa_ref, ma_ref), (sb_ref, mb_ref))
    o = jnp.concatenate([_normalised(carry[2 * hh + 1]) for hh in range(heads)], axis=0)
    o_ref[0] = o.T.astype(o_ref.dtype)


def _mla_attn_call(qm, km, vmt, tq, tk, heads):
    b, s, _ = qm.shape
    groups = MLA_HEADS // heads
    return pl.pallas_call(
        functools.partial(_mla_attn_kernel, tq=tq, tk=tk, heads=heads),
        grid=(b, groups, s // tq),
        in_specs=[pl.BlockSpec((1, tq, heads * MLA_SLAB), lambda bi, p, qi: (bi, qi, p)),
                  pl.BlockSpec((1, s, heads * MLA_SLAB), lambda bi, p, qi: (bi, 0, p)),
                  pl.BlockSpec((1, heads * V_ROWS, s), lambda bi, p, qi: (bi, p, 0))],
        out_specs=pl.BlockSpec((1, tq, heads * MLA_V_DIM), lambda bi, p, qi: (bi, qi, p)),
        out_shape=jax.ShapeDtypeStruct((b, s, MLA_WIDTH), BF16),
        scratch_shapes=[pltpu.VMEM((heads, tk, tq), F32), pltpu.VMEM((heads, tk, tq), F32),
                        pltpu.VMEM((heads, 1, tq), F32), pltpu.VMEM((heads, 1, tq), F32)],
        compiler_params=_params("parallel", "parallel", "arbitrary"),
        name="mla_attn",
    )(qm, km, vmt)


def _mem_kv_kernel(mem_ref, g_ref, w_ref, k_ref, v_ref):
    kv = _dot(_rms(mem_ref[0], g_ref[...]).astype(BF16), w_ref[...])
    k_ref[0] = kv[:, :MEM_WIDTH].astype(BF16)
    v_ref[0] = kv[:, MEM_WIDTH:].astype(BF16)


def _mem_kv_call(mem, g, w_kv):
    b, n_mem, d = mem.shape
    blk = pl.BlockSpec((1, n_mem, MEM_WIDTH), lambda bi: (bi, 0, 0))
    return pl.pallas_call(
        _mem_kv_kernel,
        grid=(b,),
        in_specs=[pl.BlockSpec((1, n_mem, d), lambda bi: (bi, 0, 0)),
                  pl.BlockSpec(g.shape, lambda bi: (0, 0)),
                  pl.BlockSpec(w_kv.shape, lambda bi: (0, 0))],
        out_specs=[blk, blk],
        out_shape=[jax.ShapeDtypeStruct((b, n_mem, MEM_WIDTH), BF16)] * 2,
        compiler_params=_params("parallel"),
        name="mem_kv",
    )(mem, g, w_kv)


def _store_tile_rows(ref, x):
    n = x.shape[0]
    chunks = x.shape[1] // LANES
    for c in range(chunks):
        ref[pl.ds(c, n, stride=chunks), :] = x[:, c * LANES:(c + 1) * LANES]


def _load_tile_rows(ref, chunks):
    n = ref.shape[0] // chunks
    return jnp.concatenate([ref[pl.ds(c, n, stride=chunks), :] for c in range(chunks)], axis=1)


def _tile_row(ref, r, chunks):
    return ref.at[pl.ds(pl.multiple_of(r * chunks, chunks), chunks)]


def _mix_kernel(x_ref, d_ref, m_ref, gm_ref, wo_ref, gc_ref, wq_ref, km_ref, vm_ref, wmo_ref,
                gf_ref, wr_ref, br_ref, x2_ref, h3_ref, r_ref, rt_ref, cnt_ref, run_ref):
    rows = x_ref.shape[0]

    @pl.when(pl.program_id(0) == 0)
    def _():
        run_ref[...] = jnp.zeros_like(run_ref)

    mla_n = _rms(m_ref[...].astype(F32), gm_ref[...]).astype(BF16)
    x1 = (x_ref[...] + _dot(d_ref[...], wo_ref[0:DIFF_WIDTH, :])
          + _dot(mla_n, wo_ref[DIFF_WIDTH:DIFF_WIDTH + MLA_WIDTH, :]))

    qx = _dot(_rms(x1, gc_ref[...]).astype(BF16), wq_ref[...])
    heads = []
    for h in range(MEM_HEADS):
        sl = slice(h * MEM_HEAD_DIM, (h + 1) * MEM_HEAD_DIM)
        sc = lax.dot_general(qx[:, sl].astype(BF16), km_ref[0, :, sl], _NT,
                             preferred_element_type=F32) * MEM_HEAD_DIM ** -0.5
        p = jnp.exp(sc - jnp.max(sc, axis=-1, keepdims=True))
        o = _dot(p.astype(BF16), vm_ref[0, :, sl]) * (1.0 / jnp.sum(p, axis=-1, keepdims=True))
        heads.append(o.astype(BF16))
    x2 = x1 + _dot(jnp.concatenate(heads, axis=1), wmo_ref[...])
    x2_ref[...] = x2

    h3 = _rms(x2, gf_ref[...])
    _store_tile_rows(h3_ref, h3)
    h_hi = h3.astype(BF16)
    h_lo = (h3 - h_hi.astype(F32)).astype(BF16)
    logits = (_dot(h_hi, wr_ref[0]) + _dot(h_lo, wr_ref[0]) + _dot(h_hi, wr_ref[1])
              + br_ref[...])
    lane = lax.broadcasted_iota(jnp.int32, (rows, LANES), 1).astype(F32)
    far = float(4 * LANES)

    def first_argmax(vals):
        top = jnp.max(vals, axis=-1, keepdims=True)
        return top, jnp.min(jnp.where(vals == top, lane, far), axis=-1, keepdims=True)

    g_logits = jnp.where(lane < N_GROUPS, logits, MASK_VALUE)
    g_top, g_sel = first_argmax(g_logits)
    g_gate = 1.0 / jnp.sum(jnp.exp(g_logits - g_top), axis=-1, keepdims=True)
    lo = ROUTER_LANE0 + EXPERTS_PER_GROUP * g_sel
    e_logits = jnp.where(lane >= lo, jnp.where(lane < lo + EXPERTS_PER_GROUP, logits, MASK_VALUE),
                         MASK_VALUE)
    top1, lane1 = first_argmax(e_logits)
    top2, lane2 = first_argmax(jnp.where(lane == lane1, MASK_VALUE, e_logits))
    t = jnp.exp(top2 - top1)
    gate1 = g_gate / (1.0 + t)
    gate2 = gate1 * t

    hot = jnp.where(lane == lane1, 1.0, 0.0) + jnp.where(lane == lane2, 1.0, 0.0)
    tri = (lax.broadcasted_iota(jnp.int32, (rows, rows), 1)
           < lax.broadcasted_iota(jnp.int32, (rows, rows), 0))
    before = _dot(jnp.where(tri, 1.0, 0.0).astype(BF16), hot.astype(BF16)) + run_ref[0:1, :]
    rank1 = jnp.sum(jnp.where(lane == lane1, before, 0.0), axis=-1, keepdims=True)
    rank2 = jnp.sum(jnp.where(lane == lane2, before, 0.0), axis=-1, keepdims=True)
    run_ref[...] = run_ref[...] + jnp.sum(hot, axis=0, keepdims=True)
    cnt_ref[...] = run_ref[...]

    fields = (lane1 - ROUTER_LANE0, lane2 - ROUTER_LANE0, gate1, gate2, rank1, rank2)
    out = jnp.zeros((rows, LANES), F32)
    for i, f in enumerate(fields):
        out = jnp.where(lane == float(i), f, out)
    r_ref[...] = out
    rt_ref[...] = out.T[0:8, :]


def _mix_call(x2d, diff, mla, gm, wo, gc, wq, kmem, vmem, wmo, gf, wr, br, seq):
    t, d = x2d.shape
    rows = min(MIX_ROWS, seq)
    per_seq = seq // rows
    row = lambda i: (i, 0)
    fixed = lambda i: (0, 0)
    full = lambda a: pl.BlockSpec(a.shape, lambda i: (0,) * a.ndim)
    memblk = pl.BlockSpec((1,) + kmem.shape[1:], lambda i: (i // per_seq, 0, 0))
    return pl.pallas_call(
        _mix_kernel,
        grid=(t // rows,),
        in_specs=[pl.BlockSpec((rows, d), row), pl.BlockSpec((rows, DIFF_WIDTH), row),
                  pl.BlockSpec((rows, MLA_WIDTH), row), full(gm), full(wo), full(gc), full(wq),
                  memblk, memblk, full(wmo), full(gf), full(wr), full(br)],
        out_specs=[pl.BlockSpec((rows, d), row), pl.BlockSpec((rows * d // LANES, LANES), row),
                   pl.BlockSpec((rows, LANES), row), pl.BlockSpec((8, rows), lambda i: (0, i)),
                   pl.BlockSpec((8, LANES), fixed)],
        out_shape=[jax.ShapeDtypeStruct((t, d), F32),
                   jax.ShapeDtypeStruct((t * d // LANES, LANES), F32),
                   jax.ShapeDtypeStruct((t, LANES), F32), jax.ShapeDtypeStruct((8, t), F32),
                   jax.ShapeDtypeStruct((8, LANES), F32)],
        scratch_shapes=[pltpu.VMEM((8, LANES), F32)],
        compiler_params=_params("arbitrary"),
        name="mix",
    )(x2d, diff, mla, gm, wo, gc, wq, kmem, vmem, wmo, gf, wr, br)


def _dispatch_kernel(d0_ref, d1_ref, h_ref, xs_in_ref, xs_ref, sem):
    del xs_in_ref
    rows = h_ref.shape[0] // ROW_CHUNKS

    def issue(r, c):
        src = _tile_row(h_ref, r, ROW_CHUNKS)
        pltpu.make_async_copy(src, _tile_row(xs_ref, d0_ref[0, 0, r], ROW_CHUNKS), sem).start(
            priority=0)
        pltpu.make_async_copy(src, _tile_row(xs_ref, d1_ref[0, 0, r], ROW_CHUNKS), sem).start(
            priority=1)
        return c

    lax.fori_loop(0, rows, issue, 0, unroll=8)
    for _ in range(2):
        pltpu.make_async_copy(h_ref, xs_ref.at[pl.ds(0, rows * ROW_CHUNKS)], sem).wait()


def _dispatch_call(d0, d1, h3, xs0, rows):
    t = h3.shape[0] // ROW_CHUNKS
    d0, d1 = d0.reshape(t // rows, 1, rows), d1.reshape(t // rows, 1, rows)
    idx = pl.BlockSpec((1, 1, rows), lambda i: (i, 0, 0), memory_space=pltpu.SMEM)
    return pl.pallas_call(
        _dispatch_kernel,
        grid=(t // rows,),
        in_specs=[idx, idx, pl.BlockSpec((rows * ROW_CHUNKS, LANES), lambda i: (i, 0)),
                  pl.BlockSpec(memory_space=pl.ANY)],
        out_specs=pl.BlockSpec(memory_space=pl.ANY),
        out_shape=jax.ShapeDtypeStruct(xs0.shape, xs0.dtype),
        scratch_shapes=[pltpu.SemaphoreType.DMA(())],
        input_output_aliases={3: 0},
        compiler_params=_params("arbitrary"),
        name="dispatch",
    )(d0, d1, h3, xs0)


def _expert_kernel(be_ref, x_ref, wg_ref, wu_ref, wd_ref, y_ref, wg_bf, wu_bf, wd_bf):
    i = pl.program_id(0)

    @pl.when((i == 0) | (be_ref[i] != be_ref[jnp.maximum(i - 1, 0)]))
    def _():
        wg_bf[...] = wg_ref[0].astype(BF16)
        wu_bf[...] = wu_ref[0].astype(BF16)
        wd_bf[...] = wd_ref[0].astype(BF16)

    x = _load_tile_rows(x_ref, ROW_CHUNKS).astype(BF16)
    gate = _dot(x, wg_bf[...])
    up = _dot(x, wu_bf[...])
    hidden = gate * (1.0 / (1.0 + jnp.exp(-gate))) * up
    _store_tile_rows(y_ref, _dot(hidden.astype(BF16), wd_bf[...]))


def _expert_call(block_expert, xs, wg, wu, wd, rows):
    _, d, ff = wg.shape
    n_slots = xs.shape[0] // ROW_CHUNKS
    grid_spec = pltpu.PrefetchScalarGridSpec(
        num_scalar_prefetch=1,
        grid=(n_slots // rows,),
        in_specs=[pl.BlockSpec((rows * ROW_CHUNKS, LANES), lambda i, be: (i, 0)),
                  pl.BlockSpec((1, d, ff), lambda i, be: (be[i], 0, 0)),
                  pl.BlockSpec((1, d, ff), lambda i, be: (be[i], 0, 0)),
                  pl.BlockSpec((1, ff, d), lambda i, be: (be[i], 0, 0))],
        out_specs=pl.BlockSpec((rows * ROW_CHUNKS, LANES), lambda i, be: (i, 0)),
        scratch_shapes=[pltpu.VMEM((d, ff), BF16), pltpu.VMEM((d, ff), BF16),
                        pltpu.VMEM((ff, d), BF16)],
    )
    return pl.pallas_call(
        _expert_kernel,
        grid_spec=grid_spec,
        out_shape=jax.ShapeDtypeStruct(xs.shape, F32),
        compiler_params=_params("arbitrary"),
        name="experts",
    )(block_expert, xs, wg, wu, wd)


def _combine_kernel(d0_ref, d1_ref, n0_ref, n1_ref, x_ref, r_ref, g_ref, y_ref, o_ref, buf0, buf1,
                    sems):
    rows = x_ref.shape[0]
    i = pl.program_id(0)
    slot = i & 1

    def start_gathers(i0_ref, i1_ref, sl):
        def issue(r, c):
            pltpu.make_async_copy(_tile_row(y_ref, i0_ref[0, 0, r], ROW_CHUNKS),
                                  _tile_row(buf0.at[sl], r, ROW_CHUNKS),
                                  sems.at[sl]).start(priority=0)
            pltpu.make_async_copy(_tile_row(y_ref, i1_ref[0, 0, r], ROW_CHUNKS),
                                  _tile_row(buf1.at[sl], r, ROW_CHUNKS),
                                  sems.at[sl]).start(priority=1)
            return c

        lax.fori_loop(0, rows, issue, 0, unroll=8)

    @pl.when(i == 0)
    def _():
        start_gathers(d0_ref, d1_ref, slot)

    @pl.when(i + 1 < pl.num_programs(0))
    def _():
        start_gathers(n0_ref, n1_ref, 1 - slot)

    for buf in (buf0, buf1):
        pltpu.make_async_copy(y_ref.at[pl.ds(0, rows * ROW_CHUNKS)], buf.at[slot],
                              sems.at[slot]).wait()
    gates = r_ref[...]
    x3 = (x_ref[...] + gates[:, 2:3] * _load_tile_rows(buf0.at[slot], ROW_CHUNKS)
          + gates[:, 3:4] * _load_tile_rows(buf1.at[slot], ROW_CHUNKS))
    o_ref[...] = _rms(x3, g_ref[...])


def _combine_call(d0, d1, x2, router, g, y, rows):
    t, d = x2.shape
    d0, d1 = d0.reshape(t // rows, 1, rows), d1.reshape(t // rows, 1, rows)
    last = t // rows - 1
    idx = pl.BlockSpec((1, 1, rows), lambda i: (i, 0, 0), memory_space=pltpu.SMEM)
    nxt = pl.BlockSpec((1, 1, rows), lambda i: (jnp.minimum(i + 1, last), 0, 0),
                       memory_space=pltpu.SMEM)
    row = lambda i: (i, 0)
    return pl.pallas_call(
        _combine_kernel,
        grid=(t // rows,),
        in_specs=[idx, idx, nxt, nxt, pl.BlockSpec((rows, d), row),
                  pl.BlockSpec((rows, LANES), row), pl.BlockSpec(g.shape, lambda i: (0, 0)),
                  pl.BlockSpec(memory_space=pl.ANY)],
        out_specs=pl.BlockSpec((rows, d), row),
        out_shape=jax.ShapeDtypeStruct((t, d), F32),
        scratch_shapes=[pltpu.VMEM((2, rows * ROW_CHUNKS, LANES), F32),
                        pltpu.VMEM((2, rows * ROW_CHUNKS, LANES), F32),
                        pltpu.SemaphoreType.DMA((2,))],
        compiler_params=_params("arbitrary"),
        name="combine",
    )(d0, d1, d0, d1, x2, router, g, y)


def _slab(cols, lo):
    return jnp.pad(cols, ((0, 0), (lo, MLA_SLAB - lo - cols.shape[1])))


def _layer_weights(w_in, w_uq, w_ukv):
    half = MLA_ROPE_DIM // 2
    o = 3 * DIFF_WIDTH + MLA_Q_RANK + MLA_KV_RANK
    kpe1, kpe2 = w_in[:, o:o + half], w_in[:, o + half:o + 2 * half]
    w_all = jnp.concatenate(
        [w_in[:, :o],
         _slab(jnp.concatenate([kpe1, kpe2], axis=1), MLA_NOPE_DIM),
         _slab(jnp.concatenate([kpe2, kpe1], axis=1), MLA_NOPE_DIM)], axis=1).astype(BF16)
    q_main, q_swap, k_nope, v_cols = [], [], [], []
    for h in range(MLA_HEADS):
        c = w_uq[:, h * MLA_QK_DIM:(h + 1) * MLA_QK_DIM]
        pe1 = c[:, MLA_NOPE_DIM:MLA_NOPE_DIM + half]
        pe2 = c[:, MLA_NOPE_DIM + half:]
        q_main.append(_slab(c, 0))
        q_swap.append(_slab(jnp.concatenate([pe2, pe1], axis=1), MLA_NOPE_DIM))
        kv = w_ukv[:, h * (MLA_NOPE_DIM + MLA_V_DIM):(h + 1) * (MLA_NOPE_DIM + MLA_V_DIM)]
        k_nope.append(_slab(kv[:, :MLA_NOPE_DIM], 0))
        v_cols.append(kv[:, MLA_NOPE_DIM:])
    wq2 = jnp.concatenate(q_main + q_swap, axis=1).astype(BF16)
    wkv2 = jnp.concatenate(k_nope + v_cols, axis=1).astype(BF16)
    return w_all, wq2, wkv2


def _rope_tables(positions):
    inv_freq = ROPE_THETA ** (-jnp.arange(0, MLA_ROPE_DIM, 2, dtype=F32) / MLA_ROPE_DIM)
    t = positions.size
    group = LANES // inv_freq.shape[0]
    ang = (positions.astype(F32).reshape(t // group, group, 1) * inv_freq).reshape(t // group, LANES)
    cos, sin = jnp.cos(ang).reshape(t, -1), jnp.sin(ang).reshape(t, -1)
    pad = jnp.zeros((t, MLA_SLAB - MLA_QK_DIM), F32)
    cos_t = jnp.concatenate([jnp.ones((t, MLA_NOPE_DIM), F32), cos, cos, pad], axis=1)
    sin_t = jnp.concatenate([jnp.zeros((t, MLA_NOPE_DIM), F32), -sin, sin, pad], axis=1)
    return cos_t, sin_t


def kernel(x, mem, positions, norm_mix_g, w_in, diff_lambda_q1, diff_lambda_k1, diff_lambda_q2, diff_lambda_k2, diff_out_g, mla_q_norm_g, w_mla_uq, mla_kv_norm_g, w_mla_ukv, mla_out_g, w_out, norm_cross_g, norm_mem_g, w_mem_q, w_mem_kv, w_mem_o, norm_ffn_g, w_group_router, b_group_router, w_expert_router, b_expert_router, w_expert_gate, w_expert_up, w_expert_down, norm_final_g):
    b, s, d = x.shape
    t = b * s
    depth = w_in.shape[0]
    row2 = lambda v: v.reshape(1, -1).astype(F32)
    cos_t, sin_t = _rope_tables(positions)
    pos_f = positions.astype(F32)
    slopes = 2.0 ** (-8.0 * jnp.arange(1, DIFF_HEADS + 1, dtype=F32) / DIFF_HEADS)
    pos_rel = (pos_f - pos_f[:, :1]).reshape(t, 1)
    slope_rows = jnp.pad(jnp.repeat((LOG2E * slopes).reshape(DIFF_HEADS // 2, 2), 3, axis=1),
                         ((0, 0), (0, LANES - 6)))

    diff_tq, mla_tq = min(DIFF_TQ, s), min(MLA_TQ, s)
    attn_tk = min(ATTN_TK, s)
    blk = EXPERT_ROWS
    n_assign = 2 * t
    n_blocks = -(-(n_assign + N_EXPERTS * (blk - 1)) // blk)
    n_slots = n_blocks * blk

    assert depth == 1, "single-layer problem: the combine kernel applies the final RMSNorm"
    l = 0
    x2d = x.reshape(t, d)
    w_all, wq2, wkv2 = _layer_weights(w_in[l], w_mla_uq[l], w_mla_ukv[l])
    qd, kd, vdt, qm, km, vmt = _proj_call(
        x2d, row2(norm_mix_g[l]), w_all, row2(mla_q_norm_g[l]), wq2,
        row2(mla_kv_norm_g[l]), wkv2, cos_t, sin_t, pos_rel, slope_rows, s)

    lambda_init = 0.8 - 0.6 * math.exp(-0.3 * l)
    lam = (jnp.exp(jnp.sum(diff_lambda_q1[l] * diff_lambda_k1[l]))
           - jnp.exp(jnp.sum(diff_lambda_q2[l] * diff_lambda_k2[l])) + lambda_init)
    gain_col = jnp.tile(diff_out_g[l] * (1.0 - lambda_init), 2).reshape(-1, 1).astype(F32)
    shp = lambda a: a.reshape(b, s, a.shape[-1])
    diff_out = _diff_attn_call(lam.reshape(1).astype(F32), shp(qd), shp(kd), vdt, gain_col,
                               diff_tq, attn_tk)
    mla_out = _mla_attn_call(shp(qm), shp(km), vmt, mla_tq, attn_tk, MLA_HEADS_PER_STEP)

    kmem, vmem = _mem_kv_call(mem, row2(norm_mem_g[l]), w_mem_kv[l].astype(BF16))
    w_router = jnp.pad(
        jnp.concatenate([w_group_router[l], w_expert_router[l]], axis=1),
        ((0, 0), (0, LANES - N_GROUPS - N_EXPERTS)))
    w_router_hi = w_router.astype(BF16)
    w_router = jnp.stack([w_router_hi, (w_router - w_router_hi.astype(F32)).astype(BF16)])
    b_router = jnp.pad(jnp.concatenate([b_group_router[l], b_expert_router[l]]),
                       (0, LANES - N_GROUPS - N_EXPERTS))
    x2, h3, router, router_t, counts = _mix_call(
        x2d, diff_out.reshape(t, DIFF_WIDTH), mla_out.reshape(t, MLA_WIDTH),
        row2(mla_out_g[l]), w_out[l].astype(BF16), row2(norm_cross_g[l]),
        w_mem_q[l].astype(BF16), kmem, vmem, w_mem_o[l].astype(BF16),
        row2(norm_ffn_g[l]), w_router, row2(b_router), s)

    cnt = counts[0, ROUTER_LANE0:ROUTER_LANE0 + N_EXPERTS].astype(jnp.int32)
    padded = ((cnt + blk - 1) // blk) * blk
    seg_end = jnp.cumsum(padded)
    seg_start = seg_end - padded
    ridx = router_t.astype(jnp.int32)
    expert_ids = jnp.arange(N_EXPERTS, dtype=jnp.int32)[:, None]

    def slot_of(choice):
        start = jnp.sum(jnp.where(ridx[choice][None, :] == expert_ids, seg_start[:, None], 0),
                        axis=0)
        return start + ridx[4 + choice]

    dest0, dest1 = slot_of(0), slot_of(1)
    block_start = jnp.arange(n_blocks, dtype=jnp.int32) * blk
    block_expert = jnp.minimum(
        jnp.sum((seg_end[None, :] <= block_start[:, None]).astype(jnp.int32), axis=1),
        N_EXPERTS - 1)

    assert d == ROW_CHUNKS * LANES
    xs = _dispatch_call(dest0, dest1, h3, jnp.zeros((n_slots * ROW_CHUNKS, LANES), F32),
                        min(DISPATCH_ROWS, t))
    y = _expert_call(block_expert, xs, w_expert_gate[l], w_expert_up[l], w_expert_down[l], blk)
    out = _combine_call(dest0, dest1, x2, router, row2(norm_final_g), y, min(COMBINE_ROWS, t))
    return out.reshape(b, s, d)
```

```python
import functools
import math

import jax
import jax.numpy as jnp
from jax import lax
from jax.experimental import pallas as pl
from jax.experimental.pallas import tpu as pltpu

F32 = jnp.float32
BF16 = jnp.bfloat16

RMS_EPS = 1e-6
ROPE_THETA = 10000.0
MASK_VALUE = -1e30

DIFF_HEADS = 8
DIFF_HALF_DIM = 32
DIFF_V_DIM = 64
DIFF_WIDTH = DIFF_HEADS * DIFF_V_DIM

MLA_HEADS = 8
MLA_Q_RANK = 256
MLA_KV_RANK = 128
MLA_NOPE_DIM = 64
MLA_ROPE_DIM = 32
MLA_QK_DIM = MLA_NOPE_DIM + MLA_ROPE_DIM
MLA_V_DIM = 64
MLA_WIDTH = MLA_HEADS * MLA_V_DIM

MEM_HEADS = 4
MEM_HEAD_DIM = 128
MEM_WIDTH = MEM_HEADS * MEM_HEAD_DIM

N_GROUPS = 4
EXPERTS_PER_GROUP = 8
N_EXPERTS = N_GROUPS * EXPERTS_PER_GROUP
EXPERT_FF = 256

LANES = 128
MLA_SLAB = LANES
ROUTER_LANE0 = N_GROUPS
BF16_SUBLANES = 16
V_ONES_ROW = DIFF_V_DIM
V_ROWS = DIFF_V_DIM + BF16_SUBLANES
ROW_CHUNKS = 8

PROJ_ROWS = 512
MIX_ROWS = 1024
DISPATCH_ROWS = 512
COMBINE_ROWS = 256
EXPERT_ROWS = 512
ATTN_TK = 512
DIFF_TQ = 512
MLA_TQ = 512
MLA_HEADS_PER_STEP = 4
LOG2E = math.log2(math.e)
VMEM_LIMIT = 48 * 1024 * 1024

_NT = (((1,), (1,)), ((), ()))


def _rms(x, gain):
    return x * lax.rsqrt(jnp.mean(x * x, axis=-1, keepdims=True) + RMS_EPS) * gain


def _dot(a, b):
    return jnp.dot(a, b, preferred_element_type=F32)


def _params(*sem):
    return pltpu.CompilerParams(dimension_semantics=sem, vmem_limit_bytes=VMEM_LIMIT)


def _values_transposed(v, heads):
    vt = v.T
    n = v.shape[0]
    tail = jnp.where(lax.broadcasted_iota(jnp.int32, (V_ROWS - V_ONES_ROW, n), 0) == 0, 1.0, 0.0)
    pieces = []
    for h in range(heads):
        pieces += [vt[h * V_ONES_ROW:(h + 1) * V_ONES_ROW], tail]
    return jnp.concatenate(pieces, axis=0).astype(BF16)


def _proj_kernel(x_ref, g_ref, w_ref, gq_ref, wq_ref, gkv_ref, wkv_ref, abs_ref, rope_ref,
                 pos_ref, slope_ref, qd_ref, kd_ref, vd_ref, qm_ref, km_ref, vm_ref):
    xn = _rms(x_ref[...], g_ref[...]).astype(BF16)
    r = _dot(xn, w_ref[...])
    w = DIFF_WIDTH
    qd_ref[...] = (r[:, 0:w] * (DIFF_HALF_DIM ** -0.5 * LOG2E)).astype(BF16)
    lane = lax.broadcasted_iota(jnp.int32, (x_ref.shape[0], LANES), 1)
    for p in range(DIFF_HEADS // 2):
        bias = pos_ref[...] * slope_ref[p:p + 1, :]
        hi = bias.astype(BF16)
        mid = (bias - hi.astype(F32)).astype(BF16)
        low = (bias - hi.astype(F32) - mid.astype(F32)).astype(BF16)
        term = jnp.where((lane == 0) | (lane == 3), hi, jnp.where((lane == 1) | (lane == 4), mid, low))
        kd_ref[:, 2 * p * LANES:(2 * p + 1) * LANES] = r[:, w + p * LANES:w + (p + 1) * LANES].astype(BF16)
        kd_ref[:, (2 * p + 1) * LANES:(2 * p + 2) * LANES] = term
    vd_ref[0] = _values_transposed(r[:, 2 * w:3 * w], DIFF_HEADS)
    o = 3 * w
    cq = r[:, o:o + MLA_Q_RANK]
    o += MLA_Q_RANK
    ckv = r[:, o:o + MLA_KV_RANK]
    o += MLA_KV_RANK
    ang = abs_ref[...] * rope_ref[0:1, :]
    cos = jnp.cos(ang)
    sin = jnp.sin(ang) * rope_ref[1:2, :]
    kpe = r[:, o:o + MLA_SLAB] * cos + r[:, o + MLA_SLAB:o + 2 * MLA_SLAB] * sin
    q2 = _dot(_rms(cq, gq_ref[...]).astype(BF16), wq_ref[...])
    kv2 = _dot(_rms(ckv, gkv_ref[...]).astype(BF16), wkv_ref[...])
    sw = MLA_HEADS * MLA_SLAB
    for h in range(MLA_HEADS):
        lo = h * MLA_SLAB
        qh = q2[:, lo:lo + MLA_SLAB] * cos + q2[:, sw + lo:sw + lo + MLA_SLAB] * sin
        qm_ref[:, lo:lo + MLA_SLAB] = (qh * (MLA_QK_DIM ** -0.5 * LOG2E)).astype(BF16)
        km_ref[:, lo:lo + MLA_SLAB] = (kv2[:, lo:lo + MLA_SLAB] + kpe).astype(BF16)
    vm_ref[0] = _values_transposed(kv2[:, sw:sw + MLA_WIDTH], MLA_HEADS)


def _proj_call(x2d, g, w_all, gq, wq2, gkv, wkv2, pos_abs, rope_rows, pos_rel, slope_rows, seq):
    t, d = x2d.shape
    rows = min(PROJ_ROWS, seq)
    per_seq = seq // rows
    row = lambda i: (i, 0)
    fixed = lambda i: (0, 0)
    full = lambda a: pl.BlockSpec(a.shape, fixed)
    row_out = lambda n: (pl.BlockSpec((rows, n), row), jax.ShapeDtypeStruct((t, n), BF16))
    vt_out = lambda n: (pl.BlockSpec((1, n, rows), lambda i: (i // per_seq, 0, i % per_seq)),
                        jax.ShapeDtypeStruct((t // seq, n, seq), BF16))
    outs = [row_out(DIFF_WIDTH), row_out(DIFF_HEADS * LANES), vt_out(DIFF_HEADS * V_ROWS),
            row_out(MLA_HEADS * MLA_SLAB), row_out(MLA_HEADS * MLA_SLAB),
            vt_out(MLA_HEADS * V_ROWS)]
    return pl.pallas_call(
        _proj_kernel,
        grid=(t // rows,),
        in_specs=[pl.BlockSpec((rows, d), row), full(g), full(w_all), full(gq), full(wq2),
                  full(gkv), full(wkv2), pl.BlockSpec((rows, 1), row), full(rope_rows),
                  pl.BlockSpec((rows, 1), row), full(slope_rows)],
        out_specs=[spec for spec, _ in outs],
        out_shape=[shape for _, shape in outs],
        compiler_params=_params("parallel"),
        name="proj",
    )(x2d, g, w_all, gq, wq2, gkv, wkv2, pos_abs, rope_rows, pos_rel, slope_rows)


def _softmax_step(s, col_max, vt_chunk, parts, m, *accs):
    m_new = jnp.maximum(m, col_max)
    alpha = jnp.exp2(m - m_new)
    pb = jnp.exp2(s - m_new).astype(BF16)
    accs = tuple(alpha[:, cols] * acc + _dot(vt_chunk[rows], pb[:, cols])
                 for (rows, cols), acc in zip(parts, accs))
    return (m_new,) + accs


def _normalised(acc):
    return acc[:V_ONES_ROW] * (1.0 / acc[V_ONES_ROW:V_ONES_ROW + 1])


def _causal_mask(s, key0, query0, tq):
    key = key0 + lax.broadcasted_iota(jnp.int32, s.shape, 0)
    query = query0 + (lax.broadcasted_iota(jnp.int32, s.shape, 1) & (tq - 1))
    return jnp.where(key <= query, s, MASK_VALUE)


def _causal_sweep(scores, consume, carry, n_full, slot_a, slot_b):
    scores(0, slot_a)

    def pair(i, c):
        j = 2 * i
        scores(j + 1, slot_b)
        c = consume(j, slot_a, c, False)
        scores(j + 2, slot_a)
        return consume(j + 1, slot_b, c, False)

    carry = lax.fori_loop(0, lax.shift_right_logical(n_full, 1), pair, carry)

    def odd_tail(c):
        scores(n_full, slot_b)
        c = consume(n_full - 1, slot_a, c, False)
        return consume(n_full, slot_b, c, True)

    def even_tail(c):
        return consume(n_full, slot_a, c, True)

    return lax.cond((n_full & 1) == 1, odd_tail, even_tail, carry)


def _diff_attn_kernel(lam_ref, q_ref, k_ref, vt_ref, g_ref, o_ref, qs_ref, sa_ref, sb_ref,
                      ma_ref, mb_ref, *, tq, tk):
    qi = pl.program_id(2)
    q = q_ref[0]
    lane = lax.broadcasted_iota(jnp.int32, (tq, LANES), 1)
    for v in range(4):
        keep = (lane >= v * DIFF_HALF_DIM) & (lane < (v + 1) * DIFF_HALF_DIM)
        ones = (lane >= 3 * (v // 2)) & (lane < 3 * (v // 2) + 3)
        qs_ref[v * tq:(v + 1) * tq, 0:LANES] = jnp.where(keep, q, jnp.zeros_like(q))
        qs_ref[v * tq:(v + 1) * tq, LANES:2 * LANES] = jnp.where(ones, 1.0, 0.0).astype(BF16)

    def scores(j, slot):
        start = pl.multiple_of(j * tk, tk)
        s = lax.dot_general(k_ref[0, pl.ds(start, tk), :], qs_ref[...], _NT,
                            preferred_element_type=F32)
        slot[0][...] = s
        slot[1][...] = jnp.max(s, axis=0, keepdims=True)

    def consume(j, slot, carry, masked):
        start = pl.multiple_of(j * tk, tk)
        s = slot[0][...]
        if masked:
            s = _causal_mask(s, start, qi * tq, tq)
            col_max = jnp.max(s, axis=0, keepdims=True)
        else:
            col_max = slot[1][...]
        return _softmax_step(s, col_max, vt_ref[0, :, pl.ds(start, tk)], parts, *carry)

    parts = [(slice(h * V_ROWS, (h + 1) * V_ROWS), slice(2 * h * tq, (2 * h + 2) * tq))
             for h in range(2)]
    init = (jnp.full((1, 4 * tq), MASK_VALUE, F32),
            jnp.zeros((V_ROWS, 2 * tq), F32), jnp.zeros((V_ROWS, 2 * tq), F32))
    _, *accs = _causal_sweep(scores, consume, init, (qi * tq) // tk,
                             (sa_ref, ma_ref), (sb_ref, mb_ref))

    lam = lam_ref[0]
    halves = []
    for acc in accs:
        on = _normalised(acc)
        o = on[:, :tq] - lam * on[:, tq:]
        halves.append(o * lax.rsqrt(jnp.mean(o * o, axis=0, keepdims=True) + RMS_EPS))
    o_ref[0] = (jnp.concatenate(halves, axis=0) * g_ref[...]).T.astype(o_ref.dtype)


def _diff_attn_call(lam, qd, kd, vdt, g_col, tq, tk):
    b, s, _ = qd.shape
    pairs = DIFF_HEADS // 2
    return pl.pallas_call(
        functools.partial(_diff_attn_kernel, tq=tq, tk=tk),
        grid=(b, pairs, s // tq),
        in_specs=[pl.BlockSpec(memory_space=pltpu.SMEM),
                  pl.BlockSpec((1, tq, LANES), lambda bi, p, qi: (bi, qi, p)),
                  pl.BlockSpec((1, s, 2 * LANES), lambda bi, p, qi: (bi, 0, p)),
                  pl.BlockSpec((1, 2 * V_ROWS, s), lambda bi, p, qi: (bi, p, 0)),
                  pl.BlockSpec((LANES, 1), lambda bi, p, qi: (0, 0))],
        out_specs=pl.BlockSpec((1, tq, LANES), lambda bi, p, qi: (bi, qi, p)),
        out_shape=jax.ShapeDtypeStruct((b, s, DIFF_WIDTH), BF16),
        scratch_shapes=[pltpu.VMEM((4 * tq, 2 * LANES), BF16), pltpu.VMEM((tk, 4 * tq), F32),
                        pltpu.VMEM((tk, 4 * tq), F32), pltpu.VMEM((1, 4 * tq), F32),
                        pltpu.VMEM((1, 4 * tq), F32)],
        compiler_params=_params("parallel", "parallel", "arbitrary"),
        name="diff_attn",
    )(lam, qd, kd, vdt, g_col)


def _mla_attn_kernel(q_ref, k_ref, vt_ref, o_ref, sa_ref, sb_ref, ma_ref, mb_ref, *, tq, tk,
                     heads):
    qi = pl.program_id(2)
    slabs = [slice(hh * MLA_SLAB, (hh + 1) * MLA_SLAB) for hh in range(heads)]

    def scores(j, slot):
        start = pl.multiple_of(j * tk, tk)
        for hh, sl in enumerate(slabs):
            s = lax.dot_general(k_ref[0, pl.ds(start, tk), sl], q_ref[0, :, sl], _NT,
                                preferred_element_type=F32)
            slot[0][hh] = s
            slot[1][hh] = jnp.max(s, axis=0, keepdims=True)

    def consume(j, slot, carry, masked):
        start = pl.multiple_of(j * tk, tk)
        vtc = vt_ref[0, :, pl.ds(start, tk)]
        out = []
        for hh in range(heads):
            s = slot[0][hh]
            if masked:
                s = _causal_mask(s, start, qi * tq, tq)
                col_max = jnp.max(s, axis=0, keepdims=True)
            else:
                col_max = slot[1][hh]
            rows = slice(hh * V_ROWS, (hh + 1) * V_ROWS)
            out.extend(_softmax_step(s, col_max, vtc, [(rows, slice(None))],
                                     *carry[2 * hh:2 * hh + 2]))
        return tuple(out)

    init = (jnp.full((1, tq), MASK_VALUE, F32), jnp.zeros((V_ROWS, tq), F32)) * heads
    carry = _causal_sweep(scores, consume, init, (qi * tq) // tk,
                          (sa_ref, ma_ref), (sb_ref, mb_ref))
    o = jnp.concatenate([_normalised(carry[2 * hh + 1]) for hh in range(heads)], axis=0)
    o_ref[0] = o.T.astype(o_ref.dtype)


def _mla_attn_call(qm, km, vmt, tq, tk, heads):
    b, s, _ = qm.shape
    groups = MLA_HEADS // heads
    return pl.pallas_call(
        functools.partial(_mla_attn_kernel, tq=tq, tk=tk, heads=heads),
        grid=(b, groups, s // tq),
        in_specs=[pl.BlockSpec((1, tq, heads * MLA_SLAB), lambda bi, p, qi: (bi, qi, p)),
                  pl.BlockSpec((1, s, heads * MLA_SLAB), lambda bi, p, qi: (bi, 0, p)),
                  pl.BlockSpec((1, heads * V_ROWS, s), lambda bi, p, qi: (bi, p, 0))],
        out_specs=pl.BlockSpec((1, tq, heads * MLA_V_DIM), lambda bi, p, qi: (bi, qi, p)),
        out_shape=jax.ShapeDtypeStruct((b, s, MLA_WIDTH), BF16),
        scratch_shapes=[pltpu.VMEM((heads, tk, tq), F32), pltpu.VMEM((heads, tk, tq), F32),
                        pltpu.VMEM((heads, 1, tq), F32), pltpu.VMEM((heads, 1, tq), F32)],
        compiler_params=_params("parallel", "parallel", "arbitrary"),
        name="mla_attn",
    )(qm, km, vmt)


def _mem_kv_kernel(mem_ref, g_ref, w_ref, k_ref, v_ref):
    kv = _dot(_rms(mem_ref[0], g_ref[...]).astype(BF16), w_ref[...])
    k_ref[0] = kv[:, :MEM_WIDTH].astype(BF16)
    v_ref[0] = kv[:, MEM_WIDTH:].astype(BF16)


def _mem_kv_call(mem, g, w_kv):
    b, n_mem, d = mem.shape
    blk = pl.BlockSpec((1, n_mem, MEM_WIDTH), lambda bi: (bi, 0, 0))
    return pl.pallas_call(
        _mem_kv_kernel,
        grid=(b,),
        in_specs=[pl.BlockSpec((1, n_mem, d), lambda bi: (bi, 0, 0)),
                  pl.BlockSpec(g.shape, lambda bi: (0, 0)),
                  pl.BlockSpec(w_kv.shape, lambda bi: (0, 0))],
        out_specs=[blk, blk],
        out_shape=[jax.ShapeDtypeStruct((b, n_mem, MEM_WIDTH), BF16)] * 2,
        compiler_params=_params("parallel"),
        name="mem_kv",
    )(mem, g, w_kv)


def _store_tile_rows(ref, x):
    n = x.shape[0]
    chunks = x.shape[1] // LANES
    for c in range(chunks):
        ref[pl.ds(c, n, stride=chunks), :] = x[:, c * LANES:(c + 1) * LANES]


def _load_tile_rows(ref, chunks):
    n = ref.shape[0] // chunks
    return jnp.concatenate([ref[pl.ds(c, n, stride=chunks), :] for c in range(chunks)], axis=1)


def _tile_row(ref, r, chunks):
    return ref.at[pl.ds(pl.multiple_of(r * chunks, chunks), chunks)]


def _mix_kernel(x_ref, d_ref, m_ref, gm_ref, wo_ref, gc_ref, wq_ref, km_ref, vm_ref, wmo_ref,
                gf_ref, wr_ref, br_ref, x2_ref, h3_ref, r_ref, rt_ref, cnt_ref, run_ref):
    rows = x_ref.shape[0]

    @pl.when(pl.program_id(0) == 0)
    def _():
        run_ref[...] = jnp.zeros_like(run_ref)

    mla_n = _rms(m_ref[...].astype(F32), gm_ref[...]).astype(BF16)
    x1 = (x_ref[...] + _dot(d_ref[...], wo_ref[0:DIFF_WIDTH, :])
          + _dot(mla_n, wo_ref[DIFF_WIDTH:DIFF_WIDTH + MLA_WIDTH, :]))

    qx = _dot(_rms(x1, gc_ref[...]).astype(BF16), wq_ref[...])
    heads = []
    for h in range(MEM_HEADS):
        sl = slice(h * MEM_HEAD_DIM, (h + 1) * MEM_HEAD_DIM)
        sc = lax.dot_general(qx[:, sl].astype(BF16), km_ref[0, :, sl], _NT,
                             preferred_element_type=F32) * MEM_HEAD_DIM ** -0.5
        p = jnp.exp(sc - jnp.max(sc, axis=-1, keepdims=True))
        o = _dot(p.astype(BF16), vm_ref[0, :, sl]) * (1.0 / jnp.sum(p, axis=-1, keepdims=True))
        heads.append(o.astype(BF16))
    x2 = x1 + _dot(jnp.concatenate(heads, axis=1), wmo_ref[...])
    x2_ref[...] = x2

    h3 = _rms(x2, gf_ref[...])
    _store_tile_rows(h3_ref, h3)
    h_hi = h3.astype(BF16)
    h_lo = (h3 - h_hi.astype(F32)).astype(BF16)
    logits = (_dot(h_hi, wr_ref[0]) + _dot(h_lo, wr_ref[0]) + _dot(h_hi, wr_ref[1])
              + br_ref[...])
    lane = lax.broadcasted_iota(jnp.int32, (rows, LANES), 1).astype(F32)
    far = float(4 * LANES)

    def first_argmax(vals):
        top = jnp.max(vals, axis=-1, keepdims=True)
        return top, jnp.min(jnp.where(vals == top, lane, far), axis=-1, keepdims=True)

    g_logits = jnp.where(lane < N_GROUPS, logits, MASK_VALUE)
    g_top, g_sel = first_argmax(g_logits)
    g_gate = 1.0 / jnp.sum(jnp.exp(g_logits - g_top), axis=-1, keepdims=True)
    lo = ROUTER_LANE0 + EXPERTS_PER_GROUP * g_sel
    e_logits = jnp.where(lane >= lo, jnp.where(lane < lo + EXPERTS_PER_GROUP, logits, MASK_VALUE),
                         MASK_VALUE)
    top1, lane1 = first_argmax(e_logits)
    top2, lane2 = first_argmax(jnp.where(lane == lane1, MASK_VALUE, e_logits))
    t = jnp.exp(top2 - top1)
    gate1 = g_gate / (1.0 + t)
    gate2 = gate1 * t

    hot = jnp.where(lane == lane1, 1.0, 0.0) + jnp.where(lane == lane2, 1.0, 0.0)
    tri = (lax.broadcasted_iota(jnp.int32, (rows, rows), 1)
           < lax.broadcasted_iota(jnp.int32, (rows, rows), 0))
    before = _dot(jnp.where(tri, 1.0, 0.0).astype(BF16), hot.astype(BF16)) + run_ref[0:1, :]
    rank1 = jnp.sum(jnp.where(lane == lane1, before, 0.0), axis=-1, keepdims=True)
    rank2 = jnp.sum(jnp.where(lane == lane2, before, 0.0), axis=-1, keepdims=True)
    run_ref[...] = run_ref[...] + jnp.sum(hot, axis=0, keepdims=True)
    cnt_ref[...] = run_ref[...]

    fields = (lane1 - ROUTER_LANE0, lane2 - ROUTER_LANE0, gate1, gate2, rank1, rank2)
    out = jnp.zeros((rows, LANES), F32)
    for i, f in enumerate(fields):
        out = jnp.where(lane == float(i), f, out)
    r_ref[...] = out
    rt_ref[...] = out.T[0:8, :]


def _mix_call(x2d, diff, mla, gm, wo, gc, wq, kmem, vmem, wmo, gf, wr, br, seq):
    t, d = x2d.shape
    rows = min(MIX_ROWS, seq)
    per_seq = seq // rows
    row = lambda i: (i, 0)
    fixed = lambda i: (0, 0)
    full = lambda a: pl.BlockSpec(a.shape, lambda i: (0,) * a.ndim)
    memblk = pl.BlockSpec((1,) + kmem.shape[1:], lambda i: (i // per_seq, 0, 0))
    return pl.pallas_call(
        _mix_kernel,
        grid=(t // rows,),
        in_specs=[pl.BlockSpec((rows, d), row), pl.BlockSpec((rows, DIFF_WIDTH), row),
                  pl.BlockSpec((rows, MLA_WIDTH), row), full(gm), full(wo), full(gc), full(wq),
                  memblk, memblk, full(wmo), full(gf), full(wr), full(br)],
        out_specs=[pl.BlockSpec((rows, d), row), pl.BlockSpec((rows * d // LANES, LANES), row),
                   pl.BlockSpec((rows, LANES), row), pl.BlockSpec((8, rows), lambda i: (0, i)),
                   pl.BlockSpec((8, LANES), fixed)],
        out_shape=[jax.ShapeDtypeStruct((t, d), F32),
                   jax.ShapeDtypeStruct((t * d // LANES, LANES), F32),
                   jax.ShapeDtypeStruct((t, LANES), F32), jax.ShapeDtypeStruct((8, t), F32),
                   jax.ShapeDtypeStruct((8, LANES), F32)],
        scratch_shapes=[pltpu.VMEM((8, LANES), F32)],
        compiler_params=_params("arbitrary"),
        name="mix",
    )(x2d, diff, mla, gm, wo, gc, wq, kmem, vmem, wmo, gf, wr, br)


def _dispatch_kernel(d0_ref, d1_ref, h_ref, xs_in_ref, xs_ref, sem):
    del xs_in_ref
    rows = h_ref.shape[0] // ROW_CHUNKS

    def issue(r, c):
        src = _tile_row(h_ref, r, ROW_CHUNKS)
        pltpu.make_async_copy(src, _tile_row(xs_ref, d0_ref[0, 0, r], ROW_CHUNKS), sem).start(
            priority=0)
        pltpu.make_async_copy(src, _tile_row(xs_ref, d1_ref[0, 0, r], ROW_CHUNKS), sem).start(
            priority=1)
        return c

    lax.fori_loop(0, rows, issue, 0, unroll=8)
    for _ in range(2):
        pltpu.make_async_copy(h_ref, xs_ref.at[pl.ds(0, rows * ROW_CHUNKS)], sem).wait()


def _dispatch_call(d0, d1, h3, xs0, rows):
    t = h3.shape[0] // ROW_CHUNKS
    d0, d1 = d0.reshape(t // rows, 1, rows), d1.reshape(t // rows, 1, rows)
    idx = pl.BlockSpec((1, 1, rows), lambda i: (i, 0, 0), memory_space=pltpu.SMEM)
    return pl.pallas_call(
        _dispatch_kernel,
        grid=(t // rows,),
        in_specs=[idx, idx, pl.BlockSpec((rows * ROW_CHUNKS, LANES), lambda i: (i, 0)),
                  pl.BlockSpec(memory_space=pl.ANY)],
        out_specs=pl.BlockSpec(memory_space=pl.ANY),
        out_shape=jax.ShapeDtypeStruct(xs0.shape, xs0.dtype),
        scratch_shapes=[pltpu.SemaphoreType.DMA(())],
        input_output_aliases={3: 0},
        compiler_params=_params("arbitrary"),
        name="dispatch",
    )(d0, d1, h3, xs0)


def _expert_kernel(be_ref, x_ref, wg_ref, wu_ref, wd_ref, y_ref, wg_bf, wu_bf, wd_bf):
    i = pl.program_id(0)

    @pl.when((i == 0) | (be_ref[i] != be_ref[jnp.maximum(i - 1, 0)]))
    def _():
        wg_bf[...] = wg_ref[0].astype(BF16)
        wu_bf[...] = wu_ref[0].astype(BF16)
        wd_bf[...] = wd_ref[0].astype(BF16)

    x = _load_tile_rows(x_ref, ROW_CHUNKS).astype(BF16)
    gate = _dot(x, wg_bf[...])
    up = _dot(x, wu_bf[...])
    hidden = gate * (1.0 / (1.0 + jnp.exp(-gate))) * up
    _store_tile_rows(y_ref, _dot(hidden.astype(BF16), wd_bf[...]))


def _expert_call(block_expert, xs, wg, wu, wd, rows):
    _, d, ff = wg.shape
    n_slots = xs.shape[0] // ROW_CHUNKS
    grid_spec = pltpu.PrefetchScalarGridSpec(
        num_scalar_prefetch=1,
        grid=(n_slots // rows,),
        in_specs=[pl.BlockSpec((rows * ROW_CHUNKS, LANES), lambda i, be: (i, 0)),
                  pl.BlockSpec((1, d, ff), lambda i, be: (be[i], 0, 0)),
                  pl.BlockSpec((1, d, ff), lambda i, be: (be[i], 0, 0)),
                  pl.BlockSpec((1, ff, d), lambda i, be: (be[i], 0, 0))],
        out_specs=pl.BlockSpec((rows * ROW_CHUNKS, LANES), lambda i, be: (i, 0)),
        scratch_shapes=[pltpu.VMEM((d, ff), BF16), pltpu.VMEM((d, ff), BF16),
                        pltpu.VMEM((ff, d), BF16)],
    )
    return pl.pallas_call(
        _expert_kernel,
        grid_spec=grid_spec,
        out_shape=jax.ShapeDtypeStruct(xs.shape, F32),
        compiler_params=_params("arbitrary"),
        name="experts",
    )(block_expert, xs, wg, wu, wd)


def _combine_kernel(d0_ref, d1_ref, n0_ref, n1_ref, x_ref, r_ref, g_ref, y_ref, o_ref, buf0, buf1,
                    sems):
    rows = x_ref.shape[0]
    i = pl.program_id(0)
    slot = i & 1

    def start_gathers(i0_ref, i1_ref, sl):
        def issue(r, c):
            pltpu.make_async_copy(_tile_row(y_ref, i0_ref[0, 0, r], ROW_CHUNKS),
                                  _tile_row(buf0.at[sl], r, ROW_CHUNKS),
                                  sems.at[sl]).start(priority=0)
            pltpu.make_async_copy(_tile_row(y_ref, i1_ref[0, 0, r], ROW_CHUNKS),
                                  _tile_row(buf1.at[sl], r, ROW_CHUNKS),
                                  sems.at[sl]).start(priority=1)
            return c

        lax.fori_loop(0, rows, issue, 0, unroll=8)

    @pl.when(i == 0)
    def _():
        start_gathers(d0_ref, d1_ref, slot)

    @pl.when(i + 1 < pl.num_programs(0))
    def _():
        start_gathers(n0_ref, n1_ref, 1 - slot)

    for buf in (buf0, buf1):
        pltpu.make_async_copy(y_ref.at[pl.ds(0, rows * ROW_CHUNKS)], buf.at[slot],
                              sems.at[slot]).wait()
    gates = r_ref[...]
    x3 = (x_ref[...] + gates[:, 2:3] * _load_tile_rows(buf0.at[slot], ROW_CHUNKS)
          + gates[:, 3:4] * _load_tile_rows(buf1.at[slot], ROW_CHUNKS))
    o_ref[...] = _rms(x3, g_ref[...])


def _combine_call(d0, d1, x2, router, g, y, rows):
    t, d = x2.shape
    d0, d1 = d0.reshape(t // rows, 1, rows), d1.reshape(t // rows, 1, rows)
    last = t // rows - 1
    idx = pl.BlockSpec((1, 1, rows), lambda i: (i, 0, 0), memory_space=pltpu.SMEM)
    nxt = pl.BlockSpec((1, 1, rows), lambda i: (jnp.minimum(i + 1, last), 0, 0),
                       memory_space=pltpu.SMEM)
    row = lambda i: (i, 0)
    return pl.pallas_call(
        _combine_kernel,
        grid=(t // rows,),
        in_specs=[idx, idx, nxt, nxt, pl.BlockSpec((rows, d), row),
                  pl.BlockSpec((rows, LANES), row), pl.BlockSpec(g.shape, lambda i: (0, 0)),
                  pl.BlockSpec(memory_space=pl.ANY)],
        out_specs=pl.BlockSpec((rows, d), row),
        out_shape=jax.ShapeDtypeStruct((t, d), F32),
        scratch_shapes=[pltpu.VMEM((2, rows * ROW_CHUNKS, LANES), F32),
                        pltpu.VMEM((2, rows * ROW_CHUNKS, LANES), F32),
                        pltpu.SemaphoreType.DMA((2,))],
        compiler_params=_params("arbitrary"),
        name="combine",
    )(d0, d1, d0, d1, x2, router, g, y)


def _slab(cols, lo):
    return jnp.pad(cols, ((0, 0), (lo, MLA_SLAB - lo - cols.shape[1])))


def _layer_weights(w_in, w_uq, w_ukv):
    half = MLA_ROPE_DIM // 2
    o = 3 * DIFF_WIDTH + MLA_Q_RANK + MLA_KV_RANK
    kpe1, kpe2 = w_in[:, o:o + half], w_in[:, o + half:o + 2 * half]
    w_all = jnp.concatenate(
        [w_in[:, :o],
         _slab(jnp.concatenate([kpe1, kpe2], axis=1), MLA_NOPE_DIM),
         _slab(jnp.concatenate([kpe2, kpe1], axis=1), MLA_NOPE_DIM)], axis=1).astype(BF16)
    q_main, q_swap, k_nope, v_cols = [], [], [], []
    for h in range(MLA_HEADS):
        c = w_uq[:, h * MLA_QK_DIM:(h + 1) * MLA_QK_DIM]
        pe1 = c[:, MLA_NOPE_DIM:MLA_NOPE_DIM + half]
        pe2 = c[:, MLA_NOPE_DIM + half:]
        q_main.append(_slab(c, 0))
        q_swap.append(_slab(jnp.concatenate([pe2, pe1], axis=1), MLA_NOPE_DIM))
        kv = w_ukv[:, h * (MLA_NOPE_DIM + MLA_V_DIM):(h + 1) * (MLA_NOPE_DIM + MLA_V_DIM)]
        k_nope.append(_slab(kv[:, :MLA_NOPE_DIM], 0))
        v_cols.append(kv[:, MLA_NOPE_DIM:])
    wq2 = jnp.concatenate(q_main + q_swap, axis=1).astype(BF16)
    wkv2 = jnp.concatenate(k_nope + v_cols, axis=1).astype(BF16)
    return w_all, wq2, wkv2


def _rope_rows():
    inv_freq = ROPE_THETA ** (-jnp.arange(0, MLA_ROPE_DIM, 2, dtype=F32) / MLA_ROPE_DIM)
    half = jnp.ones_like(inv_freq)
    place = lambda a, b: jnp.pad(jnp.concatenate([a, b]),
                                 (MLA_NOPE_DIM, MLA_SLAB - MLA_QK_DIM))
    return jnp.stack([place(inv_freq, inv_freq), place(-half, half)])


def kernel(x, mem, positions, norm_mix_g, w_in, diff_lambda_q1, diff_lambda_k1, diff_lambda_q2, diff_lambda_k2, diff_out_g, mla_q_norm_g, w_mla_uq, mla_kv_norm_g, w_mla_ukv, mla_out_g, w_out, norm_cross_g, norm_mem_g, w_mem_q, w_mem_kv, w_mem_o, norm_ffn_g, w_group_router, b_group_router, w_expert_router, b_expert_router, w_expert_gate, w_expert_up, w_expert_down, norm_final_g):
    b, s, d = x.shape
    t = b * s
    depth = w_in.shape[0]
    row2 = lambda v: v.reshape(1, -1).astype(F32)
    pos_f = positions.astype(F32)
    slopes = 2.0 ** (-8.0 * jnp.arange(1, DIFF_HEADS + 1, dtype=F32) / DIFF_HEADS)
    pos_rel = (pos_f - pos_f[:, :1]).reshape(t, 1)
    slope_rows = jnp.pad(jnp.repeat((LOG2E * slopes).reshape(DIFF_HEADS // 2, 2), 3, axis=1),
                         ((0, 0), (0, LANES - 6)))

    diff_tq, mla_tq = min(DIFF_TQ, s), min(MLA_TQ, s)
    attn_tk = min(ATTN_TK, s)
    blk = EXPERT_ROWS
    n_assign = 2 * t
    n_blocks = -(-(n_assign + N_EXPERTS * (blk - 1)) // blk)
    n_slots = n_blocks * blk

    assert depth == 1, "single-layer problem: the combine kernel applies the final RMSNorm"
    l = 0
    x2d = x.reshape(t, d)
    w_all, wq2, wkv2 = _layer_weights(w_in[l], w_mla_uq[l], w_mla_ukv[l])
    qd, kd, vdt, qm, km, vmt = _proj_call(
        x2d, row2(norm_mix_g[l]), w_all, row2(mla_q_norm_g[l]), wq2,
        row2(mla_kv_norm_g[l]), wkv2, pos_f.reshape(t, 1), _rope_rows(), pos_rel, slope_rows, s)

    lambda_init = 0.8 - 0.6 * math.exp(-0.3 * l)
    lam = (jnp.exp(jnp.sum(diff_lambda_q1[l] * diff_lambda_k1[l]))
           - jnp.exp(jnp.sum(diff_lambda_q2[l] * diff_lambda_k2[l])) + lambda_init)
    gain_col = jnp.tile(diff_out_g[l] * (1.0 - lambda_init), 2).reshape(-1, 1).astype(F32)
    shp = lambda a: a.reshape(b, s, a.shape[-1])
    diff_out = _diff_attn_call(lam.reshape(1).astype(F32), shp(qd), shp(kd), vdt, gain_col,
                               diff_tq, attn_tk)
    mla_out = _mla_attn_call(shp(qm), shp(km), vmt, mla_tq, attn_tk, MLA_HEADS_PER_STEP)

    kmem, vmem = _mem_kv_call(mem, row2(norm_mem_g[l]), w_mem_kv[l].astype(BF16))
    w_router = jnp.pad(
        jnp.concatenate([w_group_router[l], w_expert_router[l]], axis=1),
        ((0, 0), (0, LANES - N_GROUPS - N_EXPERTS)))
    w_router_hi = w_router.astype(BF16)
    w_router = jnp.stack([w_router_hi, (w_router - w_router_hi.astype(F32)).astype(BF16)])
    b_router = jnp.pad(jnp.concatenate([b_group_router[l], b_expert_router[l]]),
                       (0, LANES - N_GROUPS - N_EXPERTS))
    x2, h3, router, router_t, counts = _mix_call(
        x2d, diff_out.reshape(t, DIFF_WIDTH), mla_out.reshape(t, MLA_WIDTH),
        row2(mla_out_g[l]), w_out[l].astype(BF16), row2(norm_cross_g[l]),
        w_mem_q[l].astype(BF16), kmem, vmem, w_mem_o[l].astype(BF16),
        row2(norm_ffn_g[l]), w_router, row2(b_router), s)

    cnt = counts[0, ROUTER_LANE0:ROUTER_LANE0 + N_EXPERTS].astype(jnp.int32)
    padded = ((cnt + blk - 1) // blk) * blk
    seg_end = jnp.cumsum(padded)
    seg_start = seg_end - padded
    ridx = router_t.astype(jnp.int32)
    expert_ids = jnp.arange(N_EXPERTS, dtype=jnp.int32)[:, None]

    def slot_of(choice):
        start = jnp.sum(jnp.where(ridx[choice][None, :] == expert_ids, seg_start[:, None], 0),
                        axis=0)
        return start + ridx[4 + choice]

    dest0, dest1 = slot_of(0), slot_of(1)
    block_start = jnp.arange(n_blocks, dtype=jnp.int32) * blk
    block_expert = jnp.minimum(
        jnp.sum((seg_end[None, :] <= block_start[:, None]).astype(jnp.int32), axis=1),
        N_EXPERTS - 1)

    assert d == ROW_CHUNKS * LANES
    xs = _dispatch_call(dest0, dest1, h3, jnp.zeros((n_slots * ROW_CHUNKS, LANES), F32),
                        min(DISPATCH_ROWS, t))
    y = _expert_call(block_expert, xs, w_expert_gate[l], w_expert_up[l], w_expert_down[l], blk)
    out = _combine_call(dest0, dest1, x2, router, row2(norm_final_g), y, min(COMBINE_ROWS, t))
    return out.reshape(b, s, d)
```

```python
import functools
import math

import jax
import jax.numpy as jnp
from jax import lax
from jax.experimental import pallas as pl
from jax.experimental.pallas import tpu as pltpu

F32 = jnp.float32
BF16 = jnp.bfloat16

RMS_EPS = 1e-6
ROPE_THETA = 10000.0
MASK_VALUE = -1e30

DIFF_HEADS = 8
DIFF_HALF_DIM = 32
DIFF_V_DIM = 64
DIFF_WIDTH = DIFF_HEADS * DIFF_V_DIM

MLA_HEADS = 8
MLA_Q_RANK = 256
MLA_KV_RANK = 128
MLA_NOPE_DIM = 64
MLA_ROPE_DIM = 32
MLA_QK_DIM = MLA_NOPE_DIM + MLA_ROPE_DIM
MLA_V_DIM = 64
MLA_WIDTH = MLA_HEADS * MLA_V_DIM

MEM_HEADS = 4
MEM_HEAD_DIM = 128
MEM_WIDTH = MEM_HEADS * MEM_HEAD_DIM

N_GROUPS = 4
EXPERTS_PER_GROUP = 8
N_EXPERTS = N_GROUPS * EXPERTS_PER_GROUP
EXPERT_FF = 256

LANES = 128
MLA_SLAB = LANES
ROUTER_LANE0 = N_GROUPS
BF16_SUBLANES = 16
V_ONES_ROW = DIFF_V_DIM
V_ROWS = DIFF_V_DIM + BF16_SUBLANES
ROW_CHUNKS = 8

PROJ_ROWS = 512
MIX_ROWS = 1024
DISPATCH_ROWS = 1024
COMBINE_ROWS = 512
EXPERT_ROWS = 512
ATTN_TK = 512
DIFF_TQ = 512
MLA_TQ = 512
MLA_HEADS_PER_STEP = 4
LOG2E = math.log2(math.e)
VMEM_LIMIT = 48 * 1024 * 1024

_NT = (((1,), (1,)), ((), ()))


def _rms(x, gain):
    return x * lax.rsqrt(jnp.mean(x * x, axis=-1, keepdims=True) + RMS_EPS) * gain


def _dot(a, b):
    return jnp.dot(a, b, preferred_element_type=F32)


def _params(*sem):
    return pltpu.CompilerParams(dimension_semantics=sem, vmem_limit_bytes=VMEM_LIMIT)


def _values_transposed(v, heads):
    vt = v.T
    n = v.shape[0]
    tail = jnp.where(lax.broadcasted_iota(jnp.int32, (V_ROWS - V_ONES_ROW, n), 0) == 0, 1.0, 0.0)
    pieces = []
    for h in range(heads):
        pieces += [vt[h * V_ONES_ROW:(h + 1) * V_ONES_ROW], tail]
    return jnp.concatenate(pieces, axis=0).astype(BF16)


def _proj_kernel(x_ref, g_ref, w_ref, gq_ref, wq_ref, gkv_ref, wkv_ref, abs_ref, rope_ref,
                 pos_ref, slope_ref, qd_ref, kd_ref, vd_ref, qm_ref, km_ref, vm_ref):
    xn = _rms(x_ref[...], g_ref[...]).astype(BF16)
    r = _dot(xn, w_ref[...])
    w = DIFF_WIDTH
    qd_ref[...] = (r[:, 0:w] * (DIFF_HALF_DIM ** -0.5 * LOG2E)).astype(BF16)
    lane = lax.broadcasted_iota(jnp.int32, (x_ref.shape[0], LANES), 1)
    for p in range(DIFF_HEADS // 2):
        bias = pos_ref[...] * slope_ref[p:p + 1, :]
        hi = bias.astype(BF16)
        mid = (bias - hi.astype(F32)).astype(BF16)
        low = (bias - hi.astype(F32) - mid.astype(F32)).astype(BF16)
        term = jnp.where((lane == 0) | (lane == 3), hi, jnp.where((lane == 1) | (lane == 4), mid, low))
        kd_ref[:, 2 * p * LANES:(2 * p + 1) * LANES] = r[:, w + p * LANES:w + (p + 1) * LANES].astype(BF16)
        kd_ref[:, (2 * p + 1) * LANES:(2 * p + 2) * LANES] = term
    vd_ref[0] = _values_transposed(r[:, 2 * w:3 * w], DIFF_HEADS)
    o = 3 * w
    cq = r[:, o:o + MLA_Q_RANK]
    o += MLA_Q_RANK
    ckv = r[:, o:o + MLA_KV_RANK]
    o += MLA_KV_RANK
    ang = abs_ref[...] * rope_ref[0:1, :]
    cos = jnp.cos(ang)
    sin = jnp.sin(ang) * rope_ref[1:2, :]
    kpe = r[:, o:o + MLA_SLAB] * cos + r[:, o + MLA_SLAB:o + 2 * MLA_SLAB] * sin
    q2 = _dot(_rms(cq, gq_ref[...]).astype(BF16), wq_ref[...])
    kv2 = _dot(_rms(ckv, gkv_ref[...]).astype(BF16), wkv_ref[...])
    sw = MLA_HEADS * MLA_SLAB
    for h in range(MLA_HEADS):
        lo = h * MLA_SLAB
        qh = q2[:, lo:lo + MLA_SLAB] * cos + q2[:, sw + lo:sw + lo + MLA_SLAB] * sin
        qm_ref[:, lo:lo + MLA_SLAB] = (qh * (MLA_QK_DIM ** -0.5 * LOG2E)).astype(BF16)
        km_ref[:, lo:lo + MLA_SLAB] = (kv2[:, lo:lo + MLA_SLAB] + kpe).astype(BF16)
    vm_ref[0] = _values_transposed(kv2[:, sw:sw + MLA_WIDTH], MLA_HEADS)


def _proj_call(x2d, g, w_all, gq, wq2, gkv, wkv2, pos_abs, rope_rows, pos_rel, slope_rows, seq):
    t, d = x2d.shape
    rows = min(PROJ_ROWS, seq)
    per_seq = seq // rows
    row = lambda i: (i, 0)
    fixed = lambda i: (0, 0)
    full = lambda a: pl.BlockSpec(a.shape, fixed)
    row_out = lambda n: (pl.BlockSpec((rows, n), row), jax.ShapeDtypeStruct((t, n), BF16))
    vt_out = lambda n: (pl.BlockSpec((1, n, rows), lambda i: (i // per_seq, 0, i % per_seq)),
                        jax.ShapeDtypeStruct((t // seq, n, seq), BF16))
    outs = [row_out(DIFF_WIDTH), row_out(DIFF_HEADS * LANES), vt_out(DIFF_HEADS * V_ROWS),
            row_out(MLA_HEADS * MLA_SLAB), row_out(MLA_HEADS * MLA_SLAB),
            vt_out(MLA_HEADS * V_ROWS)]
    return pl.pallas_call(
        _proj_kernel,
        grid=(t // rows,),
        in_specs=[pl.BlockSpec((rows, d), row), full(g), full(w_all), full(gq), full(wq2),
                  full(gkv), full(wkv2), pl.BlockSpec((rows, 1), row), full(rope_rows),
                  pl.BlockSpec((rows, 1), row), full(slope_rows)],
        out_specs=[spec for spec, _ in outs],
        out_shape=[shape for _, shape in outs],
        compiler_params=_params("parallel"),
        name="proj",
    )(x2d, g, w_all, gq, wq2, gkv, wkv2, pos_abs, rope_rows, pos_rel, slope_rows)


def _softmax_step(s, col_max, vt_chunk, parts, m, *accs):
    m_new = jnp.maximum(m, col_max)
    alpha = jnp.exp2(m - m_new)
    pb = jnp.exp2(s - m_new).astype(BF16)
    accs = tuple(alpha[:, cols] * acc + _dot(vt_chunk[rows], pb[:, cols])
                 for (rows, cols), acc in zip(parts, accs))
    return (m_new,) + accs


def _normalised(acc):
    return acc[:V_ONES_ROW] * (1.0 / acc[V_ONES_ROW:V_ONES_ROW + 1])


def _causal_mask(s, key0, query0, tq):
    key = key0 + lax.broadcasted_iota(jnp.int32, s.shape, 0)
    query = query0 + (lax.broadcasted_iota(jnp.int32, s.shape, 1) & (tq - 1))
    return jnp.where(key <= query, s, MASK_VALUE)


def _causal_sweep(scores, consume, carry, n_full, slot_a, slot_b):
    scores(0, slot_a)

    def pair(i, c):
        j = 2 * i
        scores(j + 1, slot_b)
        c = consume(j, slot_a, c, False)
        scores(j + 2, slot_a)
        return consume(j + 1, slot_b, c, False)

    carry = lax.fori_loop(0, lax.shift_right_logical(n_full, 1), pair, carry)

    def odd_tail(c):
        scores(n_full, slot_b)
        c = consume(n_full - 1, slot_a, c, False)
        return consume(n_full, slot_b, c, True)

    def even_tail(c):
        return consume(n_full, slot_a, c, True)

    return lax.cond((n_full & 1) == 1, odd_tail, even_tail, carry)


def _diff_attn_kernel(lam_ref, q_ref, k_ref, vt_ref, g_ref, o_ref, qs_ref, sa_ref, sb_ref,
                      ma_ref, mb_ref, *, tq, tk):
    qi = pl.program_id(2)
    q = q_ref[0]
    lane = lax.broadcasted_iota(jnp.int32, (tq, LANES), 1)
    for v in range(4):
        keep = (lane >= v * DIFF_HALF_DIM) & (lane < (v + 1) * DIFF_HALF_DIM)
        ones = (lane >= 3 * (v // 2)) & (lane < 3 * (v // 2) + 3)
        qs_ref[v * tq:(v + 1) * tq, 0:LANES] = jnp.where(keep, q, jnp.zeros_like(q))
        qs_ref[v * tq:(v + 1) * tq, LANES:2 * LANES] = jnp.where(ones, 1.0, 0.0).astype(BF16)

    def scores(j, slot):
        start = pl.multiple_of(j * tk, tk)
        s = lax.dot_general(k_ref[0, pl.ds(start, tk), :], qs_ref[...], _NT,
                            preferred_element_type=F32)
        slot[0][...] = s
        slot[1][...] = jnp.max(s, axis=0, keepdims=True)

    def consume(j, slot, carry, masked):
        start = pl.multiple_of(j * tk, tk)
        s = slot[0][...]
        if masked:
            s = _causal_mask(s, start, qi * tq, tq)
            col_max = jnp.max(s, axis=0, keepdims=True)
        else:
            col_max = slot[1][...]
        return _softmax_step(s, col_max, vt_ref[0, :, pl.ds(start, tk)], parts, *carry)

    parts = [(slice(h * V_ROWS, (h + 1) * V_ROWS), slice(2 * h * tq, (2 * h + 2) * tq))
             for h in range(2)]
    init = (jnp.full((1, 4 * tq), MASK_VALUE, F32),
            jnp.zeros((V_ROWS, 2 * tq), F32), jnp.zeros((V_ROWS, 2 * tq), F32))
    _, *accs = _causal_sweep(scores, consume, init, (qi * tq) // tk,
                             (sa_ref, ma_ref), (sb_ref, mb_ref))

    lam = lam_ref[0]
    halves = []
    for acc in accs:
        on = _normalised(acc)
        o = on[:, :tq] - lam * on[:, tq:]
        halves.append(o * lax.rsqrt(jnp.mean(o * o, axis=0, keepdims=True) + RMS_EPS))
    o_ref[0] = (jnp.concatenate(halves, axis=0) * g_ref[...]).T.astype(o_ref.dtype)


def _diff_attn_call(lam, qd, kd, vdt, g_col, tq, tk):
    b, s, _ = qd.shape
    pairs = DIFF_HEADS // 2
    return pl.pallas_call(
        functools.partial(_diff_attn_kernel, tq=tq, tk=tk),
        grid=(b, pairs, s // tq),
        in_specs=[pl.BlockSpec(memory_space=pltpu.SMEM),
                  pl.BlockSpec((1, tq, LANES), lambda bi, p, qi: (bi, qi, p)),
                  pl.BlockSpec((1, s, 2 * LANES), lambda bi, p, qi: (bi, 0, p)),
                  pl.BlockSpec((1, 2 * V_ROWS, s), lambda bi, p, qi: (bi, p, 0)),
                  pl.BlockSpec((LANES, 1), lambda bi, p, qi: (0, 0))],
        out_specs=pl.BlockSpec((1, tq, LANES), lambda bi, p, qi: (bi, qi, p)),
        out_shape=jax.ShapeDtypeStruct((b, s, DIFF_WIDTH), BF16),
        scratch_shapes=[pltpu.VMEM((4 * tq, 2 * LANES), BF16), pltpu.VMEM((tk, 4 * tq), F32),
                        pltpu.VMEM((tk, 4 * tq), F32), pltpu.VMEM((1, 4 * tq), F32),
                        pltpu.VMEM((1, 4 * tq), F32)],
        compiler_params=_params("parallel", "parallel", "arbitrary"),
        name="diff_attn",
    )(lam, qd, kd, vdt, g_col)


def _mla_attn_kernel(q_ref, k_ref, vt_ref, o_ref, sa_ref, sb_ref, ma_ref, mb_ref, *, tq, tk,
                     heads):
    qi = pl.program_id(2)
    slabs = [slice(hh * MLA_SLAB, (hh + 1) * MLA_SLAB) for hh in range(heads)]

    def scores(j, slot):
        start = pl.multiple_of(j * tk, tk)
        for hh, sl in enumerate(slabs):
            s = lax.dot_general(k_ref[0, pl.ds(start, tk), sl], q_ref[0, :, sl], _NT,
                                preferred_element_type=F32)
            slot[0][hh] = s
            slot[1][hh] = jnp.max(s, axis=0, keepdims=True)

    def consume(j, slot, carry, masked):
        start = pl.multiple_of(j * tk, tk)
        vtc = vt_ref[0, :, pl.ds(start, tk)]
        out = []
        for hh in range(heads):
            s = slot[0][hh]
            if masked:
                s = _causal_mask(s, start, qi * tq, tq)
                col_max = jnp.max(s, axis=0, keepdims=True)
            else:
                col_max = slot[1][hh]
            rows = slice(hh * V_ROWS, (hh + 1) * V_ROWS)
            out.extend(_softmax_step(s, col_max, vtc, [(rows, slice(None))],
                                     *carry[2 * hh:2 * hh + 2]))
        return tuple(out)

    init = (jnp.full((1, tq), MASK_VALUE, F32), jnp.zeros((V_ROWS, tq), F32)) * heads
    carry = _causal_sweep(scores, consume, init, (qi * tq) // tk,
                          (sa_ref, ma_ref), (sb_ref, mb_ref))
    o = jnp.concatenate([_normalised(carry[2 * hh + 1]) for hh in range(heads)], axis=0)
    o_ref[0] = o.T.astype(o_ref.dtype)


def _mla_attn_call(qm, km, vmt, tq, tk, heads):
    b, s, _ = qm.shape
    groups = MLA_HEADS // heads
    return pl.pallas_call(
        functools.partial(_mla_attn_kernel, tq=tq, tk=tk, heads=heads),
        grid=(b, groups, s // tq),
        in_specs=[pl.BlockSpec((1, tq, heads * MLA_SLAB), lambda bi, p, qi: (bi, qi, p)),
                  pl.BlockSpec((1, s, heads * MLA_SLAB), lambda bi, p, qi: (bi, 0, p)),
                  pl.BlockSpec((1, heads * V_ROWS, s), lambda bi, p, qi: (bi, p, 0))],
        out_specs=pl.BlockSpec((1, tq, heads * MLA_V_DIM), lambda bi, p, qi: (bi, qi, p)),
        out_shape=jax.ShapeDtypeStruct((b, s, MLA_WIDTH), BF16),
        scratch_shapes=[pltpu.VMEM((heads, tk, tq), F32), pltpu.VMEM((heads, tk, tq), F32),
                        pltpu.VMEM((heads, 1, tq), F32), pltpu.VMEM((heads, 1, tq), F32)],
        compiler_params=_params("parallel", "parallel", "arbitrary"),
        name="mla_attn",
    )(qm, km, vmt)


def _mem_kv_kernel(mem_ref, g_ref, w_ref, k_ref, v_ref):
    kv = _dot(_rms(mem_ref[0], g_ref[...]).astype(BF16), w_ref[...])
    k_ref[0] = kv[:, :MEM_WIDTH].astype(BF16)
    v_ref[0] = kv[:, MEM_WIDTH:].astype(BF16)


def _mem_kv_call(mem, g, w_kv):
    b, n_mem, d = mem.shape
    blk = pl.BlockSpec((1, n_mem, MEM_WIDTH), lambda bi: (bi, 0, 0))
    return pl.pallas_call(
        _mem_kv_kernel,
        grid=(b,),
        in_specs=[pl.BlockSpec((1, n_mem, d), lambda bi: (bi, 0, 0)),
                  pl.BlockSpec(g.shape, lambda bi: (0, 0)),
                  pl.BlockSpec(w_kv.shape, lambda bi: (0, 0))],
        out_specs=[blk, blk],
        out_shape=[jax.ShapeDtypeStruct((b, n_mem, MEM_WIDTH), BF16)] * 2,
        compiler_params=_params("parallel"),
        name="mem_kv",
    )(mem, g, w_kv)


def _store_tile_rows(ref, x):
    n = x.shape[0]
    chunks = x.shape[1] // LANES
    for c in range(chunks):
        ref[pl.ds(c, n, stride=chunks), :] = x[:, c * LANES:(c + 1) * LANES]


def _load_tile_rows(ref, chunks):
    n = ref.shape[0] // chunks
    return jnp.concatenate([ref[pl.ds(c, n, stride=chunks), :] for c in range(chunks)], axis=1)


def _tile_row(ref, r, chunks):
    return ref.at[pl.ds(pl.multiple_of(r * chunks, chunks), chunks)]


def _mix_kernel(x_ref, d_ref, m_ref, gm_ref, wo_ref, gc_ref, wq_ref, km_ref, vm_ref, wmo_ref,
                gf_ref, wr_ref, br_ref, x2_ref, h3_ref, r_ref, rt_ref, cnt_ref, run_ref):
    rows = x_ref.shape[0]

    @pl.when(pl.program_id(0) == 0)
    def _():
        run_ref[...] = jnp.zeros_like(run_ref)

    mla_n = _rms(m_ref[...].astype(F32), gm_ref[...]).astype(BF16)
    x1 = (x_ref[...] + _dot(d_ref[...], wo_ref[0:DIFF_WIDTH, :])
          + _dot(mla_n, wo_ref[DIFF_WIDTH:DIFF_WIDTH + MLA_WIDTH, :]))

    qx = _dot(_rms(x1, gc_ref[...]).astype(BF16), wq_ref[...])
    heads = []
    for h in range(MEM_HEADS):
        sl = slice(h * MEM_HEAD_DIM, (h + 1) * MEM_HEAD_DIM)
        sc = lax.dot_general(qx[:, sl].astype(BF16), km_ref[0, :, sl], _NT,
                             preferred_element_type=F32) * MEM_HEAD_DIM ** -0.5
        p = jnp.exp(sc - jnp.max(sc, axis=-1, keepdims=True))
        o = _dot(p.astype(BF16), vm_ref[0, :, sl]) * (1.0 / jnp.sum(p, axis=-1, keepdims=True))
        heads.append(o.astype(BF16))
    x2 = x1 + _dot(jnp.concatenate(heads, axis=1), wmo_ref[...])
    x2_ref[...] = x2

    h3 = _rms(x2, gf_ref[...])
    _store_tile_rows(h3_ref, h3)
    h_hi = h3.astype(BF16)
    h_lo = (h3 - h_hi.astype(F32)).astype(BF16)
    logits = (_dot(h_hi, wr_ref[0]) + _dot(h_lo, wr_ref[0]) + _dot(h_hi, wr_ref[1])
              + br_ref[...])
    lane = lax.broadcasted_iota(jnp.int32, (rows, LANES), 1).astype(F32)
    far = float(4 * LANES)

    def first_argmax(vals):
        top = jnp.max(vals, axis=-1, keepdims=True)
        return top, jnp.min(jnp.where(vals == top, lane, far), axis=-1, keepdims=True)

    g_logits = jnp.where(lane < N_GROUPS, logits, MASK_VALUE)
    g_top, g_sel = first_argmax(g_logits)
    g_gate = 1.0 / jnp.sum(jnp.exp(g_logits - g_top), axis=-1, keepdims=True)
    lo = ROUTER_LANE0 + EXPERTS_PER_GROUP * g_sel
    e_logits = jnp.where(lane >= lo, jnp.where(lane < lo + EXPERTS_PER_GROUP, logits, MASK_VALUE),
                         MASK_VALUE)
    top1, lane1 = first_argmax(e_logits)
    top2, lane2 = first_argmax(jnp.where(lane == lane1, MASK_VALUE, e_logits))
    t = jnp.exp(top2 - top1)
    gate1 = g_gate / (1.0 + t)
    gate2 = gate1 * t

    hot = jnp.where(lane == lane1, 1.0, 0.0) + jnp.where(lane == lane2, 1.0, 0.0)
    tri = (lax.broadcasted_iota(jnp.int32, (rows, rows), 1)
           < lax.broadcasted_iota(jnp.int32, (rows, rows), 0))
    before = _dot(jnp.where(tri, 1.0, 0.0).astype(BF16), hot.astype(BF16)) + run_ref[0:1, :]
    rank1 = jnp.sum(jnp.where(lane == lane1, before, 0.0), axis=-1, keepdims=True)
    rank2 = jnp.sum(jnp.where(lane == lane2, before, 0.0), axis=-1, keepdims=True)
    run_ref[...] = run_ref[...] + jnp.sum(hot, axis=0, keepdims=True)
    cnt_ref[...] = run_ref[...]

    fields = (lane1 - ROUTER_LANE0, lane2 - ROUTER_LANE0, gate1, gate2, rank1, rank2)
    out = jnp.zeros((rows, LANES), F32)
    for i, f in enumerate(fields):
        out = jnp.where(lane == float(i), f, out)
    r_ref[...] = out
    rt_ref[...] = out.T[0:8, :]


def _mix_call(x2d, diff, mla, gm, wo, gc, wq, kmem, vmem, wmo, gf, wr, br, seq):
    t, d = x2d.shape
    rows = min(MIX_ROWS, seq)
    per_seq = seq // rows
    row = lambda i: (i, 0)
    fixed = lambda i: (0, 0)
    full = lambda a: pl.BlockSpec(a.shape, lambda i: (0,) * a.ndim)
    memblk = pl.BlockSpec((1,) + kmem.shape[1:], lambda i: (i // per_seq, 0, 0))
    return pl.pallas_call(
        _mix_kernel,
        grid=(t // rows,),
        in_specs=[pl.BlockSpec((rows, d), row), pl.BlockSpec((rows, DIFF_WIDTH), row),
                  pl.BlockSpec((rows, MLA_WIDTH), row), full(gm), full(wo), full(gc), full(wq),
                  memblk, memblk, full(wmo), full(gf), full(wr), full(br)],
        out_specs=[pl.BlockSpec((rows, d), row), pl.BlockSpec((rows * d // LANES, LANES), row),
                   pl.BlockSpec((rows, LANES), row), pl.BlockSpec((8, rows), lambda i: (0, i)),
                   pl.BlockSpec((8, LANES), fixed)],
        out_shape=[jax.ShapeDtypeStruct((t, d), F32),
                   jax.ShapeDtypeStruct((t * d // LANES, LANES), F32),
                   jax.ShapeDtypeStruct((t, LANES), F32), jax.ShapeDtypeStruct((8, t), F32),
                   jax.ShapeDtypeStruct((8, LANES), F32)],
        scratch_shapes=[pltpu.VMEM((8, LANES), F32)],
        compiler_params=_params("arbitrary"),
        name="mix",
    )(x2d, diff, mla, gm, wo, gc, wq, kmem, vmem, wmo, gf, wr, br)


def _dispatch_kernel(d0_ref, d1_ref, h_ref, xs_in_ref, xs_ref, sem):
    del xs_in_ref
    rows = h_ref.shape[0] // ROW_CHUNKS

    def issue(r, c):
        src = _tile_row(h_ref, r, ROW_CHUNKS)
        pltpu.make_async_copy(src, _tile_row(xs_ref, d0_ref[0, 0, r], ROW_CHUNKS), sem).start(
            priority=0)
        pltpu.make_async_copy(src, _tile_row(xs_ref, d1_ref[0, 0, r], ROW_CHUNKS), sem).start(
            priority=1)
        return c

    lax.fori_loop(0, rows, issue, 0, unroll=8)
    for _ in range(2):
        pltpu.make_async_copy(h_ref, xs_ref.at[pl.ds(0, rows * ROW_CHUNKS)], sem).wait()


def _dispatch_call(d0, d1, h3, xs0, rows):
    t = h3.shape[0] // ROW_CHUNKS
    d0, d1 = d0.reshape(t // rows, 1, rows), d1.reshape(t // rows, 1, rows)
    idx = pl.BlockSpec((1, 1, rows), lambda i: (i, 0, 0), memory_space=pltpu.SMEM)
    return pl.pallas_call(
        _dispatch_kernel,
        grid=(t // rows,),
        in_specs=[idx, idx, pl.BlockSpec((rows * ROW_CHUNKS, LANES), lambda i: (i, 0)),
                  pl.BlockSpec(memory_space=pl.ANY)],
        out_specs=pl.BlockSpec(memory_space=pl.ANY),
        out_shape=jax.ShapeDtypeStruct(xs0.shape, xs0.dtype),
        scratch_shapes=[pltpu.SemaphoreType.DMA(())],
        input_output_aliases={3: 0},
        compiler_params=_params("arbitrary"),
        name="dispatch",
    )(d0, d1, h3, xs0)


def _expert_kernel(be_ref, x_ref, wg_ref, wu_ref, wd_ref, y_ref, wg_bf, wu_bf, wd_bf):
    i = pl.program_id(0)

    @pl.when((i == 0) | (be_ref[i] != be_ref[jnp.maximum(i - 1, 0)]))
    def _():
        wg_bf[...] = wg_ref[0].astype(BF16)
        wu_bf[...] = wu_ref[0].astype(BF16)
        wd_bf[...] = wd_ref[0].astype(BF16)

    x = _load_tile_rows(x_ref, ROW_CHUNKS).astype(BF16)
    gate = _dot(x, wg_bf[...])
    up = _dot(x, wu_bf[...])
    hidden = gate * (1.0 / (1.0 + jnp.exp(-gate))) * up
    _store_tile_rows(y_ref, _dot(hidden.astype(BF16), wd_bf[...]))


def _expert_call(block_expert, xs, wg, wu, wd, rows):
    _, d, ff = wg.shape
    n_slots = xs.shape[0] // ROW_CHUNKS
    grid_spec = pltpu.PrefetchScalarGridSpec(
        num_scalar_prefetch=1,
        grid=(n_slots // rows,),
        in_specs=[pl.BlockSpec((rows * ROW_CHUNKS, LANES), lambda i, be: (i, 0)),
                  pl.BlockSpec((1, d, ff), lambda i, be: (be[i], 0, 0)),
                  pl.BlockSpec((1, d, ff), lambda i, be: (be[i], 0, 0)),
                  pl.BlockSpec((1, ff, d), lambda i, be: (be[i], 0, 0))],
        out_specs=pl.BlockSpec((rows * ROW_CHUNKS, LANES), lambda i, be: (i, 0)),
        scratch_shapes=[pltpu.VMEM((d, ff), BF16), pltpu.VMEM((d, ff), BF16),
                        pltpu.VMEM((ff, d), BF16)],
    )
    return pl.pallas_call(
        _expert_kernel,
        grid_spec=grid_spec,
        out_shape=jax.ShapeDtypeStruct(xs.shape, F32),
        compiler_params=_params("arbitrary"),
        name="experts",
    )(block_expert, xs, wg, wu, wd)


def _combine_kernel(d0_ref, d1_ref, n0_ref, n1_ref, x_ref, r_ref, g_ref, y_ref, o_ref, buf0, buf1,
                    sems):
    rows = x_ref.shape[0]
    i = pl.program_id(0)
    slot = i & 1

    def start_gathers(i0_ref, i1_ref, sl):
        def issue(r, c):
            pltpu.make_async_copy(_tile_row(y_ref, i0_ref[0, 0, r], ROW_CHUNKS),
                                  _tile_row(buf0.at[sl], r, ROW_CHUNKS),
                                  sems.at[sl]).start(priority=0)
            pltpu.make_async_copy(_tile_row(y_ref, i1_ref[0, 0, r], ROW_CHUNKS),
                                  _tile_row(buf1.at[sl], r, ROW_CHUNKS),
                                  sems.at[sl]).start(priority=1)
            return c

        lax.fori_loop(0, rows, issue, 0, unroll=8)

    @pl.when(i == 0)
    def _():
        start_gathers(d0_ref, d1_ref, slot)

    @pl.when(i + 1 < pl.num_programs(0))
    def _():
        start_gathers(n0_ref, n1_ref, 1 - slot)

    for buf in (buf0, buf1):
        pltpu.make_async_copy(y_ref.at[pl.ds(0, rows * ROW_CHUNKS)], buf.at[slot],
                              sems.at[slot]).wait()
    gates = r_ref[...]
    x3 = (x_ref[...] + gates[:, 2:3] * _load_tile_rows(buf0.at[slot], ROW_CHUNKS)
          + gates[:, 3:4] * _load_tile_rows(buf1.at[slot], ROW_CHUNKS))
    o_ref[...] = _rms(x3, g_ref[...])


def _combine_call(d0, d1, x2, router, g, y, rows):
    t, d = x2.shape
    d0, d1 = d0.reshape(t // rows, 1, rows), d1.reshape(t // rows, 1, rows)
    last = t // rows - 1
    idx = pl.BlockSpec((1, 1, rows), lambda i: (i, 0, 0), memory_space=pltpu.SMEM)
    nxt = pl.BlockSpec((1, 1, rows), lambda i: (jnp.minimum(i + 1, last), 0, 0),
                       memory_space=pltpu.SMEM)
    row = lambda i: (i, 0)
    return pl.pallas_call(
        _combine_kernel,
        grid=(t // rows,),
        in_specs=[idx, idx, nxt, nxt, pl.BlockSpec((rows, d), row),
                  pl.BlockSpec((rows, LANES), row), pl.BlockSpec(g.shape, lambda i: (0, 0)),
                  pl.BlockSpec(memory_space=pl.ANY)],
        out_specs=pl.BlockSpec((rows, d), row),
        out_shape=jax.ShapeDtypeStruct((t, d), F32),
        scratch_shapes=[pltpu.VMEM((2, rows * ROW_CHUNKS, LANES), F32),
                        pltpu.VMEM((2, rows * ROW_CHUNKS, LANES), F32),
                        pltpu.SemaphoreType.DMA((2,))],
        compiler_params=_params("arbitrary"),
        name="combine",
    )(d0, d1, d0, d1, x2, router, g, y)


def _slab(cols, lo):
    return jnp.pad(cols, ((0, 0), (lo, MLA_SLAB - lo - cols.shape[1])))


def _layer_weights(w_in, w_uq, w_ukv):
    half = MLA_ROPE_DIM // 2
    o = 3 * DIFF_WIDTH + MLA_Q_RANK + MLA_KV_RANK
    kpe1, kpe2 = w_in[:, o:o + half], w_in[:, o + half:o + 2 * half]
    w_all = jnp.concatenate(
        [w_in[:, :o],
         _slab(jnp.concatenate([kpe1, kpe2], axis=1), MLA_NOPE_DIM),
         _slab(jnp.concatenate([kpe2, kpe1], axis=1), MLA_NOPE_DIM)], axis=1).astype(BF16)
    q_main, q_swap, k_nope, v_cols = [], [], [], []
    for h in range(MLA_HEADS):
        c = w_uq[:, h * MLA_QK_DIM:(h + 1) * MLA_QK_DIM]
        pe1 = c[:, MLA_NOPE_DIM:MLA_NOPE_DIM + half]
        pe2 = c[:, MLA_NOPE_DIM + half:]
        q_main.append(_slab(c, 0))
        q_swap.append(_slab(jnp.concatenate([pe2, pe1], axis=1), MLA_NOPE_DIM))
        kv = w_ukv[:, h * (MLA_NOPE_DIM + MLA_V_DIM):(h + 1) * (MLA_NOPE_DIM + MLA_V_DIM)]
        k_nope.append(_slab(kv[:, :MLA_NOPE_DIM], 0))
        v_cols.append(kv[:, MLA_NOPE_DIM:])
    wq2 = jnp.concatenate(q_main + q_swap, axis=1).astype(BF16)
    wkv2 = jnp.concatenate(k_nope + v_cols, axis=1).astype(BF16)
    return w_all, wq2, wkv2


def _rope_rows():
    inv_freq = ROPE_THETA ** (-jnp.arange(0, MLA_ROPE_DIM, 2, dtype=F32) / MLA_ROPE_DIM)
    half = jnp.ones_like(inv_freq)
    place = lambda a, b: jnp.pad(jnp.concatenate([a, b]),
                                 (MLA_NOPE_DIM, MLA_SLAB - MLA_QK_DIM))
    return jnp.stack([place(inv_freq, inv_freq), place(-half, half)])


def kernel(x, mem, positions, norm_mix_g, w_in, diff_lambda_q1, diff_lambda_k1, diff_lambda_q2, diff_lambda_k2, diff_out_g, mla_q_norm_g, w_mla_uq, mla_kv_norm_g, w_mla_ukv, mla_out_g, w_out, norm_cross_g, norm_mem_g, w_mem_q, w_mem_kv, w_mem_o, norm_ffn_g, w_group_router, b_group_router, w_expert_router, b_expert_router, w_expert_gate, w_expert_up, w_expert_down, norm_final_g):
    b, s, d = x.shape
    t = b * s
    depth = w_in.shape[0]
    row2 = lambda v: v.reshape(1, -1).astype(F32)
    pos_f = positions.astype(F32)
    slopes = 2.0 ** (-8.0 * jnp.arange(1, DIFF_HEADS + 1, dtype=F32) / DIFF_HEADS)
    pos_rel = (pos_f - pos_f[:, :1]).reshape(t, 1)
    slope_rows = jnp.pad(jnp.repeat((LOG2E * slopes).reshape(DIFF_HEADS // 2, 2), 3, axis=1),
                         ((0, 0), (0, LANES - 6)))

    diff_tq, mla_tq = min(DIFF_TQ, s), min(MLA_TQ, s)
    attn_tk = min(ATTN_TK, s)
    blk = EXPERT_ROWS
    n_assign = 2 * t
    n_blocks = -(-(n_assign + N_EXPERTS * (blk - 1)) // blk)
    n_slots = n_blocks * blk

    assert depth == 1, "single-layer problem: the combine kernel applies the final RMSNorm"
    l = 0
    x2d = x.reshape(t, d)
    w_all, wq2, wkv2 = _layer_weights(w_in[l], w_mla_uq[l], w_mla_ukv[l])
    qd, kd, vdt, qm, km, vmt = _proj_call(
        x2d, row2(norm_mix_g[l]), w_all, row2(mla_q_norm_g[l]), wq2,
        row2(mla_kv_norm_g[l]), wkv2, pos_f.reshape(t, 1), _rope_rows(), pos_rel, slope_rows, s)

    lambda_init = 0.8 - 0.6 * math.exp(-0.3 * l)
    lam = (jnp.exp(jnp.sum(diff_lambda_q1[l] * diff_lambda_k1[l]))
           - jnp.exp(jnp.sum(diff_lambda_q2[l] * diff_lambda_k2[l])) + lambda_init)
    gain_col = jnp.tile(diff_out_g[l] * (1.0 - lambda_init), 2).reshape(-1, 1).astype(F32)
    shp = lambda a: a.reshape(b, s, a.shape[-1])
    diff_out = _diff_attn_call(lam.reshape(1).astype(F32), shp(qd), shp(kd), vdt, gain_col,
                               diff_tq, attn_tk)
    mla_out = _mla_attn_call(shp(qm), shp(km), vmt, mla_tq, attn_tk, MLA_HEADS_PER_STEP)

    kmem, vmem = _mem_kv_call(mem, row2(norm_mem_g[l]), w_mem_kv[l].astype(BF16))
    w_router = jnp.pad(
        jnp.concatenate([w_group_router[l], w_expert_router[l]], axis=1),
        ((0, 0), (0, LANES - N_GROUPS - N_EXPERTS)))
    w_router_hi = w_router.astype(BF16)
    w_router = jnp.stack([w_router_hi, (w_router - w_router_hi.astype(F32)).astype(BF16)])
    b_router = jnp.pad(jnp.concatenate([b_group_router[l], b_expert_router[l]]),
                       (0, LANES - N_GROUPS - N_EXPERTS))
    x2, h3, router, router_t, counts = _mix_call(
        x2d, diff_out.reshape(t, DIFF_WIDTH), mla_out.reshape(t, MLA_WIDTH),
        row2(mla_out_g[l]), w_out[l].astype(BF16), row2(norm_cross_g[l]),
        w_mem_q[l].astype(BF16), kmem, vmem, w_mem_o[l].astype(BF16),
        row2(norm_ffn_g[l]), w_router, row2(b_router), s)

    cnt = counts[0, ROUTER_LANE0:ROUTER_LANE0 + N_EXPERTS].astype(jnp.int32)
    padded = ((cnt + blk - 1) // blk) * blk
    seg_end = jnp.cumsum(padded)
    seg_start = seg_end - padded
    ridx = router_t.astype(jnp.int32)
    expert_ids = jnp.arange(N_EXPERTS, dtype=jnp.int32)[:, None]

    def slot_of(choice):
        start = jnp.sum(jnp.where(ridx[choice][None, :] == expert_ids, seg_start[:, None], 0),
                        axis=0)
        return start + ridx[4 + choice]

    dest0, dest1 = slot_of(0), slot_of(1)
    block_start = jnp.arange(n_blocks, dtype=jnp.int32) * blk
    block_expert = jnp.minimum(
        jnp.sum((seg_end[None, :] <= block_start[:, None]).astype(jnp.int32), axis=1),
        N_EXPERTS - 1)

    assert d == ROW_CHUNKS * LANES
    xs = _dispatch_call(dest0, dest1, h3, jnp.zeros((n_slots * ROW_CHUNKS, LANES), F32),
                        min(DISPATCH_ROWS, t))
    y = _expert_call(block_expert, xs, w_expert_gate[l], w_expert_up[l], w_expert_down[l], blk)
    out = _combine_call(dest0, dest1, x2, router, row2(norm_final_g), y, min(COMBINE_ROWS, t))
    return out.reshape(b, s, d)
```

```python
import functools
import math

import jax
import jax.numpy as jnp
from jax import lax
from jax.experimental import pallas as pl
from jax.experimental.pallas import tpu as pltpu

F32 = jnp.float32
BF16 = jnp.bfloat16

RMS_EPS = 1e-6
ROPE_THETA = 10000.0
MASK_VALUE = -1e30

DIFF_HEADS = 8
DIFF_HALF_DIM = 32
DIFF_V_DIM = 64
DIFF_WIDTH = DIFF_HEADS * DIFF_V_DIM

MLA_HEADS = 8
MLA_Q_RANK = 256
MLA_KV_RANK = 128
MLA_NOPE_DIM = 64
MLA_ROPE_DIM = 32
MLA_QK_DIM = MLA_NOPE_DIM + MLA_ROPE_DIM
MLA_V_DIM = 64
MLA_WIDTH = MLA_HEADS * MLA_V_DIM

MEM_HEADS = 4
MEM_HEAD_DIM = 128
MEM_WIDTH = MEM_HEADS * MEM_HEAD_DIM

N_GROUPS = 4
EXPERTS_PER_GROUP = 8
N_EXPERTS = N_GROUPS * EXPERTS_PER_GROUP
EXPERT_FF = 256

LANES = 128
MLA_SLAB = LANES
ROUTER_LANE0 = N_GROUPS
BF16_SUBLANES = 16
V_ONES_ROW = DIFF_V_DIM
V_ROWS = DIFF_V_DIM + BF16_SUBLANES
ROW_CHUNKS = 8

PROJ_ROWS = 512
MIX_ROWS = 1024
DISPATCH_ROWS = 1024
COMBINE_ROWS = 256
EXPERT_ROWS = 512
ATTN_TK = 512
DIFF_TQ = 512
MLA_TQ = 512
MLA_HEADS_PER_STEP = 4
LOG2E = math.log2(math.e)
V7X_VMEM_BYTES = 64 * 1024 * 1024
VMEM_LIMIT = V7X_VMEM_BYTES * 3 // 4

_NT = (((1,), (1,)), ((), ()))


def _rms(x, gain):
    return x * lax.rsqrt(jnp.mean(x * x, axis=-1, keepdims=True) + RMS_EPS) * gain


def _dot(a, b):
    return jnp.dot(a, b, preferred_element_type=F32)


def _params(*sem):
    return pltpu.CompilerParams(dimension_semantics=sem, vmem_limit_bytes=VMEM_LIMIT)


def _values_transposed(v, heads):
    vt = v.T
    n = v.shape[0]
    tail = jnp.where(lax.broadcasted_iota(jnp.int32, (V_ROWS - V_ONES_ROW, n), 0) == 0, 1.0, 0.0)
    pieces = []
    for h in range(heads):
        pieces += [vt[h * V_ONES_ROW:(h + 1) * V_ONES_ROW], tail]
    return jnp.concatenate(pieces, axis=0).astype(BF16)


def _proj_kernel(x_ref, g_ref, w_ref, gq_ref, wq_ref, gkv_ref, wkv_ref, abs_ref, rope_ref,
                 pos_ref, slope_ref, qd_ref, kd_ref, vd_ref, qm_ref, km_ref, vm_ref):
    xn = _rms(x_ref[...], g_ref[...]).astype(BF16)
    r = _dot(xn, w_ref[...])
    w = DIFF_WIDTH
    qd_ref[...] = (r[:, 0:w] * (DIFF_HALF_DIM ** -0.5 * LOG2E)).astype(BF16)
    lane = lax.broadcasted_iota(jnp.int32, (x_ref.shape[0], LANES), 1)
    for p in range(DIFF_HEADS // 2):
        bias = pos_ref[...] * slope_ref[p:p + 1, :]
        hi = bias.astype(BF16)
        mid = (bias - hi.astype(F32)).astype(BF16)
        low = (bias - hi.astype(F32) - mid.astype(F32)).astype(BF16)
        term = jnp.where((lane == 0) | (lane == 3), hi, jnp.where((lane == 1) | (lane == 4), mid, low))
        kd_ref[:, 2 * p * LANES:(2 * p + 1) * LANES] = r[:, w + p * LANES:w + (p + 1) * LANES].astype(BF16)
        kd_ref[:, (2 * p + 1) * LANES:(2 * p + 2) * LANES] = term
    vd_ref[0] = _values_transposed(r[:, 2 * w:3 * w], DIFF_HEADS)
    o = 3 * w
    cq = r[:, o:o + MLA_Q_RANK]
    o += MLA_Q_RANK
    ckv = r[:, o:o + MLA_KV_RANK]
    o += MLA_KV_RANK
    ang = abs_ref[...] * rope_ref[0:1, :]
    cos = jnp.cos(ang)
    sin = jnp.sin(ang) * rope_ref[1:2, :]
    kpe = r[:, o:o + MLA_SLAB] * cos + r[:, o + MLA_SLAB:o + 2 * MLA_SLAB] * sin
    q2 = _dot(_rms(cq, gq_ref[...]).astype(BF16), wq_ref[...])
    kv2 = _dot(_rms(ckv, gkv_ref[...]).astype(BF16), wkv_ref[...])
    sw = MLA_HEADS * MLA_SLAB
    for h in range(MLA_HEADS):
        lo = h * MLA_SLAB
        qh = q2[:, lo:lo + MLA_SLAB] * cos + q2[:, sw + lo:sw + lo + MLA_SLAB] * sin
        qm_ref[:, lo:lo + MLA_SLAB] = (qh * (MLA_QK_DIM ** -0.5 * LOG2E)).astype(BF16)
        km_ref[:, lo:lo + MLA_SLAB] = (kv2[:, lo:lo + MLA_SLAB] + kpe).astype(BF16)
    vm_ref[0] = _values_transposed(kv2[:, sw:sw + MLA_WIDTH], MLA_HEADS)


def _proj_call(x2d, g, w_all, gq, wq2, gkv, wkv2, pos_abs, rope_rows, pos_rel, slope_rows, seq):
    t, d = x2d.shape
    rows = min(PROJ_ROWS, seq)
    per_seq = seq // rows
    row = lambda i: (i, 0)
    fixed = lambda i: (0, 0)
    full = lambda a: pl.BlockSpec(a.shape, fixed)
    row_out = lambda n: (pl.BlockSpec((rows, n), row), jax.ShapeDtypeStruct((t, n), BF16))
    vt_out = lambda n: (pl.BlockSpec((1, n, rows), lambda i: (i // per_seq, 0, i % per_seq)),
                        jax.ShapeDtypeStruct((t // seq, n, seq), BF16))
    outs = [row_out(DIFF_WIDTH), row_out(DIFF_HEADS * LANES), vt_out(DIFF_HEADS * V_ROWS),
            row_out(MLA_HEADS * MLA_SLAB), row_out(MLA_HEADS * MLA_SLAB),
            vt_out(MLA_HEADS * V_ROWS)]
    return pl.pallas_call(
        _proj_kernel,
        grid=(t // rows,),
        in_specs=[pl.BlockSpec((rows, d), row), full(g), full(w_all), full(gq), full(wq2),
                  full(gkv), full(wkv2), pl.BlockSpec((rows, 1), row), full(rope_rows),
                  pl.BlockSpec((rows, 1), row), full(slope_rows)],
        out_specs=[spec for spec, _ in outs],
        out_shape=[shape for _, shape in outs],
        compiler_params=_params("parallel"),
        name="proj",
    )(x2d, g, w_all, gq, wq2, gkv, wkv2, pos_abs, rope_rows, pos_rel, slope_rows)


def _softmax_step(s, col_max, vt_chunk, parts, m, *accs):
    m_new = jnp.maximum(m, col_max)
    alpha = jnp.exp2(m - m_new)
    pb = jnp.exp2(s - m_new).astype(BF16)
    accs = tuple(alpha[:, cols] * acc + _dot(vt_chunk[rows], pb[:, cols])
                 for (rows, cols), acc in zip(parts, accs))
    return (m_new,) + accs


def _normalised(acc):
    return acc[:V_ONES_ROW] * (1.0 / acc[V_ONES_ROW:V_ONES_ROW + 1])


def _causal_mask(s, key0, query0, tq):
    key = key0 + lax.broadcasted_iota(jnp.int32, s.shape, 0)
    query = query0 + (lax.broadcasted_iota(jnp.int32, s.shape, 1) & (tq - 1))
    return jnp.where(key <= query, s, MASK_VALUE)


def _causal_sweep(scores, consume, carry, n_full, slot_a, slot_b):
    scores(0, slot_a)

    def pair(i, c):
        j = 2 * i
        scores(j + 1, slot_b)
        c = consume(j, slot_a, c, False)
        scores(j + 2, slot_a)
        return consume(j + 1, slot_b, c, False)

    carry = lax.fori_loop(0, lax.shift_right_logical(n_full, 1), pair, carry)

    def odd_tail(c):
        scores(n_full, slot_b)
        c = consume(n_full - 1, slot_a, c, False)
        return consume(n_full, slot_b, c, True)

    def even_tail(c):
        return consume(n_full, slot_a, c, True)

    return lax.cond((n_full & 1) == 1, odd_tail, even_tail, carry)


def _diff_attn_kernel(lam_ref, q_ref, k_ref, vt_ref, g_ref, o_ref, qs_ref, sa_ref, sb_ref,
                      ma_ref, mb_ref, *, tq, tk):
    qi = pl.program_id(2)
    q = q_ref[0]
    lane = lax.broadcasted_iota(jnp.int32, (tq, LANES), 1)
    for v in range(4):
        keep = (lane >= v * DIFF_HALF_DIM) & (lane < (v + 1) * DIFF_HALF_DIM)
        ones = (lane >= 3 * (v // 2)) & (lane < 3 * (v // 2) + 3)
        qs_ref[v * tq:(v + 1) * tq, 0:LANES] = jnp.where(keep, q, jnp.zeros_like(q))
        qs_ref[v * tq:(v + 1) * tq, LANES:2 * LANES] = jnp.where(ones, 1.0, 0.0).astype(BF16)

    def scores(j, slot):
        start = pl.multiple_of(j * tk, tk)
        s = lax.dot_general(k_ref[0, pl.ds(start, tk), :], qs_ref[...], _NT,
                            preferred_element_type=F32)
        slot[0][...] = s
        slot[1][...] = jnp.max(s, axis=0, keepdims=True)

    def consume(j, slot, carry, masked):
        start = pl.multiple_of(j * tk, tk)
        s = slot[0][...]
        if masked:
            s = _causal_mask(s, start, qi * tq, tq)
            col_max = jnp.max(s, axis=0, keepdims=True)
        else:
            col_max = slot[1][...]
        return _softmax_step(s, col_max, vt_ref[0, :, pl.ds(start, tk)], parts, *carry)

    parts = [(slice(h * V_ROWS, (h + 1) * V_ROWS), slice(2 * h * tq, (2 * h + 2) * tq))
             for h in range(2)]
    init = (jnp.full((1, 4 * tq), MASK_VALUE, F32),
            jnp.zeros((V_ROWS, 2 * tq), F32), jnp.zeros((V_ROWS, 2 * tq), F32))
    _, *accs = _causal_sweep(scores, consume, init, (qi * tq) // tk,
                             (sa_ref, ma_ref), (sb_ref, mb_ref))

    lam = lam_ref[0]
    halves = []
    for acc in accs:
        on = _normalised(acc)
        o = on[:, :tq] - lam * on[:, tq:]
        halves.append(o * lax.rsqrt(jnp.mean(o * o, axis=0, keepdims=True) + RMS_EPS))
    o_ref[0] = (jnp.concatenate(halves, axis=0) * g_ref[...]).T.astype(o_ref.dtype)


def _diff_attn_call(lam, qd, kd, vdt, g_col, tq, tk):
    b, s, _ = qd.shape
    pairs = DIFF_HEADS // 2
    return pl.pallas_call(
        functools.partial(_diff_attn_kernel, tq=tq, tk=tk),
        grid=(b, pairs, s // tq),
        in_specs=[pl.BlockSpec(memory_space=pltpu.SMEM),
                  pl.BlockSpec((1, tq, LANES), lambda bi, p, qi: (bi, qi, p)),
                  pl.BlockSpec((1, s, 2 * LANES), lambda bi, p, qi: (bi, 0, p)),
                  pl.BlockSpec((1, 2 * V_ROWS, s), lambda bi, p, qi: (bi, p, 0)),
                  pl.BlockSpec((LANES, 1), lambda bi, p, qi: (0, 0))],
        out_specs=pl.BlockSpec((1, tq, LANES), lambda bi, p, qi: (bi, qi, p)),
        out_shape=jax.ShapeDtypeStruct((b, s, DIFF_WIDTH), BF16),
        scratch_shapes=[pltpu.VMEM((4 * tq, 2 * LANES), BF16), pltpu.VMEM((tk, 4 * tq), F32),
                        pltpu.VMEM((tk, 4 * tq), F32), pltpu.VMEM((1, 4 * tq), F32),
                        pltpu.VMEM((1, 4 * tq), F32)],
        compiler_params=_params("parallel", "parallel", "arbitrary"),
        name="diff_attn",
    )(lam, qd, kd, vdt, g_col)


def _mla_attn_kernel(q_ref, k_ref, vt_ref, o_ref, sa_ref, sb_ref, ma_ref, mb_ref, *, tq, tk,
                     heads):
    qi = pl.program_id(2)
    slabs = [slice(hh * MLA_SLAB, (hh + 1) * MLA_SLAB) for hh in range(heads)]

    def scores(j, slot):
        start = pl.multiple_of(j * tk, tk)
        for hh, sl in enumerate(slabs):
            s = lax.dot_general(k_ref[0, pl.ds(start, tk), sl], q_ref[0, :, sl], _NT,
                                preferred_element_type=F32)
            slot[0][hh] = s
            slot[1][hh] = jnp.max(s, axis=0, keepdims=True)

    def consume(j, slot, carry, masked):
        start = pl.multiple_of(j * tk, tk)
        vtc = vt_ref[0, :, pl.ds(start, tk)]
        out = []
        for hh in range(heads):
            s = slot[0][hh]
            if masked:
                s = _causal_mask(s, start, qi * tq, tq)
                col_max = jnp.max(s, axis=0, keepdims=True)
            else:
                col_max = slot[1][hh]
            rows = slice(hh * V_ROWS, (hh + 1) * V_ROWS)
            out.extend(_softmax_step(s, col_max, vtc, [(rows, slice(None))],
                                     *carry[2 * hh:2 * hh + 2]))
        return tuple(out)

    init = (jnp.full((1, tq), MASK_VALUE, F32), jnp.zeros((V_ROWS, tq), F32)) * heads
    carry = _causal_sweep(scores, consume, init, (qi * tq) // tk,
                          (sa_ref, ma_ref), (sb_ref, mb_ref))
    o = jnp.concatenate([_normalised(carry[2 * hh + 1]) for hh in range(heads)], axis=0)
    o_ref[0] = o.T.astype(o_ref.dtype)


def _mla_attn_call(qm, km, vmt, tq, tk, heads):
    b, s, _ = qm.shape
    groups = MLA_HEADS // heads
    return pl.pallas_call(
        functools.partial(_mla_attn_kernel, tq=tq, tk=tk, heads=heads),
        grid=(b, groups, s // tq),
        in_specs=[pl.BlockSpec((1, tq, heads * MLA_SLAB), lambda bi, p, qi: (bi, qi, p)),
                  pl.BlockSpec((1, s, heads * MLA_SLAB), lambda bi, p, qi: (bi, 0, p)),
                  pl.BlockSpec((1, heads * V_ROWS, s), lambda bi, p, qi: (bi, p, 0))],
        out_specs=pl.BlockSpec((1, tq, heads * MLA_V_DIM), lambda bi, p, qi: (bi, qi, p)),
        out_shape=jax.ShapeDtypeStruct((b, s, MLA_WIDTH), BF16),
        scratch_shapes=[pltpu.VMEM((heads, tk, tq), F32), pltpu.VMEM((heads, tk, tq), F32),
                        pltpu.VMEM((heads, 1, tq), F32), pltpu.VMEM((heads, 1, tq), F32)],
        compiler_params=_params("parallel", "parallel", "arbitrary"),
        name="mla_attn",
    )(qm, km, vmt)


def _mem_kv_kernel(mem_ref, g_ref, w_ref, k_ref, v_ref):
    kv = _dot(_rms(mem_ref[0], g_ref[...]).astype(BF16), w_ref[...])
    k_ref[0] = kv[:, :MEM_WIDTH].astype(BF16)
    v_ref[0] = kv[:, MEM_WIDTH:].astype(BF16)


def _mem_kv_call(mem, g, w_kv):
    b, n_mem, d = mem.shape
    blk = pl.BlockSpec((1, n_mem, MEM_WIDTH), lambda bi: (bi, 0, 0))
    return pl.pallas_call(
        _mem_kv_kernel,
        grid=(b,),
        in_specs=[pl.BlockSpec((1, n_mem, d), lambda bi: (bi, 0, 0)),
                  pl.BlockSpec(g.shape, lambda bi: (0, 0)),
                  pl.BlockSpec(w_kv.shape, lambda bi: (0, 0))],
        out_specs=[blk, blk],
        out_shape=[jax.ShapeDtypeStruct((b, n_mem, MEM_WIDTH), BF16)] * 2,
        compiler_params=_params("parallel"),
        name="mem_kv",
    )(mem, g, w_kv)


def _store_tile_rows(ref, x):
    n = x.shape[0]
    chunks = x.shape[1] // LANES
    for c in range(chunks):
        ref[pl.ds(c, n, stride=chunks), :] = x[:, c * LANES:(c + 1) * LANES]


def _load_tile_rows(ref, chunks):
    n = ref.shape[0] // chunks
    return jnp.concatenate([ref[pl.ds(c, n, stride=chunks), :] for c in range(chunks)], axis=1)


def _tile_row(ref, r, chunks):
    return ref.at[pl.ds(pl.multiple_of(r * chunks, chunks), chunks)]


def _mix_kernel(x_ref, d_ref, m_ref, gm_ref, wo_ref, gc_ref, wq_ref, km_ref, vm_ref, wmo_ref,
                gf_ref, wr_ref, br_ref, x2_ref, h3_ref, r_ref, rt_ref, cnt_ref, run_ref):
    rows = x_ref.shape[0]

    @pl.when(pl.program_id(0) == 0)
    def _():
        run_ref[...] = jnp.zeros_like(run_ref)

    mla_n = _rms(m_ref[...].astype(F32), gm_ref[...]).astype(BF16)
    x1 = (x_ref[...] + _dot(d_ref[...], wo_ref[0:DIFF_WIDTH, :])
          + _dot(mla_n, wo_ref[DIFF_WIDTH:DIFF_WIDTH + MLA_WIDTH, :]))

    qx = _dot(_rms(x1, gc_ref[...]).astype(BF16), wq_ref[...])
    heads = []
    for h in range(MEM_HEADS):
        sl = slice(h * MEM_HEAD_DIM, (h + 1) * MEM_HEAD_DIM)
        sc = lax.dot_general(qx[:, sl].astype(BF16), km_ref[0, :, sl], _NT,
                             preferred_element_type=F32) * MEM_HEAD_DIM ** -0.5
        p = jnp.exp(sc - jnp.max(sc, axis=-1, keepdims=True))
        o = _dot(p.astype(BF16), vm_ref[0, :, sl]) * (1.0 / jnp.sum(p, axis=-1, keepdims=True))
        heads.append(o.astype(BF16))
    x2 = x1 + _dot(jnp.concatenate(heads, axis=1), wmo_ref[...])
    x2_ref[...] = x2

    h3 = _rms(x2, gf_ref[...])
    _store_tile_rows(h3_ref, h3)
    h_hi = h3.astype(BF16)
    h_lo = (h3 - h_hi.astype(F32)).astype(BF16)
    logits = (_dot(h_hi, wr_ref[0]) + _dot(h_lo, wr_ref[0]) + _dot(h_hi, wr_ref[1])
              + br_ref[...])
    lane = lax.broadcasted_iota(jnp.int32, (rows, LANES), 1).astype(F32)
    far = float(4 * LANES)

    def first_argmax(vals):
        top = jnp.max(vals, axis=-1, keepdims=True)
        return top, jnp.min(jnp.where(vals == top, lane, far), axis=-1, keepdims=True)

    g_logits = jnp.where(lane < N_GROUPS, logits, MASK_VALUE)
    g_top, g_sel = first_argmax(g_logits)
    g_gate = 1.0 / jnp.sum(jnp.exp(g_logits - g_top), axis=-1, keepdims=True)
    lo = ROUTER_LANE0 + EXPERTS_PER_GROUP * g_sel
    e_logits = jnp.where(lane >= lo, jnp.where(lane < lo + EXPERTS_PER_GROUP, logits, MASK_VALUE),
                         MASK_VALUE)
    top1, lane1 = first_argmax(e_logits)
    top2, lane2 = first_argmax(jnp.where(lane == lane1, MASK_VALUE, e_logits))
    t = jnp.exp(top2 - top1)
    gate1 = g_gate / (1.0 + t)
    gate2 = gate1 * t

    hot = jnp.where(lane == lane1, 1.0, 0.0) + jnp.where(lane == lane2, 1.0, 0.0)
    tri = (lax.broadcasted_iota(jnp.int32, (rows, rows), 1)
           < lax.broadcasted_iota(jnp.int32, (rows, rows), 0))
    before = _dot(jnp.where(tri, 1.0, 0.0).astype(BF16), hot.astype(BF16)) + run_ref[0:1, :]
    rank1 = jnp.sum(jnp.where(lane == lane1, before, 0.0), axis=-1, keepdims=True)
    rank2 = jnp.sum(jnp.where(lane == lane2, before, 0.0), axis=-1, keepdims=True)
    run_ref[...] = run_ref[...] + jnp.sum(hot, axis=0, keepdims=True)
    cnt_ref[...] = run_ref[...]

    fields = (lane1 - ROUTER_LANE0, lane2 - ROUTER_LANE0, gate1, gate2, rank1, rank2)
    out = jnp.zeros((rows, LANES), F32)
    for i, f in enumerate(fields):
        out = jnp.where(lane == float(i), f, out)
    r_ref[...] = out
    rt_ref[...] = out.T[0:8, :]


def _mix_call(x2d, diff, mla, gm, wo, gc, wq, kmem, vmem, wmo, gf, wr, br, seq):
    t, d = x2d.shape
    rows = min(MIX_ROWS, seq)
    per_seq = seq // rows
    row = lambda i: (i, 0)
    fixed = lambda i: (0, 0)
    full = lambda a: pl.BlockSpec(a.shape, lambda i: (0,) * a.ndim)
    memblk = pl.BlockSpec((1,) + kmem.shape[1:], lambda i: (i // per_seq, 0, 0))
    return pl.pallas_call(
        _mix_kernel,
        grid=(t // rows,),
        in_specs=[pl.BlockSpec((rows, d), row), pl.BlockSpec((rows, DIFF_WIDTH), row),
                  pl.BlockSpec((rows, MLA_WIDTH), row), full(gm), full(wo), full(gc), full(wq),
                  memblk, memblk, full(wmo), full(gf), full(wr), full(br)],
        out_specs=[pl.BlockSpec((rows, d), row), pl.BlockSpec((rows * d // LANES, LANES), row),
                   pl.BlockSpec((rows, LANES), row), pl.BlockSpec((8, rows), lambda i: (0, i)),
                   pl.BlockSpec((8, LANES), fixed)],
        out_shape=[jax.ShapeDtypeStruct((t, d), F32),
                   jax.ShapeDtypeStruct((t * d // LANES, LANES), F32),
                   jax.ShapeDtypeStruct((t, LANES), F32), jax.ShapeDtypeStruct((8, t), F32),
                   jax.ShapeDtypeStruct((8, LANES), F32)],
        scratch_shapes=[pltpu.VMEM((8, LANES), F32)],
        compiler_params=_params("arbitrary"),
        name="mix",
    )(x2d, diff, mla, gm, wo, gc, wq, kmem, vmem, wmo, gf, wr, br)


def _dispatch_kernel(d0_ref, d1_ref, h_ref, xs_in_ref, xs_ref, sem):
    del xs_in_ref
    rows = h_ref.shape[0] // ROW_CHUNKS

    def issue(r, c):
        src = _tile_row(h_ref, r, ROW_CHUNKS)
        pltpu.make_async_copy(src, _tile_row(xs_ref, d0_ref[0, 0, r], ROW_CHUNKS), sem).start(
            priority=0)
        pltpu.make_async_copy(src, _tile_row(xs_ref, d1_ref[0, 0, r], ROW_CHUNKS), sem).start(
            priority=1)
        return c

    lax.fori_loop(0, rows, issue, 0, unroll=8)
    for _ in range(2):
        pltpu.make_async_copy(h_ref, xs_ref.at[pl.ds(0, rows * ROW_CHUNKS)], sem).wait()


def _dispatch_call(d0, d1, h3, xs0, rows):
    t = h3.shape[0] // ROW_CHUNKS
    d0, d1 = d0.reshape(t // rows, 1, rows), d1.reshape(t // rows, 1, rows)
    idx = pl.BlockSpec((1, 1, rows), lambda i: (i, 0, 0), memory_space=pltpu.SMEM)
    return pl.pallas_call(
        _dispatch_kernel,
        grid=(t // rows,),
        in_specs=[idx, idx, pl.BlockSpec((rows * ROW_CHUNKS, LANES), lambda i: (i, 0)),
                  pl.BlockSpec(memory_space=pl.ANY)],
        out_specs=pl.BlockSpec(memory_space=pl.ANY),
        out_shape=jax.ShapeDtypeStruct(xs0.shape, xs0.dtype),
        scratch_shapes=[pltpu.SemaphoreType.DMA(())],
        input_output_aliases={3: 0},
        compiler_params=_params("arbitrary"),
        name="dispatch",
    )(d0, d1, h3, xs0)


def _expert_kernel(be_ref, x_ref, wg_ref, wu_ref, wd_ref, y_ref, wg_bf, wu_bf, wd_bf):
    i = pl.program_id(0)

    @pl.when((i == 0) | (be_ref[i] != be_ref[jnp.maximum(i - 1, 0)]))
    def _():
        wg_bf[...] = wg_ref[0].astype(BF16)
        wu_bf[...] = wu_ref[0].astype(BF16)
        wd_bf[...] = wd_ref[0].astype(BF16)

    x = _load_tile_rows(x_ref, ROW_CHUNKS).astype(BF16)
    gate = _dot(x, wg_bf[...])
    up = _dot(x, wu_bf[...])
    hidden = gate * (1.0 / (1.0 + jnp.exp(-gate))) * up
    _store_tile_rows(y_ref, _dot(hidden.astype(BF16), wd_bf[...]))


def _expert_call(block_expert, xs, wg, wu, wd, rows):
    _, d, ff = wg.shape
    n_slots = xs.shape[0] // ROW_CHUNKS
    grid_spec = pltpu.PrefetchScalarGridSpec(
        num_scalar_prefetch=1,
        grid=(n_slots // rows,),
        in_specs=[pl.BlockSpec((rows * ROW_CHUNKS, LANES), lambda i, be: (i, 0)),
                  pl.BlockSpec((1, d, ff), lambda i, be: (be[i], 0, 0)),
                  pl.BlockSpec((1, d, ff), lambda i, be: (be[i], 0, 0)),
                  pl.BlockSpec((1, ff, d), lambda i, be: (be[i], 0, 0))],
        out_specs=pl.BlockSpec((rows * ROW_CHUNKS, LANES), lambda i, be: (i, 0)),
        scratch_shapes=[pltpu.VMEM((d, ff), BF16), pltpu.VMEM((d, ff), BF16),
                        pltpu.VMEM((ff, d), BF16)],
    )
    return pl.pallas_call(
        _expert_kernel,
        grid_spec=grid_spec,
        out_shape=jax.ShapeDtypeStruct(xs.shape, F32),
        compiler_params=_params("arbitrary"),
        name="experts",
    )(block_expert, xs, wg, wu, wd)


def _combine_kernel(d0_ref, d1_ref, n0_ref, n1_ref, x_ref, r_ref, g_ref, y_ref, o_ref, buf0, buf1,
                    sems):
    rows = x_ref.shape[0]
    i = pl.program_id(0)
    slot = i & 1

    def start_gathers(i0_ref, i1_ref, sl):
        def issue(r, c):
            pltpu.make_async_copy(_tile_row(y_ref, i0_ref[0, 0, r], ROW_CHUNKS),
                                  _tile_row(buf0.at[sl], r, ROW_CHUNKS),
                                  sems.at[sl]).start(priority=0)
            pltpu.make_async_copy(_tile_row(y_ref, i1_ref[0, 0, r], ROW_CHUNKS),
                                  _tile_row(buf1.at[sl], r, ROW_CHUNKS),
                                  sems.at[sl]).start(priority=1)
            return c

        lax.fori_loop(0, rows, issue, 0, unroll=8)

    @pl.when(i == 0)
    def _():
        start_gathers(d0_ref, d1_ref, slot)

    @pl.when(i + 1 < pl.num_programs(0))
    def _():
        start_gathers(n0_ref, n1_ref, 1 - slot)

    for buf in (buf0, buf1):
        pltpu.make_async_copy(y_ref.at[pl.ds(0, rows * ROW_CHUNKS)], buf.at[slot],
                              sems.at[slot]).wait()
    gates = r_ref[...]
    x3 = (x_ref[...] + gates[:, 2:3] * _load_tile_rows(buf0.at[slot], ROW_CHUNKS)
          + gates[:, 3:4] * _load_tile_rows(buf1.at[slot], ROW_CHUNKS))
    o_ref[...] = _rms(x3, g_ref[...])


def _combine_call(d0, d1, x2, router, g, y, rows):
    t, d = x2.shape
    d0, d1 = d0.reshape(t // rows, 1, rows), d1.reshape(t // rows, 1, rows)
    last = t // rows - 1
    idx = pl.BlockSpec((1, 1, rows), lambda i: (i, 0, 0), memory_space=pltpu.SMEM)
    nxt = pl.BlockSpec((1, 1, rows), lambda i: (jnp.minimum(i + 1, last), 0, 0),
                       memory_space=pltpu.SMEM)
    row = lambda i: (i, 0)
    return pl.pallas_call(
        _combine_kernel,
        grid=(t // rows,),
        in_specs=[idx, idx, nxt, nxt, pl.BlockSpec((rows, d), row),
                  pl.BlockSpec((rows, LANES), row), pl.BlockSpec(g.shape, lambda i: (0, 0)),
                  pl.BlockSpec(memory_space=pl.ANY)],
        out_specs=pl.BlockSpec((rows, d), row),
        out_shape=jax.ShapeDtypeStruct((t, d), F32),
        scratch_shapes=[pltpu.VMEM((2, rows * ROW_CHUNKS, LANES), F32),
                        pltpu.VMEM((2, rows * ROW_CHUNKS, LANES), F32),
                        pltpu.SemaphoreType.DMA((2,))],
        compiler_params=_params("arbitrary"),
        name="combine",
    )(d0, d1, d0, d1, x2, router, g, y)


def _slab(cols, lo):
    return jnp.pad(cols, ((0, 0), (lo, MLA_SLAB - lo - cols.shape[1])))


def _layer_weights(w_in, w_uq, w_ukv):
    half = MLA_ROPE_DIM // 2
    o = 3 * DIFF_WIDTH + MLA_Q_RANK + MLA_KV_RANK
    kpe1, kpe2 = w_in[:, o:o + half], w_in[:, o + half:o + 2 * half]
    w_all = jnp.concatenate(
        [w_in[:, :o],
         _slab(jnp.concatenate([kpe1, kpe2], axis=1), MLA_NOPE_DIM),
         _slab(jnp.concatenate([kpe2, kpe1], axis=1), MLA_NOPE_DIM)], axis=1).astype(BF16)
    q_main, q_swap, k_nope, v_cols = [], [], [], []
    for h in range(MLA_HEADS):
        c = w_uq[:, h * MLA_QK_DIM:(h + 1) * MLA_QK_DIM]
        pe1 = c[:, MLA_NOPE_DIM:MLA_NOPE_DIM + half]
        pe2 = c[:, MLA_NOPE_DIM + half:]
        q_main.append(_slab(c, 0))
        q_swap.append(_slab(jnp.concatenate([pe2, pe1], axis=1), MLA_NOPE_DIM))
        kv = w_ukv[:, h * (MLA_NOPE_DIM + MLA_V_DIM):(h + 1) * (MLA_NOPE_DIM + MLA_V_DIM)]
        k_nope.append(_slab(kv[:, :MLA_NOPE_DIM], 0))
        v_cols.append(kv[:, MLA_NOPE_DIM:])
    wq2 = jnp.concatenate(q_main + q_swap, axis=1).astype(BF16)
    wkv2 = jnp.concatenate(k_nope + v_cols, axis=1).astype(BF16)
    return w_all, wq2, wkv2


def _rope_rows():
    inv_freq = ROPE_THETA ** (-jnp.arange(0, MLA_ROPE_DIM, 2, dtype=F32) / MLA_ROPE_DIM)
    half = jnp.ones_like(inv_freq)
    place = lambda a, b: jnp.pad(jnp.concatenate([a, b]),
                                 (MLA_NOPE_DIM, MLA_SLAB - MLA_QK_DIM))
    return jnp.stack([place(inv_freq, inv_freq), place(-half, half)])


def kernel(x, mem, positions, norm_mix_g, w_in, diff_lambda_q1, diff_lambda_k1, diff_lambda_q2, diff_lambda_k2, diff_out_g, mla_q_norm_g, w_mla_uq, mla_kv_norm_g, w_mla_ukv, mla_out_g, w_out, norm_cross_g, norm_mem_g, w_mem_q, w_mem_kv, w_mem_o, norm_ffn_g, w_group_router, b_group_router, w_expert_router, b_expert_router, w_expert_gate, w_expert_up, w_expert_down, norm_final_g):
    b, s, d = x.shape
    t = b * s
    depth = w_in.shape[0]
    row2 = lambda v: v.reshape(1, -1).astype(F32)
    pos_f = positions.astype(F32)
    slopes = 2.0 ** (-8.0 * jnp.arange(1, DIFF_HEADS + 1, dtype=F32) / DIFF_HEADS)
    pos_rel = (pos_f - pos_f[:, :1]).reshape(t, 1)
    slope_rows = jnp.pad(jnp.repeat((LOG2E * slopes).reshape(DIFF_HEADS // 2, 2), 3, axis=1),
                         ((0, 0), (0, LANES - 6)))

    diff_tq, mla_tq = min(DIFF_TQ, s), min(MLA_TQ, s)
    attn_tk = min(ATTN_TK, s)
    blk = EXPERT_ROWS
    n_assign = 2 * t
    n_blocks = -(-(n_assign + N_EXPERTS * (blk - 1)) // blk)
    n_slots = n_blocks * blk

    assert depth == 1, "single-layer problem: the combine kernel applies the final RMSNorm"
    l = 0
    x2d = x.reshape(t, d)
    w_all, wq2, wkv2 = _layer_weights(w_in[l], w_mla_uq[l], w_mla_ukv[l])
    qd, kd, vdt, qm, km, vmt = _proj_call(
        x2d, row2(norm_mix_g[l]), w_all, row2(mla_q_norm_g[l]), wq2,
        row2(mla_kv_norm_g[l]), wkv2, pos_f.reshape(t, 1), _rope_rows(), pos_rel, slope_rows, s)

    lambda_init = 0.8 - 0.6 * math.exp(-0.3 * l)
    lam = (jnp.exp(jnp.sum(diff_lambda_q1[l] * diff_lambda_k1[l]))
           - jnp.exp(jnp.sum(diff_lambda_q2[l] * diff_lambda_k2[l])) + lambda_init)
    gain_col = jnp.tile(diff_out_g[l] * (1.0 - lambda_init), 2).reshape(-1, 1).astype(F32)
    shp = lambda a: a.reshape(b, s, a.shape[-1])
    diff_out = _diff_attn_call(lam.reshape(1).astype(F32), shp(qd), shp(kd), vdt, gain_col,
                               diff_tq, attn_tk)
    mla_out = _mla_attn_call(shp(qm), shp(km), vmt, mla_tq, attn_tk, MLA_HEADS_PER_STEP)

    kmem, vmem = _mem_kv_call(mem, row2(norm_mem_g[l]), w_mem_kv[l].astype(BF16))
    w_router = jnp.pad(
        jnp.concatenate([w_group_router[l], w_expert_router[l]], axis=1),
        ((0, 0), (0, LANES - N_GROUPS - N_EXPERTS)))
    w_router_hi = w_router.astype(BF16)
    w_router = jnp.stack([w_router_hi, (w_router - w_router_hi.astype(F32)).astype(BF16)])
    b_router = jnp.pad(jnp.concatenate([b_group_router[l], b_expert_router[l]]),
                       (0, LANES - N_GROUPS - N_EXPERTS))
    x2, h3, router, router_t, counts = _mix_call(
        x2d, diff_out.reshape(t, DIFF_WIDTH), mla_out.reshape(t, MLA_WIDTH),
        row2(mla_out_g[l]), w_out[l].astype(BF16), row2(norm_cross_g[l]),
        w_mem_q[l].astype(BF16), kmem, vmem, w_mem_o[l].astype(BF16),
        row2(norm_ffn_g[l]), w_router, row2(b_router), s)

    cnt = counts[0, ROUTER_LANE0:ROUTER_LANE0 + N_EXPERTS].astype(jnp.int32)
    padded = ((cnt + blk - 1) // blk) * blk
    seg_end = jnp.cumsum(padded)
    seg_start = seg_end - padded
    ridx = router_t.astype(jnp.int32)
    expert_ids = jnp.arange(N_EXPERTS, dtype=jnp.int32)[:, None]

    def slot_of(choice):
        start = jnp.sum(jnp.where(ridx[choice][None, :] == expert_ids, seg_start[:, None], 0),
                        axis=0)
        return start + ridx[4 + choice]

    dest0, dest1 = slot_of(0), slot_of(1)
    block_start = jnp.arange(n_blocks, dtype=jnp.int32) * blk
    block_expert = jnp.minimum(
        jnp.sum((seg_end[None, :] <= block_start[:, None]).astype(jnp.int32), axis=1),
        N_EXPERTS - 1)

    assert d == ROW_CHUNKS * LANES
    xs = _dispatch_call(dest0, dest1, h3, jnp.zeros((n_slots * ROW_CHUNKS, LANES), F32),
                        min(DISPATCH_ROWS, t))
    y = _expert_call(block_expert, xs, w_expert_gate[l], w_expert_up[l], w_expert_down[l], blk)
    out = _combine_call(dest0, dest1, x2, router, row2(norm_final_g), y, min(COMBINE_ROWS, t))
    return out.reshape(b, s, d)
```

```python
import functools
import math

import jax
import jax.numpy as jnp
from jax import lax
from jax.experimental import pallas as pl
from jax.experimental.pallas import tpu as pltpu

F32 = jnp.float32
BF16 = jnp.bfloat16

RMS_EPS = 1e-6
ROPE_THETA = 10000.0
MASK_VALUE = -1e30

DIFF_HEADS = 8
DIFF_HALF_DIM = 32
DIFF_V_DIM = 64
DIFF_WIDTH = DIFF_HEADS * DIFF_V_DIM

MLA_HEADS = 8
MLA_Q_RANK = 256
MLA_KV_RANK = 128
MLA_NOPE_DIM = 64
MLA_ROPE_DIM = 32
MLA_QK_DIM = MLA_NOPE_DIM + MLA_ROPE_DIM
MLA_V_DIM = 64
MLA_WIDTH = MLA_HEADS * MLA_V_DIM

MEM_HEADS = 4
MEM_HEAD_DIM = 128
MEM_WIDTH = MEM_HEADS * MEM_HEAD_DIM

N_GROUPS = 4
EXPERTS_PER_GROUP = 8
N_EXPERTS = N_GROUPS * EXPERTS_PER_GROUP
EXPERT_FF = 256

LANES = 128
MLA_SLAB = LANES
ROUTER_LANE0 = N_GROUPS
BF16_SUBLANES = 16
V_ONES_ROW = DIFF_V_DIM
V_ROWS = DIFF_V_DIM + BF16_SUBLANES
ROW_CHUNKS = 8

PROJ_ROWS = 512
MIX_ROWS = 1024
DISPATCH_ROWS = 1024
COMBINE_ROWS = 256
EXPERT_ROWS = 512
ATTN_TK = 512
DIFF_TQ = 512
MLA_TQ = 512
MLA_HEADS_PER_STEP = 4
LOG2E = math.log2(math.e)
V7X_VMEM_BYTES = 64 * 1024 * 1024
VMEM_LIMIT = V7X_VMEM_BYTES * 3 // 4

_NT = (((1,), (1,)), ((), ()))


def _rms(x, gain):
    return x * lax.rsqrt(jnp.mean(x * x, axis=-1, keepdims=True) + RMS_EPS) * gain


def _dot(a, b):
    return jnp.dot(a, b, preferred_element_type=F32)


def _params(*sem):
    return pltpu.CompilerParams(dimension_semantics=sem, vmem_limit_bytes=VMEM_LIMIT)


def _values_transposed(v, heads):
    vt = v.T
    n = v.shape[0]
    tail = jnp.where(lax.broadcasted_iota(jnp.int32, (V_ROWS - V_ONES_ROW, n), 0) == 0, 1.0, 0.0)
    pieces = []
    for h in range(heads):
        pieces += [vt[h * V_ONES_ROW:(h + 1) * V_ONES_ROW], tail]
    return jnp.concatenate(pieces, axis=0).astype(BF16)


def _proj_kernel(x_ref, g_ref, w_ref, gq_ref, wq_ref, gkv_ref, wkv_ref, abs_ref, rope_ref,
                 pos_ref, slope_ref, qd_ref, kd_ref, vd_ref, qm_ref, km_ref, vm_ref):
    xn = _rms(x_ref[...], g_ref[...]).astype(BF16)
    r = _dot(xn, w_ref[...])
    w = DIFF_WIDTH
    qd_ref[...] = (r[:, 0:w] * (DIFF_HALF_DIM ** -0.5 * LOG2E)).astype(BF16)
    lane = lax.broadcasted_iota(jnp.int32, (x_ref.shape[0], LANES), 1)
    for p in range(DIFF_HEADS // 2):
        bias = pos_ref[...] * slope_ref[p:p + 1, :]
        hi = bias.astype(BF16)
        mid = (bias - hi.astype(F32)).astype(BF16)
        low = (bias - hi.astype(F32) - mid.astype(F32)).astype(BF16)
        term = jnp.where((lane == 0) | (lane == 3), hi, jnp.where((lane == 1) | (lane == 4), mid, low))
        kd_ref[:, 2 * p * LANES:(2 * p + 1) * LANES] = r[:, w + p * LANES:w + (p + 1) * LANES].astype(BF16)
        kd_ref[:, (2 * p + 1) * LANES:(2 * p + 2) * LANES] = term
    vd_ref[0] = _values_transposed(r[:, 2 * w:3 * w], DIFF_HEADS)
    o = 3 * w
    cq = r[:, o:o + MLA_Q_RANK]
    o += MLA_Q_RANK
    ckv = r[:, o:o + MLA_KV_RANK]
    o += MLA_KV_RANK
    ang = abs_ref[...] * rope_ref[0:1, :]
    cos = jnp.cos(ang)
    sin = jnp.sin(ang) * rope_ref[1:2, :]
    kpe = r[:, o:o + MLA_SLAB] * cos + r[:, o + MLA_SLAB:o + 2 * MLA_SLAB] * sin
    q2 = _dot(_rms(cq, gq_ref[...]).astype(BF16), wq_ref[...])
    kv2 = _dot(_rms(ckv, gkv_ref[...]).astype(BF16), wkv_ref[...])
    sw = MLA_HEADS * MLA_SLAB
    for h in range(MLA_HEADS):
        lo = h * MLA_SLAB
        qh = q2[:, lo:lo + MLA_SLAB] * cos + q2[:, sw + lo:sw + lo + MLA_SLAB] * sin
        qm_ref[:, lo:lo + MLA_SLAB] = (qh * (MLA_QK_DIM ** -0.5 * LOG2E)).astype(BF16)
        km_ref[:, lo:lo + MLA_SLAB] = (kv2[:, lo:lo + MLA_SLAB] + kpe).astype(BF16)
    vm_ref[0] = _values_transposed(kv2[:, sw:sw + MLA_WIDTH], MLA_HEADS)


def _proj_call(x2d, g, w_all, gq, wq2, gkv, wkv2, pos_abs, rope_rows, pos_rel, slope_rows, seq):
    t, d = x2d.shape
    rows = min(PROJ_ROWS, seq)
    per_seq = seq // rows
    row = lambda i: (i, 0)
    fixed = lambda i: (0, 0)
    full = lambda a: pl.BlockSpec(a.shape, fixed)
    row_out = lambda n: (pl.BlockSpec((rows, n), row), jax.ShapeDtypeStruct((t, n), BF16))
    vt_out = lambda n: (pl.BlockSpec((1, n, rows), lambda i: (i // per_seq, 0, i % per_seq)),
                        jax.ShapeDtypeStruct((t // seq, n, seq), BF16))
    outs = [row_out(DIFF_WIDTH), row_out(DIFF_HEADS * LANES), vt_out(DIFF_HEADS * V_ROWS),
            row_out(MLA_HEADS * MLA_SLAB), row_out(MLA_HEADS * MLA_SLAB),
            vt_out(MLA_HEADS * V_ROWS)]
    return pl.pallas_call(
        _proj_kernel,
        grid=(t // rows,),
        in_specs=[pl.BlockSpec((rows, d), row), full(g), full(w_all), full(gq), full(wq2),
                  full(gkv), full(wkv2), pl.BlockSpec((rows, 1), row), full(rope_rows),
                  pl.BlockSpec((rows, 1), row), full(slope_rows)],
        out_specs=[spec for spec, _ in outs],
        out_shape=[shape for _, shape in outs],
        compiler_params=_params("parallel"),
        name="proj",
    )(x2d, g, w_all, gq, wq2, gkv, wkv2, pos_abs, rope_rows, pos_rel, slope_rows)


def _softmax_step(s, col_max, vt_chunk, parts, m, *accs):
    m_new = jnp.maximum(m, col_max)
    alpha = jnp.exp2(m - m_new)
    pb = jnp.exp2(s - m_new).astype(BF16)
    accs = tuple(alpha[:, cols] * acc + _dot(vt_chunk[rows], pb[:, cols])
                 for (rows, cols), acc in zip(parts, accs))
    return (m_new,) + accs


def _normalised(acc):
    return acc[:V_ONES_ROW] * (1.0 / acc[V_ONES_ROW:V_ONES_ROW + 1])


def _causal_mask(s, key0, query0, tq):
    key = key0 + lax.broadcasted_iota(jnp.int32, s.shape, 0)
    query = query0 + (lax.broadcasted_iota(jnp.int32, s.shape, 1) & (tq - 1))
    return jnp.where(key <= query, s, MASK_VALUE)


def _causal_sweep(scores, consume, carry, n_full, slot_a, slot_b):
    scores(0, slot_a)

    def pair(i, c):
        j = 2 * i
        scores(j + 1, slot_b)
        c = consume(j, slot_a, c, False)
        scores(j + 2, slot_a)
        return consume(j + 1, slot_b, c, False)

    carry = lax.fori_loop(0, lax.shift_right_logical(n_full, 1), pair, carry)

    def odd_tail(c):
        scores(n_full, slot_b)
        c = consume(n_full - 1, slot_a, c, False)
        return consume(n_full, slot_b, c, True)

    def even_tail(c):
        return consume(n_full, slot_a, c, True)

    return lax.cond((n_full & 1) == 1, odd_tail, even_tail, carry)


def _diff_attn_kernel(lam_ref, q_ref, k_ref, vt_ref, g_ref, o_ref, qs_ref, sa_ref, sb_ref,
                      ma_ref, mb_ref, *, tq, tk):
    qi = pl.program_id(2)
    q = q_ref[0]
    lane = lax.broadcasted_iota(jnp.int32, (tq, LANES), 1)
    for v in range(4):
        keep = (lane >= v * DIFF_HALF_DIM) & (lane < (v + 1) * DIFF_HALF_DIM)
        ones = (lane >= 3 * (v // 2)) & (lane < 3 * (v // 2) + 3)
        qs_ref[v * tq:(v + 1) * tq, 0:LANES] = jnp.where(keep, q, jnp.zeros_like(q))
        qs_ref[v * tq:(v + 1) * tq, LANES:2 * LANES] = jnp.where(ones, 1.0, 0.0).astype(BF16)

    def scores(j, slot):
        start = pl.multiple_of(j * tk, tk)
        s = lax.dot_general(k_ref[0, pl.ds(start, tk), :], qs_ref[...], _NT,
                            preferred_element_type=F32)
        slot[0][...] = s
        slot[1][...] = jnp.max(s, axis=0, keepdims=True)

    def consume(j, slot, carry, masked):
        start = pl.multiple_of(j * tk, tk)
        s = slot[0][...]
        if masked:
            s = _causal_mask(s, start, qi * tq, tq)
            col_max = jnp.max(s, axis=0, keepdims=True)
        else:
            col_max = slot[1][...]
        return _softmax_step(s, col_max, vt_ref[0, :, pl.ds(start, tk)], parts, *carry)

    parts = [(slice(h * V_ROWS, (h + 1) * V_ROWS), slice(2 * h * tq, (2 * h + 2) * tq))
             for h in range(2)]
    init = (jnp.full((1, 4 * tq), MASK_VALUE, F32),
            jnp.zeros((V_ROWS, 2 * tq), F32), jnp.zeros((V_ROWS, 2 * tq), F32))
    _, *accs = _causal_sweep(scores, consume, init, (qi * tq) // tk,
                             (sa_ref, ma_ref), (sb_ref, mb_ref))

    lam = lam_ref[0]
    halves = []
    for acc in accs:
        on = _normalised(acc)
        o = on[:, :tq] - lam * on[:, tq:]
        halves.append(o * lax.rsqrt(jnp.mean(o * o, axis=0, keepdims=True) + RMS_EPS))
    o_ref[0] = (jnp.concatenate(halves, axis=0) * g_ref[...]).T.astype(o_ref.dtype)


def _diff_attn_call(lam, qd, kd, vdt, g_col, tq, tk):
    b, s, _ = qd.shape
    pairs = DIFF_HEADS // 2
    return pl.pallas_call(
        functools.partial(_diff_attn_kernel, tq=tq, tk=tk),
        grid=(b, pairs, s // tq),
        in_specs=[pl.BlockSpec(memory_space=pltpu.SMEM),
                  pl.BlockSpec((1, tq, LANES), lambda bi, p, qi: (bi, qi, p)),
                  pl.BlockSpec((1, s, 2 * LANES), lambda bi, p, qi: (bi, 0, p)),
                  pl.BlockSpec((1, 2 * V_ROWS, s), lambda bi, p, qi: (bi, p, 0)),
                  pl.BlockSpec((LANES, 1), lambda bi, p, qi: (0, 0))],
        out_specs=pl.BlockSpec((1, tq, LANES), lambda bi, p, qi: (bi, qi, p)),
        out_shape=jax.ShapeDtypeStruct((b, s, DIFF_WIDTH), BF16),
        scratch_shapes=[pltpu.VMEM((4 * tq, 2 * LANES), BF16), pltpu.VMEM((tk, 4 * tq), F32),
                        pltpu.VMEM((tk, 4 * tq), F32), pltpu.VMEM((1, 4 * tq), F32),
                        pltpu.VMEM((1, 4 * tq), F32)],
        compiler_params=_params("parallel", "parallel", "arbitrary"),
        name="diff_attn",
    )(lam, qd, kd, vdt, g_col)


def _mla_attn_kernel(q_ref, k_ref, vt_ref, o_ref, sa_ref, sb_ref, ma_ref, mb_ref, *, tq, tk,
                     heads):
    qi = pl.program_id(2)
    slabs = [slice(hh * MLA_SLAB, (hh + 1) * MLA_SLAB) for hh in range(heads)]

    def scores(j, slot):
        start = pl.multiple_of(j * tk, tk)
        for hh, sl in enumerate(slabs):
            s = lax.dot_general(k_ref[0, pl.ds(start, tk), sl], q_ref[0, :, sl], _NT,
                                preferred_element_type=F32)
            slot[0][hh] = s
            slot[1][hh] = jnp.max(s, axis=0, keepdims=True)

    def consume(j, slot, carry, masked):
        start = pl.multiple_of(j * tk, tk)
        vtc = vt_ref[0, :, pl.ds(start, tk)]
        out = []
        for hh in range(heads):
            s = slot[0][hh]
            if masked:
                s = _causal_mask(s, start, qi * tq, tq)
                col_max = jnp.max(s, axis=0, keepdims=True)
            else:
                col_max = slot[1][hh]
            rows = slice(hh * V_ROWS, (hh + 1) * V_ROWS)
            out.extend(_softmax_step(s, col_max, vtc, [(rows, slice(None))],
                                     *carry[2 * hh:2 * hh + 2]))
        return tuple(out)

    init = (jnp.full((1, tq), MASK_VALUE, F32), jnp.zeros((V_ROWS, tq), F32)) * heads
    carry = _causal_sweep(scores, consume, init, (qi * tq) // tk,
                          (sa_ref, ma_ref), (sb_ref, mb_ref))
    o = jnp.concatenate([_normalised(carry[2 * hh + 1]) for hh in range(heads)], axis=0)
    o_ref[0] = o.T.astype(o_ref.dtype)


def _mla_attn_call(qm, km, vmt, tq, tk, heads):
    b, s, _ = qm.shape
    groups = MLA_HEADS // heads
    return pl.pallas_call(
        functools.partial(_mla_attn_kernel, tq=tq, tk=tk, heads=heads),
        grid=(b, groups, s // tq),
        in_specs=[pl.BlockSpec((1, tq, heads * MLA_SLAB), lambda bi, p, qi: (bi, qi, p)),
                  pl.BlockSpec((1, s, heads * MLA_SLAB), lambda bi, p, qi: (bi, 0, p)),
                  pl.BlockSpec((1, heads * V_ROWS, s), lambda bi, p, qi: (bi, p, 0))],
        out_specs=pl.BlockSpec((1, tq, heads * MLA_V_DIM), lambda bi, p, qi: (bi, qi, p)),
        out_shape=jax.ShapeDtypeStruct((b, s, MLA_WIDTH), BF16),
        scratch_shapes=[pltpu.VMEM((heads, tk, tq), F32), pltpu.VMEM((heads, tk, tq), F32),
                        pltpu.VMEM((heads, 1, tq), F32), pltpu.VMEM((heads, 1, tq), F32)],
        compiler_params=_params("parallel", "parallel", "arbitrary"),
        name="mla_attn",
    )(qm, km, vmt)


def _mem_kv_kernel(mem_ref, g_ref, w_ref, k_ref, v_ref):
    kv = _dot(_rms(mem_ref[0], g_ref[...]).astype(BF16), w_ref[...])
    k_ref[0] = kv[:, :MEM_WIDTH].astype(BF16)
    v_ref[0] = kv[:, MEM_WIDTH:].astype(BF16)


def _mem_kv_call(mem, g, w_kv):
    b, n_mem, d = mem.shape
    blk = pl.BlockSpec((1, n_mem, MEM_WIDTH), lambda bi: (bi, 0, 0))
    return pl.pallas_call(
        _mem_kv_kernel,
        grid=(b,),
        in_specs=[pl.BlockSpec((1, n_mem, d), lambda bi: (bi, 0, 0)),
                  pl.BlockSpec(g.shape, lambda bi: (0, 0)),
                  pl.BlockSpec(w_kv.shape, lambda bi: (0, 0))],
        out_specs=[blk, blk],
        out_shape=[jax.ShapeDtypeStruct((b, n_mem, MEM_WIDTH), BF16)] * 2,
        compiler_params=_params("parallel"),
        name="mem_kv",
    )(mem, g, w_kv)


def _store_tile_rows(ref, x):
    n = x.shape[0]
    chunks = x.shape[1] // LANES
    for c in range(chunks):
        ref[pl.ds(c, n, stride=chunks), :] = x[:, c * LANES:(c + 1) * LANES]


def _load_tile_rows(ref, chunks):
    n = ref.shape[0] // chunks
    return jnp.concatenate([ref[pl.ds(c, n, stride=chunks), :] for c in range(chunks)], axis=1)


def _tile_row(ref, r, chunks):
    return ref.at[pl.ds(pl.multiple_of(r * chunks, chunks), chunks)]


def _mix_kernel(x_ref, d_ref, m_ref, gm_ref, wo_ref, gc_ref, wq_ref, km_ref, vm_ref, wmo_ref,
                gf_ref, wr_ref, br_ref, x2_ref, h3_ref, r_ref, rt_ref, cnt_ref, run_ref):
    rows = x_ref.shape[0]

    @pl.when(pl.program_id(0) == 0)
    def _():
        run_ref[...] = jnp.zeros_like(run_ref)

    mla_n = _rms(m_ref[...].astype(F32), gm_ref[...]).astype(BF16)
    x1 = (x_ref[...] + _dot(d_ref[...], wo_ref[0:DIFF_WIDTH, :])
          + _dot(mla_n, wo_ref[DIFF_WIDTH:DIFF_WIDTH + MLA_WIDTH, :]))

    qx = _dot(_rms(x1, gc_ref[...]).astype(BF16), wq_ref[...])
    heads = []
    for h in range(MEM_HEADS):
        sl = slice(h * MEM_HEAD_DIM, (h + 1) * MEM_HEAD_DIM)
        sc = lax.dot_general(qx[:, sl].astype(BF16), km_ref[0, :, sl], _NT,
                             preferred_element_type=F32) * MEM_HEAD_DIM ** -0.5
        p = jnp.exp(sc - jnp.max(sc, axis=-1, keepdims=True))
        o = _dot(p.astype(BF16), vm_ref[0, :, sl]) * (1.0 / jnp.sum(p, axis=-1, keepdims=True))
        heads.append(o.astype(BF16))
    x2 = x1 + _dot(jnp.concatenate(heads, axis=1), wmo_ref[...])
    x2_ref[...] = x2

    h3 = _rms(x2, gf_ref[...])
    _store_tile_rows(h3_ref, h3)
    h_hi = h3.astype(BF16)
    h_lo = (h3 - h_hi.astype(F32)).astype(BF16)
    logits = (_dot(h_hi, wr_ref[0]) + _dot(h_lo, wr_ref[0]) + _dot(h_hi, wr_ref[1])
              + br_ref[...])
    lane = lax.broadcasted_iota(jnp.int32, (rows, LANES), 1).astype(F32)
    far = float(4 * LANES)

    def first_argmax(vals):
        top = jnp.max(vals, axis=-1, keepdims=True)
        return top, jnp.min(jnp.where(vals == top, lane, far), axis=-1, keepdims=True)

    g_logits = jnp.where(lane < N_GROUPS, logits, MASK_VALUE)
    g_top, g_sel = first_argmax(g_logits)
    g_gate = 1.0 / jnp.sum(jnp.exp(g_logits - g_top), axis=-1, keepdims=True)
    lo = ROUTER_LANE0 + EXPERTS_PER_GROUP * g_sel
    e_logits = jnp.where(lane >= lo, jnp.where(lane < lo + EXPERTS_PER_GROUP, logits, MASK_VALUE),
                         MASK_VALUE)
    top1, lane1 = first_argmax(e_logits)
    top2, lane2 = first_argmax(jnp.where(lane == lane1, MASK_VALUE, e_logits))
    t = jnp.exp(top2 - top1)
    gate1 = g_gate / (1.0 + t)
    gate2 = gate1 * t

    hot = jnp.where(lane == lane1, 1.0, 0.0) + jnp.where(lane == lane2, 1.0, 0.0)
    tri = (lax.broadcasted_iota(jnp.int32, (rows, rows), 1)
           < lax.broadcasted_iota(jnp.int32, (rows, rows), 0))
    before = _dot(jnp.where(tri, 1.0, 0.0).astype(BF16), hot.astype(BF16)) + run_ref[0:1, :]
    rank1 = jnp.sum(jnp.where(lane == lane1, before, 0.0), axis=-1, keepdims=True)
    rank2 = jnp.sum(jnp.where(lane == lane2, before, 0.0), axis=-1, keepdims=True)
    run_ref[...] = run_ref[...] + jnp.sum(hot, axis=0, keepdims=True)
    cnt_ref[...] = run_ref[...]

    fields = (lane1 - ROUTER_LANE0, lane2 - ROUTER_LANE0, gate1, gate2, rank1, rank2)
    out = jnp.zeros((rows, LANES), F32)
    for i, f in enumerate(fields):
        out = jnp.where(lane == float(i), f, out)
    r_ref[...] = out
    rt_ref[...] = out.T[0:8, :]


def _mix_call(x2d, diff, mla, gm, wo, gc, wq, kmem, vmem, wmo, gf, wr, br, seq):
    t, d = x2d.shape
    rows = min(MIX_ROWS, seq)
    per_seq = seq // rows
    row = lambda i: (i, 0)
    fixed = lambda i: (0, 0)
    full = lambda a: pl.BlockSpec(a.shape, lambda i: (0,) * a.ndim)
    memblk = pl.BlockSpec((1,) + kmem.shape[1:], lambda i: (i // per_seq, 0, 0))
    return pl.pallas_call(
        _mix_kernel,
        grid=(t // rows,),
        in_specs=[pl.BlockSpec((rows, d), row), pl.BlockSpec((rows, DIFF_WIDTH), row),
                  pl.BlockSpec((rows, MLA_WIDTH), row), full(gm), full(wo), full(gc), full(wq),
                  memblk, memblk, full(wmo), full(gf), full(wr), full(br)],
        out_specs=[pl.BlockSpec((rows, d), row), pl.BlockSpec((rows * d // LANES, LANES), row),
                   pl.BlockSpec((rows, LANES), row), pl.BlockSpec((8, rows), lambda i: (0, i)),
                   pl.BlockSpec((8, LANES), fixed)],
        out_shape=[jax.ShapeDtypeStruct((t, d), F32),
                   jax.ShapeDtypeStruct((t * d // LANES, LANES), F32),
                   jax.ShapeDtypeStruct((t, LANES), F32), jax.ShapeDtypeStruct((8, t), F32),
                   jax.ShapeDtypeStruct((8, LANES), F32)],
        scratch_shapes=[pltpu.VMEM((8, LANES), F32)],
        compiler_params=_params("arbitrary"),
        name="mix",
    )(x2d, diff, mla, gm, wo, gc, wq, kmem, vmem, wmo, gf, wr, br)


def _dispatch_kernel(d0_ref, d1_ref, pad_ref, h_ref, xs_ref, sem):
    rows = h_ref.shape[0] // ROW_CHUNKS
    pads = pad_ref.shape[-1]

    def issue(r, c):
        src = _tile_row(h_ref, r, ROW_CHUNKS)
        pltpu.make_async_copy(src, _tile_row(xs_ref, d0_ref[0, 0, r], ROW_CHUNKS), sem).start(
            priority=0)
        pltpu.make_async_copy(src, _tile_row(xs_ref, d1_ref[0, 0, r], ROW_CHUNKS), sem).start(
            priority=1)
        return c

    def fill(r, c):
        pltpu.make_async_copy(_tile_row(h_ref, lax.rem(r, rows), ROW_CHUNKS),
                              _tile_row(xs_ref, pad_ref[0, 0, r], ROW_CHUNKS), sem).start()
        return c

    lax.fori_loop(0, rows, issue, 0, unroll=8)
    lax.fori_loop(0, pads, fill, 0, unroll=8)
    counts = [rows, rows] + [rows] * (pads // rows) + [pads % rows] * (pads % rows > 0)
    for n in counts:
        pltpu.make_async_copy(h_ref.at[pl.ds(0, n * ROW_CHUNKS)],
                              xs_ref.at[pl.ds(0, n * ROW_CHUNKS)], sem).wait()


def _dispatch_call(d0, d1, pad_dest, h3, n_slots, rows):
    t = h3.shape[0] // ROW_CHUNKS
    steps = t // rows
    pads = pad_dest.shape[0] // steps
    assert pads * steps == pad_dest.shape[0]
    d0, d1 = d0.reshape(steps, 1, rows), d1.reshape(steps, 1, rows)
    idx = pl.BlockSpec((1, 1, rows), lambda i: (i, 0, 0), memory_space=pltpu.SMEM)
    return pl.pallas_call(
        _dispatch_kernel,
        grid=(steps,),
        in_specs=[idx, idx,
                  pl.BlockSpec((1, 1, pads), lambda i: (i, 0, 0), memory_space=pltpu.SMEM),
                  pl.BlockSpec((rows * ROW_CHUNKS, LANES), lambda i: (i, 0))],
        out_specs=pl.BlockSpec(memory_space=pl.ANY),
        out_shape=jax.ShapeDtypeStruct((n_slots * ROW_CHUNKS, LANES), h3.dtype),
        scratch_shapes=[pltpu.SemaphoreType.DMA(())],
        compiler_params=_params("arbitrary"),
        name="dispatch",
    )(d0, d1, pad_dest.reshape(steps, 1, pads), h3)


def _expert_kernel(be_ref, x_ref, wg_ref, wu_ref, wd_ref, y_ref, wg_bf, wu_bf, wd_bf):
    i = pl.program_id(0)

    @pl.when((i == 0) | (be_ref[i] != be_ref[jnp.maximum(i - 1, 0)]))
    def _():
        wg_bf[...] = wg_ref[0].astype(BF16)
        wu_bf[...] = wu_ref[0].astype(BF16)
        wd_bf[...] = wd_ref[0].astype(BF16)

    x = _load_tile_rows(x_ref, ROW_CHUNKS).astype(BF16)
    gate = _dot(x, wg_bf[...])
    up = _dot(x, wu_bf[...])
    hidden = gate * (1.0 / (1.0 + jnp.exp(-gate))) * up
    _store_tile_rows(y_ref, _dot(hidden.astype(BF16), wd_bf[...]))


def _expert_call(block_expert, xs, wg, wu, wd, rows):
    _, d, ff = wg.shape
    n_slots = xs.shape[0] // ROW_CHUNKS
    grid_spec = pltpu.PrefetchScalarGridSpec(
        num_scalar_prefetch=1,
        grid=(n_slots // rows,),
        in_specs=[pl.BlockSpec((rows * ROW_CHUNKS, LANES), lambda i, be: (i, 0)),
                  pl.BlockSpec((1, d, ff), lambda i, be: (be[i], 0, 0)),
                  pl.BlockSpec((1, d, ff), lambda i, be: (be[i], 0, 0)),
                  pl.BlockSpec((1, ff, d), lambda i, be: (be[i], 0, 0))],
        out_specs=pl.BlockSpec((rows * ROW_CHUNKS, LANES), lambda i, be: (i, 0)),
        scratch_shapes=[pltpu.VMEM((d, ff), BF16), pltpu.VMEM((d, ff), BF16),
                        pltpu.VMEM((ff, d), BF16)],
    )
    return pl.pallas_call(
        _expert_kernel,
        grid_spec=grid_spec,
        out_shape=jax.ShapeDtypeStruct(xs.shape, F32),
        compiler_params=_params("arbitrary"),
        name="experts",
    )(block_expert, xs, wg, wu, wd)


def _combine_kernel(d0_ref, d1_ref, n0_ref, n1_ref, x_ref, r_ref, g_ref, y_ref, o_ref, buf0, buf1,
                    sems):
    rows = x_ref.shape[0]
    i = pl.program_id(0)
    slot = i & 1

    def start_gathers(i0_ref, i1_ref, sl):
        def issue(r, c):
            pltpu.make_async_copy(_tile_row(y_ref, i0_ref[0, 0, r], ROW_CHUNKS),
                                  _tile_row(buf0.at[sl], r, ROW_CHUNKS),
                                  sems.at[sl]).start(priority=0)
            pltpu.make_async_copy(_tile_row(y_ref, i1_ref[0, 0, r], ROW_CHUNKS),
                                  _tile_row(buf1.at[sl], r, ROW_CHUNKS),
                                  sems.at[sl]).start(priority=1)
            return c

        lax.fori_loop(0, rows, issue, 0, unroll=8)

    @pl.when(i == 0)
    def _():
        start_gathers(d0_ref, d1_ref, slot)

    @pl.when(i + 1 < pl.num_programs(0))
    def _():
        start_gathers(n0_ref, n1_ref, 1 - slot)

    for buf in (buf0, buf1):
        pltpu.make_async_copy(y_ref.at[pl.ds(0, rows * ROW_CHUNKS)], buf.at[slot],
                              sems.at[slot]).wait()
    gates = r_ref[...]
    x3 = (x_ref[...] + gates[:, 2:3] * _load_tile_rows(buf0.at[slot], ROW_CHUNKS)
          + gates[:, 3:4] * _load_tile_rows(buf1.at[slot], ROW_CHUNKS))
    o_ref[...] = _rms(x3, g_ref[...])


def _combine_call(d0, d1, x2, router, g, y, rows):
    t, d = x2.shape
    d0, d1 = d0.reshape(t // rows, 1, rows), d1.reshape(t // rows, 1, rows)
    last = t // rows - 1
    idx = pl.BlockSpec((1, 1, rows), lambda i: (i, 0, 0), memory_space=pltpu.SMEM)
    nxt = pl.BlockSpec((1, 1, rows), lambda i: (jnp.minimum(i + 1, last), 0, 0),
                       memory_space=pltpu.SMEM)
    row = lambda i: (i, 0)
    return pl.pallas_call(
        _combine_kernel,
        grid=(t // rows,),
        in_specs=[idx, idx, nxt, nxt, pl.BlockSpec((rows, d), row),
                  pl.BlockSpec((rows, LANES), row), pl.BlockSpec(g.shape, lambda i: (0, 0)),
                  pl.BlockSpec(memory_space=pl.ANY)],
        out_specs=pl.BlockSpec((rows, d), row),
        out_shape=jax.ShapeDtypeStruct((t, d), F32),
        scratch_shapes=[pltpu.VMEM((2, rows * ROW_CHUNKS, LANES), F32),
                        pltpu.VMEM((2, rows * ROW_CHUNKS, LANES), F32),
                        pltpu.SemaphoreType.DMA((2,))],
        compiler_params=_params("arbitrary"),
        name="combine",
    )(d0, d1, d0, d1, x2, router, g, y)


def _slab(cols, lo):
    return jnp.pad(cols, ((0, 0), (lo, MLA_SLAB - lo - cols.shape[1])))


def _layer_weights(w_in, w_uq, w_ukv):
    half = MLA_ROPE_DIM // 2
    o = 3 * DIFF_WIDTH + MLA_Q_RANK + MLA_KV_RANK
    kpe1, kpe2 = w_in[:, o:o + half], w_in[:, o + half:o + 2 * half]
    w_all = jnp.concatenate(
        [w_in[:, :o],
         _slab(jnp.concatenate([kpe1, kpe2], axis=1), MLA_NOPE_DIM),
         _slab(jnp.concatenate([kpe2, kpe1], axis=1), MLA_NOPE_DIM)], axis=1).astype(BF16)
    q_main, q_swap, k_nope, v_cols = [], [], [], []
    for h in range(MLA_HEADS):
        c = w_uq[:, h * MLA_QK_DIM:(h + 1) * MLA_QK_DIM]
        pe1 = c[:, MLA_NOPE_DIM:MLA_NOPE_DIM + half]
        pe2 = c[:, MLA_NOPE_DIM + half:]
        q_main.append(_slab(c, 0))
        q_swap.append(_slab(jnp.concatenate([pe2, pe1], axis=1), MLA_NOPE_DIM))
        kv = w_ukv[:, h * (MLA_NOPE_DIM + MLA_V_DIM):(h + 1) * (MLA_NOPE_DIM + MLA_V_DIM)]
        k_nope.append(_slab(kv[:, :MLA_NOPE_DIM], 0))
        v_cols.append(kv[:, MLA_NOPE_DIM:])
    wq2 = jnp.concatenate(q_main + q_swap, axis=1).astype(BF16)
    wkv2 = jnp.concatenate(k_nope + v_cols, axis=1).astype(BF16)
    return w_all, wq2, wkv2


def _rope_rows():
    inv_freq = ROPE_THETA ** (-jnp.arange(0, MLA_ROPE_DIM, 2, dtype=F32) / MLA_ROPE_DIM)
    half = jnp.ones_like(inv_freq)
    place = lambda a, b: jnp.pad(jnp.concatenate([a, b]),
                                 (MLA_NOPE_DIM, MLA_SLAB - MLA_QK_DIM))
    return jnp.stack([place(inv_freq, inv_freq), place(-half, half)])


def kernel(x, mem, positions, norm_mix_g, w_in, diff_lambda_q1, diff_lambda_k1, diff_lambda_q2, diff_lambda_k2, diff_out_g, mla_q_norm_g, w_mla_uq, mla_kv_norm_g, w_mla_ukv, mla_out_g, w_out, norm_cross_g, norm_mem_g, w_mem_q, w_mem_kv, w_mem_o, norm_ffn_g, w_group_router, b_group_router, w_expert_router, b_expert_router, w_expert_gate, w_expert_up, w_expert_down, norm_final_g):
    b, s, d = x.shape
    t = b * s
    depth = w_in.shape[0]
    row2 = lambda v: v.reshape(1, -1).astype(F32)
    pos_f = positions.astype(F32)
    slopes = 2.0 ** (-8.0 * jnp.arange(1, DIFF_HEADS + 1, dtype=F32) / DIFF_HEADS)
    pos_rel = (pos_f - pos_f[:, :1]).reshape(t, 1)
    slope_rows = jnp.pad(jnp.repeat((LOG2E * slopes).reshape(DIFF_HEADS // 2, 2), 3, axis=1),
                         ((0, 0), (0, LANES - 6)))

    diff_tq, mla_tq = min(DIFF_TQ, s), min(MLA_TQ, s)
    attn_tk = min(ATTN_TK, s)
    blk = EXPERT_ROWS
    n_assign = 2 * t
    n_blocks = -(-(n_assign + N_EXPERTS * (blk - 1)) // blk)
    n_slots = n_blocks * blk

    assert depth == 1, "single-layer problem: the combine kernel applies the final RMSNorm"
    l = 0
    x2d = x.reshape(t, d)
    w_all, wq2, wkv2 = _layer_weights(w_in[l], w_mla_uq[l], w_mla_ukv[l])
    qd, kd, vdt, qm, km, vmt = _proj_call(
        x2d, row2(norm_mix_g[l]), w_all, row2(mla_q_norm_g[l]), wq2,
        row2(mla_kv_norm_g[l]), wkv2, pos_f.reshape(t, 1), _rope_rows(), pos_rel, slope_rows, s)

    lambda_init = 0.8 - 0.6 * math.exp(-0.3 * l)
    lam = (jnp.exp(jnp.sum(diff_lambda_q1[l] * diff_lambda_k1[l]))
           - jnp.exp(jnp.sum(diff_lambda_q2[l] * diff_lambda_k2[l])) + lambda_init)
    gain_col = jnp.tile(diff_out_g[l] * (1.0 - lambda_init), 2).reshape(-1, 1).astype(F32)
    shp = lambda a: a.reshape(b, s, a.shape[-1])
    diff_out = _diff_attn_call(lam.reshape(1).astype(F32), shp(qd), shp(kd), vdt, gain_col,
                               diff_tq, attn_tk)
    mla_out = _mla_attn_call(shp(qm), shp(km), vmt, mla_tq, attn_tk, MLA_HEADS_PER_STEP)

    kmem, vmem = _mem_kv_call(mem, row2(norm_mem_g[l]), w_mem_kv[l].astype(BF16))
    w_router = jnp.pad(
        jnp.concatenate([w_group_router[l], w_expert_router[l]], axis=1),
        ((0, 0), (0, LANES - N_GROUPS - N_EXPERTS)))
    w_router_hi = w_router.astype(BF16)
    w_router = jnp.stack([w_router_hi, (w_router - w_router_hi.astype(F32)).astype(BF16)])
    b_router = jnp.pad(jnp.concatenate([b_group_router[l], b_expert_router[l]]),
                       (0, LANES - N_GROUPS - N_EXPERTS))
    x2, h3, router, router_t, counts = _mix_call(
        x2d, diff_out.reshape(t, DIFF_WIDTH), mla_out.reshape(t, MLA_WIDTH),
        row2(mla_out_g[l]), w_out[l].astype(BF16), row2(norm_cross_g[l]),
        w_mem_q[l].astype(BF16), kmem, vmem, w_mem_o[l].astype(BF16),
        row2(norm_ffn_g[l]), w_router, row2(b_router), s)

    cnt = counts[0, ROUTER_LANE0:ROUTER_LANE0 + N_EXPERTS].astype(jnp.int32)
    padded = ((cnt + blk - 1) // blk) * blk
    seg_end = jnp.cumsum(padded)
    seg_start = seg_end - padded
    ridx = router_t.astype(jnp.int32)
    expert_ids = jnp.arange(N_EXPERTS, dtype=jnp.int32)[:, None]

    def slot_of(choice):
        start = jnp.sum(jnp.where(ridx[choice][None, :] == expert_ids, seg_start[:, None], 0),
                        axis=0)
        return start + ridx[4 + choice]

    dest0, dest1 = slot_of(0), slot_of(1)
    block_start = jnp.arange(n_blocks, dtype=jnp.int32) * blk
    block_expert = jnp.minimum(
        jnp.sum((seg_end[None, :] <= block_start[:, None]).astype(jnp.int32), axis=1),
        N_EXPERTS - 1)

    gap_first = jnp.concatenate([seg_start + cnt, seg_end[-1:]])
    gap_len = jnp.concatenate([padded - cnt, n_slots - seg_end[-1:]])
    gap_end = jnp.cumsum(gap_len)
    k = jnp.arange(n_slots - n_assign, dtype=jnp.int32)
    gap_of_k = jnp.sum((gap_end[None, :] <= k[:, None]).astype(jnp.int32), axis=1)
    gap_ids = jnp.arange(N_EXPERTS + 1, dtype=jnp.int32)[None, :]
    pad_dest = k + jnp.sum(jnp.where(gap_of_k[:, None] == gap_ids,
                                     (gap_first - (gap_end - gap_len))[None, :], 0), axis=1)

    assert d == ROW_CHUNKS * LANES
    xs = _dispatch_call(dest0, dest1, pad_dest, h3, n_slots, min(DISPATCH_ROWS, t))
    y = _expert_call(block_expert, xs, w_expert_gate[l], w_expert_up[l], w_expert_down[l], blk)
    out = _combine_call(dest0, dest1, x2, router, row2(norm_final_g), y, min(COMBINE_ROWS, t))
    return out.reshape(b, s, d)
```

```python
import functools
import math

import jax
import jax.numpy as jnp
from jax import lax
from jax.experimental import pallas as pl
from jax.experimental.pallas import tpu as pltpu

F32 = jnp.float32
BF16 = jnp.bfloat16

RMS_EPS = 1e-6
ROPE_THETA = 10000.0
MASK_VALUE = -1e30

DIFF_HEADS = 8
DIFF_HALF_DIM = 32
DIFF_V_DIM = 64
DIFF_WIDTH = DIFF_HEADS * DIFF_V_DIM

MLA_HEADS = 8
MLA_Q_RANK = 256
MLA_KV_RANK = 128
MLA_NOPE_DIM = 64
MLA_ROPE_DIM = 32
MLA_QK_DIM = MLA_NOPE_DIM + MLA_ROPE_DIM
MLA_V_DIM = 64
MLA_WIDTH = MLA_HEADS * MLA_V_DIM

MEM_HEADS = 4
MEM_HEAD_DIM = 128
MEM_WIDTH = MEM_HEADS * MEM_HEAD_DIM

N_GROUPS = 4
EXPERTS_PER_GROUP = 8
N_EXPERTS = N_GROUPS * EXPERTS_PER_GROUP
EXPERT_FF = 256

LANES = 128
MLA_SLAB = LANES
ROUTER_LANE0 = N_GROUPS
BF16_SUBLANES = 16
V_ONES_ROW = DIFF_V_DIM
V_ROWS = DIFF_V_DIM + BF16_SUBLANES
ROW_CHUNKS = 8

PROJ_ROWS = 512
MIX_ROWS = 1024
DISPATCH_ROWS = 1024
COMBINE_ROWS = 256
EXPERT_ROWS = 512
ATTN_TK = 512
DIFF_TQ = 512
MLA_TQ = 512
MLA_HEADS_PER_STEP = 4
LOG2E = math.log2(math.e)
V7X_VMEM_BYTES = 64 * 1024 * 1024
VMEM_LIMIT = V7X_VMEM_BYTES * 3 // 4

_NT = (((1,), (1,)), ((), ()))


def _rms(x, gain):
    return x * lax.rsqrt(jnp.mean(x * x, axis=-1, keepdims=True) + RMS_EPS) * gain


def _dot(a, b):
    return jnp.dot(a, b, preferred_element_type=F32)


def _params(*sem):
    return pltpu.CompilerParams(dimension_semantics=sem, vmem_limit_bytes=VMEM_LIMIT)


def _values_transposed(v, heads):
    vt = v.T
    n = v.shape[0]
    tail = jnp.where(lax.broadcasted_iota(jnp.int32, (V_ROWS - V_ONES_ROW, n), 0) == 0, 1.0, 0.0)
    pieces = []
    for h in range(heads):
        pieces += [vt[h * V_ONES_ROW:(h + 1) * V_ONES_ROW], tail]
    return jnp.concatenate(pieces, axis=0).astype(BF16)


def _proj_kernel(x_ref, g_ref, w_ref, gq_ref, wq_ref, gkv_ref, wkv_ref, abs_ref, rope_ref,
                 pos_ref, slope_ref, qd_ref, kd_ref, vd_ref, qm_ref, km_ref, vm_ref):
    xn = _rms(x_ref[...], g_ref[...]).astype(BF16)
    r = _dot(xn, w_ref[...])
    w = DIFF_WIDTH
    qd_ref[...] = (r[:, 0:w] * (DIFF_HALF_DIM ** -0.5 * LOG2E)).astype(BF16)
    lane = lax.broadcasted_iota(jnp.int32, (x_ref.shape[0], LANES), 1)
    for p in range(DIFF_HEADS // 2):
        bias = pos_ref[...] * slope_ref[p:p + 1, :]
        hi = bias.astype(BF16)
        mid = (bias - hi.astype(F32)).astype(BF16)
        low = (bias - hi.astype(F32) - mid.astype(F32)).astype(BF16)
        term = jnp.where((lane == 0) | (lane == 3), hi, jnp.where((lane == 1) | (lane == 4), mid, low))
        kd_ref[:, 2 * p * LANES:(2 * p + 1) * LANES] = r[:, w + p * LANES:w + (p + 1) * LANES].astype(BF16)
        kd_ref[:, (2 * p + 1) * LANES:(2 * p + 2) * LANES] = term
    vd_ref[0] = _values_transposed(r[:, 2 * w:3 * w], DIFF_HEADS)
    o = 3 * w
    cq = r[:, o:o + MLA_Q_RANK]
    o += MLA_Q_RANK
    ckv = r[:, o:o + MLA_KV_RANK]
    o += MLA_KV_RANK
    ang = abs_ref[...] * rope_ref[0:1, :]
    cos = jnp.cos(ang)
    sin = jnp.sin(ang) * rope_ref[1:2, :]
    kpe = r[:, o:o + MLA_SLAB] * cos + r[:, o + MLA_SLAB:o + 2 * MLA_SLAB] * sin
    q2 = _dot(_rms(cq, gq_ref[...]).astype(BF16), wq_ref[...])
    kv2 = _dot(_rms(ckv, gkv_ref[...]).astype(BF16), wkv_ref[...])
    sw = MLA_HEADS * MLA_SLAB
    for h in range(MLA_HEADS):
        lo = h * MLA_SLAB
        qh = q2[:, lo:lo + MLA_SLAB] * cos + q2[:, sw + lo:sw + lo + MLA_SLAB] * sin
        qm_ref[:, lo:lo + MLA_SLAB] = (qh * (MLA_QK_DIM ** -0.5 * LOG2E)).astype(BF16)
        km_ref[:, lo:lo + MLA_SLAB] = (kv2[:, lo:lo + MLA_SLAB] + kpe).astype(BF16)
    vm_ref[0] = _values_transposed(kv2[:, sw:sw + MLA_WIDTH], MLA_HEADS)


def _proj_call(x2d, g, w_all, gq, wq2, gkv, wkv2, pos_abs, rope_rows, pos_rel, slope_rows, seq):
    t, d = x2d.shape
    rows = min(PROJ_ROWS, seq)
    per_seq = seq // rows
    row = lambda i: (i, 0)
    fixed = lambda i: (0, 0)
    full = lambda a: pl.BlockSpec(a.shape, fixed)
    row_out = lambda n: (pl.BlockSpec((rows, n), row), jax.ShapeDtypeStruct((t, n), BF16))
    vt_out = lambda n: (pl.BlockSpec((1, n, rows), lambda i: (i // per_seq, 0, i % per_seq)),
                        jax.ShapeDtypeStruct((t // seq, n, seq), BF16))
    outs = [row_out(DIFF_WIDTH), row_out(DIFF_HEADS * LANES), vt_out(DIFF_HEADS * V_ROWS),
            row_out(MLA_HEADS * MLA_SLAB), row_out(MLA_HEADS * MLA_SLAB),
            vt_out(MLA_HEADS * V_ROWS)]
    return pl.pallas_call(
        _proj_kernel,
        grid=(t // rows,),
        in_specs=[pl.BlockSpec((rows, d), row), full(g), full(w_all), full(gq), full(wq2),
                  full(gkv), full(wkv2), pl.BlockSpec((rows, 1), row), full(rope_rows),
                  pl.BlockSpec((rows, 1), row), full(slope_rows)],
        out_specs=[spec for spec, _ in outs],
        out_shape=[shape for _, shape in outs],
        compiler_params=_params("parallel"),
        name="proj",
    )(x2d, g, w_all, gq, wq2, gkv, wkv2, pos_abs, rope_rows, pos_rel, slope_rows)


def _softmax_step(s, col_max, vt_chunk, parts, m, *accs):
    m_new = jnp.maximum(m, col_max)
    alpha = jnp.exp2(m - m_new)
    pb = jnp.exp2(s - m_new).astype(BF16)
    accs = tuple(alpha[:, cols] * acc + _dot(vt_chunk[rows], pb[:, cols])
                 for (rows, cols), acc in zip(parts, accs))
    return (m_new,) + accs


def _normalised(acc):
    return acc[:V_ONES_ROW] * (1.0 / acc[V_ONES_ROW:V_ONES_ROW + 1])


def _causal_mask(s, key0, query0, tq):
    key = key0 + lax.broadcasted_iota(jnp.int32, s.shape, 0)
    query = query0 + (lax.broadcasted_iota(jnp.int32, s.shape, 1) & (tq - 1))
    return jnp.where(key <= query, s, MASK_VALUE)


def _causal_sweep(scores, consume, carry, n_full, slot_a, slot_b):
    scores(0, slot_a)

    def pair(i, c):
        j = 2 * i
        scores(j + 1, slot_b)
        c = consume(j, slot_a, c, False)
        scores(j + 2, slot_a)
        return consume(j + 1, slot_b, c, False)

    carry = lax.fori_loop(0, lax.shift_right_logical(n_full, 1), pair, carry)

    def odd_tail(c):
        scores(n_full, slot_b)
        c = consume(n_full - 1, slot_a, c, False)
        return consume(n_full, slot_b, c, True)

    def even_tail(c):
        return consume(n_full, slot_a, c, True)

    return lax.cond((n_full & 1) == 1, odd_tail, even_tail, carry)


def _diff_attn_kernel(lam_ref, q_ref, k_ref, vt_ref, g_ref, o_ref, qs_ref, sa_ref, sb_ref,
                      ma_ref, mb_ref, *, tq, tk):
    qi = pl.program_id(2)
    q = q_ref[0]
    lane = lax.broadcasted_iota(jnp.int32, (tq, LANES), 1)
    for v in range(4):
        keep = (lane >= v * DIFF_HALF_DIM) & (lane < (v + 1) * DIFF_HALF_DIM)
        ones = (lane >= 3 * (v // 2)) & (lane < 3 * (v // 2) + 3)
        qs_ref[v * tq:(v + 1) * tq, 0:LANES] = jnp.where(keep, q, jnp.zeros_like(q))
        qs_ref[v * tq:(v + 1) * tq, LANES:2 * LANES] = jnp.where(ones, 1.0, 0.0).astype(BF16)

    def scores(j, slot):
        start = pl.multiple_of(j * tk, tk)
        s = lax.dot_general(k_ref[0, pl.ds(start, tk), :], qs_ref[...], _NT,
                            preferred_element_type=F32)
        slot[0][...] = s
        slot[1][...] = jnp.max(s, axis=0, keepdims=True)

    def consume(j, slot, carry, masked):
        start = pl.multiple_of(j * tk, tk)
        s = slot[0][...]
        if masked:
            s = _causal_mask(s, start, qi * tq, tq)
            col_max = jnp.max(s, axis=0, keepdims=True)
        else:
            col_max = slot[1][...]
        return _softmax_step(s, col_max, vt_ref[0, :, pl.ds(start, tk)], parts, *carry)

    parts = [(slice(h * V_ROWS, (h + 1) * V_ROWS), slice(2 * h * tq, (2 * h + 2) * tq))
             for h in range(2)]
    init = (jnp.full((1, 4 * tq), MASK_VALUE, F32),
            jnp.zeros((V_ROWS, 2 * tq), F32), jnp.zeros((V_ROWS, 2 * tq), F32))
    _, *accs = _causal_sweep(scores, consume, init, (qi * tq) // tk,
                             (sa_ref, ma_ref), (sb_ref, mb_ref))

    lam = lam_ref[0]
    halves = []
    for acc in accs:
        on = _normalised(acc)
        o = on[:, :tq] - lam * on[:, tq:]
        halves.append(o * lax.rsqrt(jnp.mean(o * o, axis=0, keepdims=True) + RMS_EPS))
    o_ref[0] = (jnp.concatenate(halves, axis=0) * g_ref[...]).T.astype(o_ref.dtype)


def _diff_attn_call(lam, qd, kd, vdt, g_col, tq, tk):
    b, s, _ = qd.shape
    pairs = DIFF_HEADS // 2
    return pl.pallas_call(
        functools.partial(_diff_attn_kernel, tq=tq, tk=tk),
        grid=(b, pairs, s // tq),
        in_specs=[pl.BlockSpec(memory_space=pltpu.SMEM),
                  pl.BlockSpec((1, tq, LANES), lambda bi, p, qi: (bi, qi, p)),
                  pl.BlockSpec((1, s, 2 * LANES), lambda bi, p, qi: (bi, 0, p)),
                  pl.BlockSpec((1, 2 * V_ROWS, s), lambda bi, p, qi: (bi, p, 0)),
                  pl.BlockSpec((LANES, 1), lambda bi, p, qi: (0, 0))],
        out_specs=pl.BlockSpec((1, tq, LANES), lambda bi, p, qi: (bi, qi, p)),
        out_shape=jax.ShapeDtypeStruct((b, s, DIFF_WIDTH), BF16),
        scratch_shapes=[pltpu.VMEM((4 * tq, 2 * LANES), BF16), pltpu.VMEM((tk, 4 * tq), F32),
                        pltpu.VMEM((tk, 4 * tq), F32), pltpu.VMEM((1, 4 * tq), F32),
                        pltpu.VMEM((1, 4 * tq), F32)],
        compiler_params=_params("parallel", "parallel", "arbitrary"),
        name="diff_attn",
    )(lam, qd, kd, vdt, g_col)


def _mla_attn_kernel(q_ref, k_ref, vt_ref, o_ref, sa_ref, sb_ref, ma_ref, mb_ref, *, tq, tk,
                     heads):
    qi = pl.program_id(2)
    slabs = [slice(hh * MLA_SLAB, (hh + 1) * MLA_SLAB) for hh in range(heads)]

    def scores(j, slot):
        start = pl.multiple_of(j * tk, tk)
        for hh, sl in enumerate(slabs):
            s = lax.dot_general(k_ref[0, pl.ds(start, tk), sl], q_ref[0, :, sl], _NT,
                                preferred_element_type=F32)
            slot[0][hh] = s
            slot[1][hh] = jnp.max(s, axis=0, keepdims=True)

    def consume(j, slot, carry, masked):
        start = pl.multiple_of(j * tk, tk)
        vtc = vt_ref[0, :, pl.ds(start, tk)]
        out = []
        for hh in range(heads):
            s = slot[0][hh]
            if masked:
                s = _causal_mask(s, start, qi * tq, tq)
                col_max = jnp.max(s, axis=0, keepdims=True)
            else:
                col_max = slot[1][hh]
            rows = slice(hh * V_ROWS, (hh + 1) * V_ROWS)
            out.extend(_softmax_step(s, col_max, vtc, [(rows, slice(None))],
                                     *carry[2 * hh:2 * hh + 2]))
        return tuple(out)

    init = (jnp.full((1, tq), MASK_VALUE, F32), jnp.zeros((V_ROWS, tq), F32)) * heads
    carry = _causal_sweep(scores, consume, init, (qi * tq) // tk,
                          (sa_ref, ma_ref), (sb_ref, mb_ref))
    o = jnp.concatenate([_normalised(carry[2 * hh + 1]) for hh in range(heads)], axis=0)
    o_ref[0] = o.T.astype(o_ref.dtype)


def _mla_attn_call(qm, km, vmt, tq, tk, heads):
    b, s, _ = qm.shape
    groups = MLA_HEADS // heads
    return pl.pallas_call(
        functools.partial(_mla_attn_kernel, tq=tq, tk=tk, heads=heads),
        grid=(b, groups, s // tq),
        in_specs=[pl.BlockSpec((1, tq, heads * MLA_SLAB), lambda bi, p, qi: (bi, qi, p)),
                  pl.BlockSpec((1, s, heads * MLA_SLAB), lambda bi, p, qi: (bi, 0, p)),
                  pl.BlockSpec((1, heads * V_ROWS, s), lambda bi, p, qi: (bi, p, 0))],
        out_specs=pl.BlockSpec((1, tq, heads * MLA_V_DIM), lambda bi, p, qi: (bi, qi, p)),
        out_shape=jax.ShapeDtypeStruct((b, s, MLA_WIDTH), BF16),
        scratch_shapes=[pltpu.VMEM((heads, tk, tq), F32), pltpu.VMEM((heads, tk, tq), F32),
                        pltpu.VMEM((heads, 1, tq), F32), pltpu.VMEM((heads, 1, tq), F32)],
        compiler_params=_params("parallel", "parallel", "arbitrary"),
        name="mla_attn",
    )(qm, km, vmt)


def _mem_kv_kernel(mem_ref, g_ref, w_ref, k_ref, v_ref):
    kv = _dot(_rms(mem_ref[0], g_ref[...]).astype(BF16), w_ref[...])
    k_ref[0] = kv[:, :MEM_WIDTH].astype(BF16)
    v_ref[0] = kv[:, MEM_WIDTH:].astype(BF16)


def _mem_kv_call(mem, g, w_kv):
    b, n_mem, d = mem.shape
    blk = pl.BlockSpec((1, n_mem, MEM_WIDTH), lambda bi: (bi, 0, 0))
    return pl.pallas_call(
        _mem_kv_kernel,
        grid=(b,),
        in_specs=[pl.BlockSpec((1, n_mem, d), lambda bi: (bi, 0, 0)),
                  pl.BlockSpec(g.shape, lambda bi: (0, 0)),
                  pl.BlockSpec(w_kv.shape, lambda bi: (0, 0))],
        out_specs=[blk, blk],
        out_shape=[jax.ShapeDtypeStruct((b, n_mem, MEM_WIDTH), BF16)] * 2,
        compiler_params=_params("parallel"),
        name="mem_kv",
    )(mem, g, w_kv)


def _store_tile_rows(ref, x):
    n = x.shape[0]
    chunks = x.shape[1] // LANES
    for c in range(chunks):
        ref[pl.ds(c, n, stride=chunks), :] = x[:, c * LANES:(c + 1) * LANES]


def _load_tile_rows(ref, chunks):
    n = ref.shape[0] // chunks
    return jnp.concatenate([ref[pl.ds(c, n, stride=chunks), :] for c in range(chunks)], axis=1)


def _tile_row(ref, r, chunks):
    return ref.at[pl.ds(pl.multiple_of(r * chunks, chunks), chunks)]


def _mix_kernel(x_ref, d_ref, m_ref, gm_ref, wo_ref, gc_ref, wq_ref, km_ref, vm_ref, wmo_ref,
                gf_ref, wr_ref, br_ref, x2_ref, h3_ref, r_ref, rt_ref, cnt_ref, run_ref):
    rows = x_ref.shape[0]

    @pl.when(pl.program_id(0) == 0)
    def _():
        run_ref[...] = jnp.zeros_like(run_ref)

    mla_n = _rms(m_ref[...].astype(F32), gm_ref[...]).astype(BF16)
    x1 = (x_ref[...] + _dot(d_ref[...], wo_ref[0:DIFF_WIDTH, :])
          + _dot(mla_n, wo_ref[DIFF_WIDTH:DIFF_WIDTH + MLA_WIDTH, :]))

    qx = _dot(_rms(x1, gc_ref[...]).astype(BF16), wq_ref[...])
    heads = []
    for h in range(MEM_HEADS):
        sl = slice(h * MEM_HEAD_DIM, (h + 1) * MEM_HEAD_DIM)
        sc = lax.dot_general(qx[:, sl].astype(BF16), km_ref[0, :, sl], _NT,
                             preferred_element_type=F32) * MEM_HEAD_DIM ** -0.5
        p = jnp.exp(sc - jnp.max(sc, axis=-1, keepdims=True))
        o = _dot(p.astype(BF16), vm_ref[0, :, sl]) * (1.0 / jnp.sum(p, axis=-1, keepdims=True))
        heads.append(o.astype(BF16))
    x2 = x1 + _dot(jnp.concatenate(heads, axis=1), wmo_ref[...])
    x2_ref[...] = x2

    h3 = _rms(x2, gf_ref[...])
    _store_tile_rows(h3_ref, h3)
    h_hi = h3.astype(BF16)
    h_lo = (h3 - h_hi.astype(F32)).astype(BF16)
    logits = (_dot(h_hi, wr_ref[0]) + _dot(h_lo, wr_ref[0]) + _dot(h_hi, wr_ref[1])
              + br_ref[...])
    lane = lax.broadcasted_iota(jnp.int32, (rows, LANES), 1).astype(F32)
    far = float(4 * LANES)

    def first_argmax(vals):
        top = jnp.max(vals, axis=-1, keepdims=True)
        return top, jnp.min(jnp.where(vals == top, lane, far), axis=-1, keepdims=True)

    g_logits = jnp.where(lane < N_GROUPS, logits, MASK_VALUE)
    g_top, g_sel = first_argmax(g_logits)
    g_gate = 1.0 / jnp.sum(jnp.exp(g_logits - g_top), axis=-1, keepdims=True)
    lo = ROUTER_LANE0 + EXPERTS_PER_GROUP * g_sel
    e_logits = jnp.where(lane >= lo, jnp.where(lane < lo + EXPERTS_PER_GROUP, logits, MASK_VALUE),
                         MASK_VALUE)
    top1, lane1 = first_argmax(e_logits)
    top2, lane2 = first_argmax(jnp.where(lane == lane1, MASK_VALUE, e_logits))
    t = jnp.exp(top2 - top1)
    gate1 = g_gate / (1.0 + t)
    gate2 = gate1 * t

    hot = jnp.where(lane == lane1, 1.0, 0.0) + jnp.where(lane == lane2, 1.0, 0.0)
    tri = (lax.broadcasted_iota(jnp.int32, (rows, rows), 1)
           < lax.broadcasted_iota(jnp.int32, (rows, rows), 0))
    before = _dot(jnp.where(tri, 1.0, 0.0).astype(BF16), hot.astype(BF16)) + run_ref[0:1, :]
    rank1 = jnp.sum(jnp.where(lane == lane1, before, 0.0), axis=-1, keepdims=True)
    rank2 = jnp.sum(jnp.where(lane == lane2, before, 0.0), axis=-1, keepdims=True)
    run_ref[...] = run_ref[...] + jnp.sum(hot, axis=0, keepdims=True)
    cnt_ref[...] = run_ref[...]

    fields = (lane1 - ROUTER_LANE0, lane2 - ROUTER_LANE0, gate1, gate2, rank1, rank2)
    out = jnp.zeros((rows, LANES), F32)
    for i, f in enumerate(fields):
        out = jnp.where(lane == float(i), f, out)
    r_ref[...] = out
    rt_ref[...] = out.T[0:8, :]


def _mix_call(x2d, diff, mla, gm, wo, gc, wq, kmem, vmem, wmo, gf, wr, br, seq):
    t, d = x2d.shape
    rows = min(MIX_ROWS, seq)
    per_seq = seq // rows
    row = lambda i: (i, 0)
    fixed = lambda i: (0, 0)
    full = lambda a: pl.BlockSpec(a.shape, lambda i: (0,) * a.ndim)
    memblk = pl.BlockSpec((1,) + kmem.shape[1:], lambda i: (i // per_seq, 0, 0))
    return pl.pallas_call(
        _mix_kernel,
        grid=(t // rows,),
        in_specs=[pl.BlockSpec((rows, d), row), pl.BlockSpec((rows, DIFF_WIDTH), row),
                  pl.BlockSpec((rows, MLA_WIDTH), row), full(gm), full(wo), full(gc), full(wq),
                  memblk, memblk, full(wmo), full(gf), full(wr), full(br)],
        out_specs=[pl.BlockSpec((rows, d), row), pl.BlockSpec((rows * d // LANES, LANES), row),
                   pl.BlockSpec((rows, LANES), row), pl.BlockSpec((8, rows), lambda i: (0, i)),
                   pl.BlockSpec((8, LANES), fixed)],
        out_shape=[jax.ShapeDtypeStruct((t, d), F32),
                   jax.ShapeDtypeStruct((t * d // LANES, LANES), F32),
                   jax.ShapeDtypeStruct((t, LANES), F32), jax.ShapeDtypeStruct((8, t), F32),
                   jax.ShapeDtypeStruct((8, LANES), F32)],
        scratch_shapes=[pltpu.VMEM((8, LANES), F32)],
        compiler_params=_params("arbitrary"),
        name="mix",
    )(x2d, diff, mla, gm, wo, gc, wq, kmem, vmem, wmo, gf, wr, br)


def _dispatch_kernel(d0_ref, d1_ref, pad_ref, h_ref, xs_ref, sem):
    rows = d0_ref.shape[-1]
    pads = pad_ref.shape[-1]
    base = pl.program_id(0) * rows

    def issue(r, c):
        src = _tile_row(h_ref, base + r, ROW_CHUNKS)
        pltpu.make_async_copy(src, _tile_row(xs_ref, d0_ref[0, 0, r], ROW_CHUNKS), sem).start(
            priority=0)
        pltpu.make_async_copy(src, _tile_row(xs_ref, d1_ref[0, 0, r], ROW_CHUNKS), sem).start(
            priority=1)
        return c

    def fill(r, c):
        pltpu.make_async_copy(_tile_row(h_ref, base + lax.rem(r, rows), ROW_CHUNKS),
                              _tile_row(xs_ref, pad_ref[0, 0, r], ROW_CHUNKS), sem).start()
        return c

    lax.fori_loop(0, rows, issue, 0, unroll=8)
    lax.fori_loop(0, pads, fill, 0, unroll=8)
    counts = [rows, rows] + [rows] * (pads // rows) + [pads % rows] * (pads % rows > 0)
    for n in counts:
        pltpu.make_async_copy(h_ref.at[pl.ds(0, n * ROW_CHUNKS)],
                              xs_ref.at[pl.ds(0, n * ROW_CHUNKS)], sem).wait()


def _dispatch_call(d0, d1, pad_dest, h3, n_slots, rows):
    t = h3.shape[0] // ROW_CHUNKS
    steps = t // rows
    pads = pad_dest.shape[0] // steps
    assert pads * steps == pad_dest.shape[0]
    d0, d1 = d0.reshape(steps, 1, rows), d1.reshape(steps, 1, rows)
    idx = pl.BlockSpec((1, 1, rows), lambda i: (i, 0, 0), memory_space=pltpu.SMEM)
    return pl.pallas_call(
        _dispatch_kernel,
        grid=(steps,),
        in_specs=[idx, idx,
                  pl.BlockSpec((1, 1, pads), lambda i: (i, 0, 0), memory_space=pltpu.SMEM),
                  pl.BlockSpec(memory_space=pl.ANY)],
        out_specs=pl.BlockSpec(memory_space=pl.ANY),
        out_shape=jax.ShapeDtypeStruct((n_slots * ROW_CHUNKS, LANES), h3.dtype),
        scratch_shapes=[pltpu.SemaphoreType.DMA(())],
        compiler_params=_params("arbitrary"),
        name="dispatch",
    )(d0, d1, pad_dest.reshape(steps, 1, pads), h3)


def _expert_kernel(be_ref, x_ref, wg_ref, wu_ref, wd_ref, y_ref, wg_bf, wu_bf, wd_bf):
    i = pl.program_id(0)

    @pl.when((i == 0) | (be_ref[i] != be_ref[jnp.maximum(i - 1, 0)]))
    def _():
        wg_bf[...] = wg_ref[0].astype(BF16)
        wu_bf[...] = wu_ref[0].astype(BF16)
        wd_bf[...] = wd_ref[0].astype(BF16)

    x = _load_tile_rows(x_ref, ROW_CHUNKS).astype(BF16)
    gate = _dot(x, wg_bf[...])
    up = _dot(x, wu_bf[...])
    hidden = gate * (1.0 / (1.0 + jnp.exp(-gate))) * up
    _store_tile_rows(y_ref, _dot(hidden.astype(BF16), wd_bf[...]))


def _expert_call(block_expert, xs, wg, wu, wd, rows):
    _, d, ff = wg.shape
    n_slots = xs.shape[0] // ROW_CHUNKS
    grid_spec = pltpu.PrefetchScalarGridSpec(
        num_scalar_prefetch=1,
        grid=(n_slots // rows,),
        in_specs=[pl.BlockSpec((rows * ROW_CHUNKS, LANES), lambda i, be: (i, 0)),
                  pl.BlockSpec((1, d, ff), lambda i, be: (be[i], 0, 0)),
                  pl.BlockSpec((1, d, ff), lambda i, be: (be[i], 0, 0)),
                  pl.BlockSpec((1, ff, d), lambda i, be: (be[i], 0, 0))],
        out_specs=pl.BlockSpec((rows * ROW_CHUNKS, LANES), lambda i, be: (i, 0)),
        scratch_shapes=[pltpu.VMEM((d, ff), BF16), pltpu.VMEM((d, ff), BF16),
                        pltpu.VMEM((ff, d), BF16)],
    )
    return pl.pallas_call(
        _expert_kernel,
        grid_spec=grid_spec,
        out_shape=jax.ShapeDtypeStruct(xs.shape, F32),
        compiler_params=_params("arbitrary"),
        name="experts",
    )(block_expert, xs, wg, wu, wd)


def _combine_kernel(d0_ref, d1_ref, n0_ref, n1_ref, x_ref, r_ref, g_ref, y_ref, o_ref, buf0, buf1,
                    sems):
    rows = x_ref.shape[0]
    i = pl.program_id(0)
    slot = i & 1

    def start_gathers(i0_ref, i1_ref, sl):
        def issue(r, c):
            pltpu.make_async_copy(_tile_row(y_ref, i0_ref[0, 0, r], ROW_CHUNKS),
                                  _tile_row(buf0.at[sl], r, ROW_CHUNKS),
                                  sems.at[sl]).start(priority=0)
            pltpu.make_async_copy(_tile_row(y_ref, i1_ref[0, 0, r], ROW_CHUNKS),
                                  _tile_row(buf1.at[sl], r, ROW_CHUNKS),
                                  sems.at[sl]).start(priority=1)
            return c

        lax.fori_loop(0, rows, issue, 0, unroll=8)

    @pl.when(i == 0)
    def _():
        start_gathers(d0_ref, d1_ref, slot)

    @pl.when(i + 1 < pl.num_programs(0))
    def _():
        start_gathers(n0_ref, n1_ref, 1 - slot)

    for buf in (buf0, buf1):
        pltpu.make_async_copy(y_ref.at[pl.ds(0, rows * ROW_CHUNKS)], buf.at[slot],
                              sems.at[slot]).wait()
    gates = r_ref[...]
    x3 = (x_ref[...] + gates[:, 2:3] * _load_tile_rows(buf0.at[slot], ROW_CHUNKS)
          + gates[:, 3:4] * _load_tile_rows(buf1.at[slot], ROW_CHUNKS))
    o_ref[...] = _rms(x3, g_ref[...])


def _combine_call(d0, d1, x2, router, g, y, rows):
    t, d = x2.shape
    d0, d1 = d0.reshape(t // rows, 1, rows), d1.reshape(t // rows, 1, rows)
    last = t // rows - 1
    idx = pl.BlockSpec((1, 1, rows), lambda i: (i, 0, 0), memory_space=pltpu.SMEM)
    nxt = pl.BlockSpec((1, 1, rows), lambda i: (jnp.minimum(i + 1, last), 0, 0),
                       memory_space=pltpu.SMEM)
    row = lambda i: (i, 0)
    return pl.pallas_call(
        _combine_kernel,
        grid=(t // rows,),
        in_specs=[idx, idx, nxt, nxt, pl.BlockSpec((rows, d), row),
                  pl.BlockSpec((rows, LANES), row), pl.BlockSpec(g.shape, lambda i: (0, 0)),
                  pl.BlockSpec(memory_space=pl.ANY)],
        out_specs=pl.BlockSpec((rows, d), row),
        out_shape=jax.ShapeDtypeStruct((t, d), F32),
        scratch_shapes=[pltpu.VMEM((2, rows * ROW_CHUNKS, LANES), F32),
                        pltpu.VMEM((2, rows * ROW_CHUNKS, LANES), F32),
                        pltpu.SemaphoreType.DMA((2,))],
        compiler_params=_params("arbitrary"),
        name="combine",
    )(d0, d1, d0, d1, x2, router, g, y)


def _slab(cols, lo):
    return jnp.pad(cols, ((0, 0), (lo, MLA_SLAB - lo - cols.shape[1])))


def _layer_weights(w_in, w_uq, w_ukv):
    half = MLA_ROPE_DIM // 2
    o = 3 * DIFF_WIDTH + MLA_Q_RANK + MLA_KV_RANK
    kpe1, kpe2 = w_in[:, o:o + half], w_in[:, o + half:o + 2 * half]
    w_all = jnp.concatenate(
        [w_in[:, :o],
         _slab(jnp.concatenate([kpe1, kpe2], axis=1), MLA_NOPE_DIM),
         _slab(jnp.concatenate([kpe2, kpe1], axis=1), MLA_NOPE_DIM)], axis=1).astype(BF16)
    q_main, q_swap, k_nope, v_cols = [], [], [], []
    for h in range(MLA_HEADS):
        c = w_uq[:, h * MLA_QK_DIM:(h + 1) * MLA_QK_DIM]
        pe1 = c[:, MLA_NOPE_DIM:MLA_NOPE_DIM + half]
        pe2 = c[:, MLA_NOPE_DIM + half:]
        q_main.append(_slab(c, 0))
        q_swap.append(_slab(jnp.concatenate([pe2, pe1], axis=1), MLA_NOPE_DIM))
        kv = w_ukv[:, h * (MLA_NOPE_DIM + MLA_V_DIM):(h + 1) * (MLA_NOPE_DIM + MLA_V_DIM)]
        k_nope.append(_slab(kv[:, :MLA_NOPE_DIM], 0))
        v_cols.append(kv[:, MLA_NOPE_DIM:])
    wq2 = jnp.concatenate(q_main + q_swap, axis=1).astype(BF16)
    wkv2 = jnp.concatenate(k_nope + v_cols, axis=1).astype(BF16)
    return w_all, wq2, wkv2


def _rope_rows():
    inv_freq = ROPE_THETA ** (-jnp.arange(0, MLA_ROPE_DIM, 2, dtype=F32) / MLA_ROPE_DIM)
    half = jnp.ones_like(inv_freq)
    place = lambda a, b: jnp.pad(jnp.concatenate([a, b]),
                                 (MLA_NOPE_DIM, MLA_SLAB - MLA_QK_DIM))
    return jnp.stack([place(inv_freq, inv_freq), place(-half, half)])


def kernel(x, mem, positions, norm_mix_g, w_in, diff_lambda_q1, diff_lambda_k1, diff_lambda_q2, diff_lambda_k2, diff_out_g, mla_q_norm_g, w_mla_uq, mla_kv_norm_g, w_mla_ukv, mla_out_g, w_out, norm_cross_g, norm_mem_g, w_mem_q, w_mem_kv, w_mem_o, norm_ffn_g, w_group_router, b_group_router, w_expert_router, b_expert_router, w_expert_gate, w_expert_up, w_expert_down, norm_final_g):
    b, s, d = x.shape
    t = b * s
    depth = w_in.shape[0]
    row2 = lambda v: v.reshape(1, -1).astype(F32)
    pos_f = positions.astype(F32)
    slopes = 2.0 ** (-8.0 * jnp.arange(1, DIFF_HEADS + 1, dtype=F32) / DIFF_HEADS)
    pos_rel = (pos_f - pos_f[:, :1]).reshape(t, 1)
    slope_rows = jnp.pad(jnp.repeat((LOG2E * slopes).reshape(DIFF_HEADS // 2, 2), 3, axis=1),
                         ((0, 0), (0, LANES - 6)))

    diff_tq, mla_tq = min(DIFF_TQ, s), min(MLA_TQ, s)
    attn_tk = min(ATTN_TK, s)
    blk = EXPERT_ROWS
    n_assign = 2 * t
    n_blocks = -(-(n_assign + N_EXPERTS * (blk - 1)) // blk)
    n_slots = n_blocks * blk

    assert depth == 1, "single-layer problem: the combine kernel applies the final RMSNorm"
    l = 0
    x2d = x.reshape(t, d)
    w_all, wq2, wkv2 = _layer_weights(w_in[l], w_mla_uq[l], w_mla_ukv[l])
    qd, kd, vdt, qm, km, vmt = _proj_call(
        x2d, row2(norm_mix_g[l]), w_all, row2(mla_q_norm_g[l]), wq2,
        row2(mla_kv_norm_g[l]), wkv2, pos_f.reshape(t, 1), _rope_rows(), pos_rel, slope_rows, s)

    lambda_init = 0.8 - 0.6 * math.exp(-0.3 * l)
    lam = (jnp.exp(jnp.sum(diff_lambda_q1[l] * diff_lambda_k1[l]))
           - jnp.exp(jnp.sum(diff_lambda_q2[l] * diff_lambda_k2[l])) + lambda_init)
    gain_col = jnp.tile(diff_out_g[l] * (1.0 - lambda_init), 2).reshape(-1, 1).astype(F32)
    shp = lambda a: a.reshape(b, s, a.shape[-1])
    diff_out = _diff_attn_call(lam.reshape(1).astype(F32), shp(qd), shp(kd), vdt, gain_col,
                               diff_tq, attn_tk)
    mla_out = _mla_attn_call(shp(qm), shp(km), vmt, mla_tq, attn_tk, MLA_HEADS_PER_STEP)

    kmem, vmem = _mem_kv_call(mem, row2(norm_mem_g[l]), w_mem_kv[l].astype(BF16))
    w_router = jnp.pad(
        jnp.concatenate([w_group_router[l], w_expert_router[l]], axis=1),
        ((0, 0), (0, LANES - N_GROUPS - N_EXPERTS)))
    w_router_hi = w_router.astype(BF16)
    w_router = jnp.stack([w_router_hi, (w_router - w_router_hi.astype(F32)).astype(BF16)])
    b_router = jnp.pad(jnp.concatenate([b_group_router[l], b_expert_router[l]]),
                       (0, LANES - N_GROUPS - N_EXPERTS))
    x2, h3, router, router_t, counts = _mix_call(
        x2d, diff_out.reshape(t, DIFF_WIDTH), mla_out.reshape(t, MLA_WIDTH),
        row2(mla_out_g[l]), w_out[l].astype(BF16), row2(norm_cross_g[l]),
        w_mem_q[l].astype(BF16), kmem, vmem, w_mem_o[l].astype(BF16),
        row2(norm_ffn_g[l]), w_router, row2(b_router), s)

    cnt = counts[0, ROUTER_LANE0:ROUTER_LANE0 + N_EXPERTS].astype(jnp.int32)
    padded = ((cnt + blk - 1) // blk) * blk
    seg_end = jnp.cumsum(padded)
    seg_start = seg_end - padded
    ridx = router_t.astype(jnp.int32)
    expert_ids = jnp.arange(N_EXPERTS, dtype=jnp.int32)[:, None]

    def slot_of(choice):
        start = jnp.sum(jnp.where(ridx[choice][None, :] == expert_ids, seg_start[:, None], 0),
                        axis=0)
        return start + ridx[4 + choice]

    dest0, dest1 = slot_of(0), slot_of(1)
    block_start = jnp.arange(n_blocks, dtype=jnp.int32) * blk
    block_expert = jnp.minimum(
        jnp.sum((seg_end[None, :] <= block_start[:, None]).astype(jnp.int32), axis=1),
        N_EXPERTS - 1)

    gap_first = jnp.concatenate([seg_start + cnt, seg_end[-1:]])
    gap_len = jnp.concatenate([padded - cnt, n_slots - seg_end[-1:]])
    gap_end = jnp.cumsum(gap_len)
    k = jnp.arange(n_slots - n_assign, dtype=jnp.int32)
    gap_of_k = jnp.sum((gap_end[None, :] <= k[:, None]).astype(jnp.int32), axis=1)
    gap_ids = jnp.arange(N_EXPERTS + 1, dtype=jnp.int32)[None, :]
    pad_dest = k + jnp.sum(jnp.where(gap_of_k[:, None] == gap_ids,
                                     (gap_first - (gap_end - gap_len))[None, :], 0), axis=1)

    assert d == ROW_CHUNKS * LANES
    xs = _dispatch_call(dest0, dest1, pad_dest, h3, n_slots, min(DISPATCH_ROWS, t))
    y = _expert_call(block_expert, xs, w_expert_gate[l], w_expert_up[l], w_expert_down[l], blk)
    out = _combine_call(dest0, dest1, x2, router, row2(norm_final_g), y, min(COMBINE_ROWS, t))
    return out.reshape(b, s, d)
```
